```python
import jax, jax.numpy as jnp
from jax import lax
import numpy as np

D_MODEL = 2048
BATCH = 16
SEQ = 2048
DEPTH = 1
DEC_BATCH = 16
DEC_SEQ = 64
PAST_LEN = 1024

CHUNK = 64
N_PAST_CHUNKS = 8
A_REACH = N_PAST_CHUNKS * CHUNK
HEAD_DIM = 128
H_A = 8
H_B = 8
W_A = H_A * HEAD_DIM
W_B = H_B * HEAD_DIM
REL_CLIP = 128
N_REL = 2 * REL_CLIP + 1
D_FF = 4 * D_MODEL
D_PLE = 256
Q_BLOCK = 128
RMS_EPS = 1e-6
D_IN = 3 * W_A + 3 * W_B + H_B + 2 * D_MODEL
SPLITS = [int(s) for s in np.cumsum([W_A, W_A, W_A, W_B, W_B, W_B, H_B, D_MODEL])]
SCALE = HEAD_DIM ** -0.5
NEG = -1e30

kernel_name = 'hybrid_chunkband_fox_stream_step'


def rmsnorm(x, g):
    x32 = x.astype(jnp.float32)
    y = x32 * lax.rsqrt(jnp.mean(x32 * x32, axis=-1, keepdims=True) + RMS_EPS)
    return (y * g.astype(jnp.float32)).astype(x.dtype)


def heads(t, h):
    return t.reshape(t.shape[:-1] + (h, HEAD_DIM))


def mix_inputs(h, g_mix, w_in, b_f):
    n = rmsnorm(h, g_mix)
    u = n @ w_in
    qa, ka, va, qb, kb, vb, fl, ga, gb = jnp.split(u, SPLITS, axis=-1)
    logf = jax.nn.log_sigmoid(fl.astype(jnp.float32) + b_f.astype(jnp.float32))
    return (heads(qa, H_A), heads(ka, H_A), heads(va, H_A),
            heads(qb, H_B), heads(kb, H_B), heads(vb, H_B), logf, ga, gb)


def rel_bias_lookup(table, dist):
    idx = np.clip(dist, -REL_CLIP, REL_CLIP) + REL_CLIP
    return jnp.take(table.astype(jnp.float32), jnp.asarray(idx), axis=1)


def band_attn_prompt(q, k, v, table):
    b, s, h, d = q.shape
    nc = s // CHUNK
    n_off = N_PAST_CHUNKS + 1
    qc = q.reshape(b, nc, CHUNK, h, d)
    pad = jnp.zeros((b, A_REACH, h, d), k.dtype)
    kp = jnp.concatenate([pad, k], axis=1).reshape(b, nc + N_PAST_CHUNKS, CHUNK, h, d)
    vp = jnp.concatenate([pad.astype(v.dtype), v], axis=1).reshape(b, nc + N_PAST_CHUNKS, CHUNK, h, d)
    scores = jnp.concatenate(
        [jnp.einsum('bcihd,bcjhd->bhcij', qc, kp[:, j:j + nc]) for j in range(n_off)],
        axis=-1).astype(jnp.float32) * SCALE
    kk = np.arange(n_off * CHUNK)
    k_chunk = kk // CHUNK - N_PAST_CHUNKS
    k_pos = k_chunk * CHUNK + kk % CHUNK
    q_pos = np.arange(CHUNK)
    bias = rel_bias_lookup(table, q_pos[:, None] - k_pos[None, :])
    valid = (np.arange(nc)[:, None] + k_chunk[None, :]) >= 0
    scores = jnp.where(valid[None, None, :, None, :], scores + bias[:, None], NEG)
    probs = jax.nn.softmax(scores, axis=-1).astype(v.dtype)
    out = sum(jnp.einsum('bhcij,bcjhd->bcihd', probs[..., j * CHUNK:(j + 1) * CHUNK], vp[:, j:j + nc])
              for j in range(n_off))
    return out.reshape(b, s, h, d)


def band_attn_sample(q, k_new, v_new, k_cache, v_cache, table):
    t = q.shape[1]
    L = k_cache.shape[1]
    k = jnp.concatenate([k_cache.astype(k_new.dtype), k_new], axis=1)
    v = jnp.concatenate([v_cache.astype(v_new.dtype), v_new], axis=1)
    scores = jnp.einsum('bqhd,bkhd->bhqk', q, k).astype(jnp.float32) * SCALE
    k_pos = np.concatenate([np.arange(L) - L, np.arange(t)])
    bias = rel_bias_lookup(table, np.arange(t)[:, None] - k_pos[None, :])
    probs = jax.nn.softmax(scores + bias[None], axis=-1).astype(v.dtype)
    out = jnp.einsum('bhqk,bkhd->bqhd', probs, v)
    return out, k[:, -L:], v[:, -L:]


def fox_prompt(q, k, v, logf):
    b, s, h, d = q.shape
    nb = s // Q_BLOCK
    Fk = jnp.cumsum(logf, axis=1).transpose(0, 2, 1)
    qb = q.reshape(b, nb, Q_BLOCK, h, d).transpose(1, 0, 2, 3, 4)
    Fq = Fk.reshape(b, h, nb, Q_BLOCK).transpose(2, 0, 1, 3)
    qpos = jnp.arange(s).reshape(nb, Q_BLOCK)
    kpos = jnp.arange(s)

    def block(args):
        qi, fqi, pi = args
        sc = (jnp.einsum('bqhd,bkhd->bhqk', qi, k).astype(jnp.float32) * SCALE
              + (fqi[..., None] - Fk[:, :, None, :]))
        sc = jnp.where(pi[:, None] >= kpos[None, :], sc, NEG)
        p = jax.nn.softmax(sc, axis=-1).astype(v.dtype)
        return jnp.einsum('bhqk,bkhd->bqhd', p, v)

    out = lax.map(block, (qb, Fq, qpos))
    return out.transpose(1, 0, 2, 3, 4).reshape(b, s, h, d)


def fox_sample(q, k_new, v_new, lf_new, k_cache, v_cache, lf_cache):
    t = q.shape[1]
    L = k_cache.shape[1]
    k = jnp.concatenate([k_cache.astype(k_new.dtype), k_new], axis=1)
    v = jnp.concatenate([v_cache.astype(v_new.dtype), v_new], axis=1)
    lf = jnp.concatenate([lf_cache.astype(jnp.float32), lf_new], axis=1)
    F = jnp.cumsum(lf, axis=1).transpose(0, 2, 1)
    sc = (jnp.einsum('bqhd,bkhd->bhqk', q, k).astype(jnp.float32) * SCALE
          + (F[:, :, L:, None] - F[:, :, None, :]))
    mask = (L + np.arange(t))[:, None] >= np.arange(L + t)[None, :]
    sc = jnp.where(mask[None, None], sc, NEG)
    p = jax.nn.softmax(sc, axis=-1).astype(v.dtype)
    return jnp.einsum('bhqk,bkhd->bqhd', p, v)


def finish_layer(h, oa, ob, ga, gb, ple, w_a_proj, w_b_proj, w_o, g_mlp, w_up, w_down,
                 g_ple, w_ple_gate, w_ple_proj):
    ya = oa.reshape(oa.shape[:2] + (W_A,)) @ w_a_proj
    yb = ob.reshape(ob.shape[:2] + (W_B,)) @ w_b_proj
    m = jax.nn.sigmoid(ga) * ya + jax.nn.sigmoid(gb) * yb
    h = h + m @ w_o
    a = rmsnorm(h, g_mlp) @ w_up
    h = h + jnp.square(jax.nn.relu(a)) @ w_down
    gate = jax.nn.sigmoid(rmsnorm(h, g_ple) @ w_ple_gate)
    return h + (ple.astype(h.dtype) @ w_ple_proj) * gate


def setup_inputs(seed: int = 0) -> dict:
    key = jax.random.key(seed)
    ks = jax.random.split(key, 32)
    f32 = jnp.float32
    la = min(A_REACH, PAST_LEN)

    def nrm(k, shape, scale=1.0):
        return jax.random.normal(k, shape, f32) * scale

    def gain(k, shape):
        return 1.0 + 0.05 * jax.random.normal(k, shape, f32)

    return {
        'x_prompt': nrm(ks[0], (BATCH, SEQ, D_MODEL)),
        'x_sample': nrm(ks[1], (DEC_BATCH, DEC_SEQ, D_MODEL)),
        'p_prompt': nrm(ks[2], (DEPTH, BATCH, SEQ, D_PLE)),
        'p_sample': nrm(ks[3], (DEPTH, DEC_BATCH, DEC_SEQ, D_PLE)),
        'cache_a_k': nrm(ks[4], (DEPTH, DEC_BATCH, la, H_A, HEAD_DIM)),
        'cache_a_v': nrm(ks[5], (DEPTH, DEC_BATCH, la, H_A, HEAD_DIM)),
        'cache_b_k': nrm(ks[6], (DEPTH, DEC_BATCH, PAST_LEN, H_B, HEAD_DIM)),
        'cache_b_v': nrm(ks[7], (DEPTH, DEC_BATCH, PAST_LEN, H_B, HEAD_DIM)),
        'cache_b_logf': jax.nn.log_sigmoid(3.0 + nrm(ks[8], (DEPTH, DEC_BATCH, PAST_LEN, H_B))),
        'g_mix': gain(ks[9], (DEPTH, D_MODEL)),
        'w_in': nrm(ks[10], (DEPTH, D_MODEL, D_IN), D_MODEL ** -0.5),
        'b_f': 3.0 + 0.5 * jax.random.normal(ks[11], (DEPTH, H_B), f32),
        'rel_bias': nrm(ks[12], (DEPTH, H_A, N_REL), 0.1),
        'w_a_proj': nrm(ks[13], (DEPTH, W_A, D_MODEL), W_A ** -0.5),
        'w_b_proj': nrm(ks[14], (DEPTH, W_B, D_MODEL), W_B ** -0.5),
        'w_o': nrm(ks[15], (DEPTH, D_MODEL, D_MODEL), D_MODEL ** -0.5),
        'g_mlp': gain(ks[16], (DEPTH, D_MODEL)),
        'w_up': nrm(ks[17], (DEPTH, D_MODEL, D_FF), D_MODEL ** -0.5),
        'w_down': nrm(ks[18], (DEPTH, D_FF, D_MODEL), D_FF ** -0.5),
        'g_ple': gain(ks[19], (DEPTH, D_MODEL)),
        'w_ple_gate': nrm(ks[20], (DEPTH, D_MODEL, D_MODEL), D_MODEL ** -0.5),
        'w_ple_proj': nrm(ks[21], (DEPTH, D_PLE, D_MODEL), D_PLE ** -0.5),
        'g_final': gain(ks[22], (D_MODEL,)),
    }


def reference(x_prompt, x_sample, p_prompt, p_sample, cache_a_k, cache_a_v, cache_b_k,
              cache_b_v, cache_b_logf, g_mix, w_in, b_f, rel_bias, w_a_proj, w_b_proj, w_o,
              g_mlp, w_up, w_down, g_ple, w_ple_gate, w_ple_proj, g_final):
    sa_kp, sa_vp, sb_kp, sb_vp, sb_lp = [], [], [], [], []
    sa_ks, sa_vs, sb_ks, sb_vs, sb_ls = [], [], [], [], []
    hp, hs = x_prompt, x_sample
    for i in range(DEPTH):
        qa, ka, va, qb, kb, vb, lf, ga, gb = mix_inputs(hp, g_mix[i], w_in[i], b_f[i])
        oa = band_attn_prompt(qa, ka, va, rel_bias[i])
        ob = fox_prompt(qb, kb, vb, lf)
        la = min(A_REACH, hp.shape[1])
        sa_kp.append(ka[:, -la:])
        sa_vp.append(va[:, -la:])
        sb_kp.append(kb)
        sb_vp.append(vb)
        sb_lp.append(lf)
        hp = finish_layer(hp, oa, ob, ga, gb, p_prompt[i], w_a_proj[i], w_b_proj[i], w_o[i],
                          g_mlp[i], w_up[i], w_down[i], g_ple[i], w_ple_gate[i], w_ple_proj[i])
        qa, ka, va, qb, kb, vb, lf, ga, gb = mix_inputs(hs, g_mix[i], w_in[i], b_f[i])
        oa, nka, nva = band_attn_sample(qa, ka, va, cache_a_k[i], cache_a_v[i], rel_bias[i])
        ob = fox_sample(qb, kb, vb, lf, cache_b_k[i], cache_b_v[i], cache_b_logf[i])
        sa_ks.append(nka)
        sa_vs.append(nva)
        sb_ks.append(kb)
        sb_vs.append(vb)
        sb_ls.append(lf)
        hs = finish_layer(hs, oa, ob, ga, gb, p_sample[i], w_a_proj[i], w_b_proj[i], w_o[i],
                          g_mlp[i], w_up[i], w_down[i], g_ple[i], w_ple_gate[i], w_ple_proj[i])
    y_prompt = rmsnorm(hp, g_final)
    y_sample = rmsnorm(hs, g_final)
    return (y_prompt, y_sample,
            jnp.stack(sa_kp), jnp.stack(sa_vp), jnp.stack(sb_kp), jnp.stack(sb_vp), jnp.stack(sb_lp),
            jnp.stack(sa_ks), jnp.stack(sa_vs), jnp.stack(sb_ks), jnp.stack(sb_vs), jnp.stack(sb_ls))
```

```python
import functools

import numpy as np
import jax
import jax.numpy as jnp
from jax import lax
from jax.experimental import pallas as pl
from jax.experimental.pallas import tpu as pltpu

F32 = jnp.float32
BF16 = jnp.bfloat16

D_MODEL = 2048
CHUNK = 64
N_PAST_CHUNKS = 8
A_REACH = N_PAST_CHUNKS * CHUNK
HEAD_DIM = 128
N_HEADS = 8
W_MIX = N_HEADS * HEAD_DIM
REL_CLIP = 128
D_FF = 4 * D_MODEL
D_PLE = 256
RMS_EPS = 1e-6
SCALE = HEAD_DIM ** -0.5
NEG = -1e30

V7X_LANES = 128
V7X_VMEM_LIMIT_CAP = 56 * 1024 * 1024
MIB = 1024 * 1024

NORM_ROWS = 128
BAND_TQ = 128
BAND_W = A_REACH + BAND_TQ
FOX_T = 256
SEG_COLS = 1024


def _vmem_limit(block_bytes, scratch_bytes, temp_bytes):
    est = 2 * block_bytes + scratch_bytes + temp_bytes
    return int(min(max(est, 16 * MIB), V7X_VMEM_LIMIT_CAP))


def _rmsnorm_rows(x, g):
    ms = jnp.mean(x * x, axis=-1, keepdims=True)
    return (x * lax.rsqrt(ms + RMS_EPS)) * g


def _norm_to_bf16(x_ref, g_ref, n_ref):
    rows = min(NORM_ROWS, x_ref.shape[0])

    def body(c, carry):
        r = pl.ds(pl.multiple_of(c * rows, rows), rows)
        n_ref[r, :] = _rmsnorm_rows(x_ref[r, :], g_ref[...]).astype(BF16)
        return carry

    lax.fori_loop(0, x_ref.shape[0] // rows, body, 0)


def _inproj_kernel(x_ref, g_ref, w_ref, wf_ref, bf_ref,
                   qkv_ref, ka_ref, va_ref, kb_ref, vb_ref, gate_ref, lf_ref,
                   n_ref, acc_ref, *, nseg, a_rows, a_period):
    i = pl.program_id(0)
    j = pl.program_id(1)
    tm = x_ref.shape[0]

    @pl.when(j == 0)
    def _():
        _norm_to_bf16(x_ref, g_ref, n_ref)
        z = jnp.dot(n_ref[...], wf_ref[...], preferred_element_type=F32) + bf_ref[...]
        lf = jnp.minimum(z, 0.0) - jnp.log1p(jnp.exp(-jnp.abs(z)))
        lf_ref[...] = lf[:, :N_HEADS]

    acc_ref[...] = jnp.dot(n_ref[...], w_ref[...], preferred_element_type=F32)

    @pl.when(j < 6 * nseg)
    def _():
        qkv_ref[...] = acc_ref[...].astype(BF16)

    is_tail = (i % a_period) == (a_period - 1)

    @pl.when(jnp.logical_and(is_tail, jnp.logical_and(j >= nseg, j < 2 * nseg)))
    def _():
        ka_ref[...] = acc_ref[tm - a_rows:, :]

    @pl.when(jnp.logical_and(is_tail, jnp.logical_and(j >= 2 * nseg, j < 3 * nseg)))
    def _():
        va_ref[...] = acc_ref[tm - a_rows:, :]

    @pl.when(jnp.logical_and(j >= 4 * nseg, j < 5 * nseg))
    def _():
        kb_ref[...] = acc_ref[...]

    @pl.when(jnp.logical_and(j >= 5 * nseg, j < 6 * nseg))
    def _():
        vb_ref[...] = acc_ref[...]

    @pl.when(j >= 6 * nseg)
    def _():
        gate_ref[...] = jax.nn.sigmoid(acc_ref[...]).astype(BF16)


def _in_proj(x, g, w_main, w_f, b_f, *, a_rows, a_period, tm, tn):
    t = x.shape[0]
    nseg = SEG_COLS // tn
    nj = 10 * nseg
    n_a = (t // tm) // a_period * a_rows

    def seg_map(first, count):
        return lambda i, j: (i, jnp.clip(j - first * nseg, 0, count * nseg - 1))

    def tail_map(first):
        def index(i, j):
            col = jnp.clip(j - first * nseg, 0, nseg - 1)
            return (i // a_period, jnp.where(i % a_period == a_period - 1, col, 0))
        return index

    blocks = (tm * D_MODEL * 4 + D_MODEL * tn * 2 + D_MODEL * V7X_LANES * 2
              + tm * tn * 2 * 2 + 2 * a_rows * tn * 4 + 2 * tm * tn * 4 + tm * V7X_LANES * 4)
    scratch = tm * D_MODEL * 2 + tm * tn * 4
    kern = functools.partial(_inproj_kernel, nseg=nseg, a_rows=a_rows, a_period=a_period)
    return pl.pallas_call(
        kern,
        grid=(t // tm, nj),
        in_specs=[
            pl.BlockSpec((tm, D_MODEL), lambda i, j: (i, 0)),
            pl.BlockSpec((1, D_MODEL), lambda i, j: (0, 0)),
            pl.BlockSpec((D_MODEL, tn), lambda i, j: (0, j)),
            pl.BlockSpec((D_MODEL, V7X_LANES), lambda i, j: (0, 0)),
            pl.BlockSpec((1, V7X_LANES), lambda i, j: (0, 0)),
        ],
        out_specs=[
            pl.BlockSpec((tm, tn), seg_map(0, 6)),
            pl.BlockSpec((a_rows, tn), tail_map(1)),
            pl.BlockSpec((a_rows, tn), tail_map(2)),
            pl.BlockSpec((tm, tn), seg_map(4, 1)),
            pl.BlockSpec((tm, tn), seg_map(5, 1)),
            pl.BlockSpec((tm, tn), seg_map(6, 4)),
            pl.BlockSpec((tm, N_HEADS), lambda i, j: (i, 0)),
        ],
        out_shape=[
            jax.ShapeDtypeStruct((t, 6 * SEG_COLS), BF16),
            jax.ShapeDtypeStruct((n_a, SEG_COLS), F32),
            jax.ShapeDtypeStruct((n_a, SEG_COLS), F32),
            jax.ShapeDtypeStruct((t, SEG_COLS), F32),
            jax.ShapeDtypeStruct((t, SEG_COLS), F32),
            jax.ShapeDtypeStruct((t, 4 * SEG_COLS), BF16),
            jax.ShapeDtypeStruct((t, N_HEADS), F32),
        ],
        scratch_shapes=[pltpu.VMEM((tm, D_MODEL), BF16), pltpu.VMEM((tm, tn), F32)],
        compiler_params=pltpu.CompilerParams(
            dimension_semantics=("parallel", "arbitrary"),
            vmem_limit_bytes=_vmem_limit(blocks, scratch, 4 * MIB)),
        name="in_proj",
    )(x, g, w_main, w_f, b_f)


def _cumsum_kernel(x_ref, o_ref):
    rows, length = x_ref.shape
    r_i = lax.broadcasted_iota(jnp.int32, (V7X_LANES, V7X_LANES), 0)
    c_i = lax.broadcasted_iota(jnp.int32, (V7X_LANES, V7X_LANES), 1)
    tri = (r_i <= c_i).astype(BF16)
    carry = jnp.zeros((rows, 1), F32)
    for c in range(length // V7X_LANES):
        x = x_ref[:, c * V7X_LANES:(c + 1) * V7X_LANES]
        hi = x.astype(BF16)
        r1 = x - hi.astype(F32)
        mid = r1.astype(BF16)
        lo = (r1 - mid.astype(F32)).astype(BF16)
        blk = (jnp.dot(hi, tri, preferred_element_type=F32)
               + jnp.dot(mid, tri, preferred_element_type=F32)
               + jnp.dot(lo, tri, preferred_element_type=F32)) + carry
        o_ref[:, c * V7X_LANES:(c + 1) * V7X_LANES] = blk
        carry = blk[:, V7X_LANES - 1:V7X_LANES]


def _cumsum_lanes(x):
    return pl.pallas_call(
        _cumsum_kernel,
        out_shape=jax.ShapeDtypeStruct(x.shape, F32),
        name="logf_cumsum",
    )(x)


def _band_prompt_kernel(q_ref, k_ref, v_ref, bias_ref, o_ref, kpad_ref, vpad_ref):
    s_len = q_ref.shape[0]
    zeros = jnp.zeros((A_REACH, HEAD_DIM), BF16)
    kpad_ref[0:A_REACH, :] = zeros
    vpad_ref[0:A_REACH, :] = zeros
    kpad_ref[A_REACH:, :] = k_ref[...]
    vpad_ref[A_REACH:, :] = v_ref[...]
    col = lax.broadcasted_iota(jnp.int32, (BAND_TQ, BAND_W), 1)

    def body(t, carry):
        t0 = pl.multiple_of(t * BAND_TQ, BAND_TQ)
        q = q_ref[pl.ds(t0, BAND_TQ), :]
        kw = kpad_ref[pl.ds(t0, BAND_W), :]
        vw = vpad_ref[pl.ds(t0, BAND_W), :]
        s = lax.dot_general(q, kw, (((1,), (1,)), ((), ())), preferred_element_type=F32)
        s = s * SCALE + bias_ref[0]
        s = jnp.where(col + t0 >= A_REACH, s, NEG)
        m = jnp.max(s, axis=-1, keepdims=True)
        p = jnp.exp(s - m)
        l = jnp.sum(p, axis=-1, keepdims=True)
        o = jnp.dot(p.astype(BF16), vw, preferred_element_type=F32)
        o_ref[pl.ds(t0, BAND_TQ), :] = (o / l).astype(BF16)
        return carry

    lax.fori_loop(0, s_len // BAND_TQ, body, 0)


def _band_prompt(qkv, bias, batch, s_len):
    t = batch * s_len
    blocks = 4 * s_len * HEAD_DIM * 2 + BAND_TQ * BAND_W * 4
    scratch = 2 * (A_REACH + s_len) * HEAD_DIM * 2
    return pl.pallas_call(
        _band_prompt_kernel,
        grid=(batch, N_HEADS),
        in_specs=[
            pl.BlockSpec((s_len, HEAD_DIM), lambda b, h: (b, h)),
            pl.BlockSpec((s_len, HEAD_DIM), lambda b, h: (b, N_HEADS + h)),
            pl.BlockSpec((s_len, HEAD_DIM), lambda b, h: (b, 2 * N_HEADS + h)),
            pl.BlockSpec((1, BAND_TQ, BAND_W), lambda b, h: (h, 0, 0)),
        ],
        out_specs=pl.BlockSpec((s_len, HEAD_DIM), lambda b, h: (b, h)),
        out_shape=jax.ShapeDtypeStruct((t, W_MIX), BF16),
        scratch_shapes=[pltpu.VMEM((A_REACH + s_len, HEAD_DIM), BF16),
                        pltpu.VMEM((A_REACH + s_len, HEAD_DIM), BF16)],
        compiler_params=pltpu.CompilerParams(
            dimension_semantics=("parallel", "parallel"),
            vmem_limit_bytes=_vmem_limit(blocks, scratch, 4 * MIB)),
        name="band_prompt",
    )(qkv, qkv, qkv, bias)


def _band_sample_kernel(q_ref, kn_ref, vn_ref, ck_ref, cv_ref, bias_ref, o_ref):
    n_cache = ck_ref.shape[1]
    for h in range(N_HEADS):
        hs = slice(h * HEAD_DIM, (h + 1) * HEAD_DIM)
        q = q_ref[:, hs]
        kc = ck_ref[0, :, hs].astype(BF16)
        vc = cv_ref[0, :, hs].astype(BF16)
        dn = (((1,), (1,)), ((), ()))
        s1 = lax.dot_general(q, kc, dn, preferred_element_type=F32) * SCALE + bias_ref[h, :, :n_cache]
        s2 = lax.dot_general(q, kn_ref[:, hs], dn, preferred_element_type=F32) * SCALE + bias_ref[h, :, n_cache:]
        m = jnp.maximum(jnp.max(s1, axis=-1, keepdims=True), jnp.max(s2, axis=-1, keepdims=True))
        p1 = jnp.exp(s1 - m)
        p2 = jnp.exp(s2 - m)
        l = jnp.sum(p1, axis=-1, keepdims=True) + jnp.sum(p2, axis=-1, keepdims=True)
        o = (jnp.dot(p1.astype(BF16), vc, preferred_element_type=F32)
             + jnp.dot(p2.astype(BF16), vn_ref[:, hs], preferred_element_type=F32))
        o_ref[:, hs] = (o / l).astype(BF16)


def _band_sample(qkv, cache_k, cache_v, bias, batch, t_new):
    n_cache = cache_k.shape[1]
    blocks = 4 * t_new * W_MIX * 2 + 2 * n_cache * W_MIX * 4 + bias.size * 4
    return pl.pallas_call(
        _band_sample_kernel,
        grid=(batch,),
        in_specs=[
            pl.BlockSpec((t_new, W_MIX), lambda b: (b, 0)),
            pl.BlockSpec((t_new, W_MIX), lambda b: (b, 1)),
            pl.BlockSpec((t_new, W_MIX), lambda b: (b, 2)),
            pl.BlockSpec((1, n_cache, W_MIX), lambda b: (b, 0, 0)),
            pl.BlockSpec((1, n_cache, W_MIX), lambda b: (b, 0, 0)),
            pl.BlockSpec(bias.shape, lambda b: (0, 0, 0)),
        ],
        out_specs=pl.BlockSpec((t_new, W_MIX), lambda b: (b, 0)),
        out_shape=jax.ShapeDtypeStruct((batch * t_new, W_MIX), BF16),
        compiler_params=pltpu.CompilerParams(
            dimension_semantics=("parallel",),
            vmem_limit_bytes=_vmem_limit(blocks, 0, 4 * MIB)),
        name="band_sample",
    )(qkv, qkv, qkv, cache_k, cache_v, bias)


def _fox_step(q, k, v, f_row, m, l, acc, mask):
    s = lax.dot_general(q, k, (((1,), (1,)), ((), ())), preferred_element_type=F32) * SCALE - f_row
    if mask is not None:
        s = jnp.where(mask, s, NEG)
    m_new = jnp.maximum(m, jnp.max(s, axis=-1, keepdims=True))
    alpha = jnp.exp(m - m_new)
    p = jnp.exp(s - m_new)
    l = alpha * l + jnp.sum(p, axis=-1, keepdims=True)
    acc = alpha * acc + jnp.dot(p.astype(BF16), v, preferred_element_type=F32)
    return m_new, l, acc


def _fox_prompt_kernel(q_ref, k_ref, v_ref, f_ref, o_ref):
    s_len = q_ref.shape[0]
    row = lax.broadcasted_iota(jnp.int32, (FOX_T, FOX_T), 0)
    col = lax.broadcasted_iota(jnp.int32, (FOX_T, FOX_T), 1)
    causal = row >= col

    def q_body(qi, carry):
        q0 = pl.multiple_of(qi * FOX_T, FOX_T)
        q = q_ref[pl.ds(q0, FOX_T), :]

        def k_body(ki, state):
            k0 = pl.multiple_of(ki * FOX_T, FOX_T)
            return _fox_step(q, k_ref[pl.ds(k0, FOX_T), :], v_ref[pl.ds(k0, FOX_T), :],
                             f_ref[0, 0, pl.ds(ki, 1), :], *state, None)

        init = (jnp.full((FOX_T, 1), -jnp.inf, F32), jnp.zeros((FOX_T, 1), F32),
                jnp.zeros((FOX_T, HEAD_DIM), F32))
        state = lax.fori_loop(0, qi, k_body, init)
        _, l, acc = _fox_step(q, k_ref[pl.ds(q0, FOX_T), :], v_ref[pl.ds(q0, FOX_T), :],
                              f_ref[0, 0, pl.ds(qi, 1), :], *state, causal)
        o_ref[pl.ds(q0, FOX_T), :] = (acc / l).astype(BF16)
        return carry

    lax.fori_loop(0, s_len // FOX_T, q_body, 0)


def _fox_prompt(qkv, f_rows, batch, s_len):
    t = batch * s_len
    blocks = 4 * s_len * HEAD_DIM * 2 + s_len * 4
    return pl.pallas_call(
        _fox_prompt_kernel,
        grid=(batch, N_HEADS),
        in_specs=[
            pl.BlockSpec((s_len, HEAD_DIM), lambda b, h: (b, 3 * N_HEADS + h)),
            pl.BlockSpec((s_len, HEAD_DIM), lambda b, h: (b, 4 * N_HEADS + h)),
            pl.BlockSpec((s_len, HEAD_DIM), lambda b, h: (b, 5 * N_HEADS + h)),
            pl.BlockSpec((1, 1, s_len // FOX_T, FOX_T), lambda b, h: (b, h, 0, 0)),
        ],
        out_specs=pl.BlockSpec((s_len, HEAD_DIM), lambda b, h: (b, h)),
        out_shape=jax.ShapeDtypeStruct((t, W_MIX), BF16),
        compiler_params=pltpu.CompilerParams(
            dimension_semantics=("parallel", "parallel"),
            vmem_limit_bytes=_vmem_limit(blocks, 0, 4 * MIB)),
        name="fox_prompt",
    )(qkv, qkv, qkv, f_rows)


def _fox_sample_kernel(q_ref, kn_ref, vn_ref, ck_ref, cv_ref, f_ref, o_ref):
    n_cache = ck_ref.shape[1]
    t_new = q_ref.shape[0]
    row = lax.broadcasted_iota(jnp.int32, (t_new, t_new), 0)
    col = lax.broadcasted_iota(jnp.int32, (t_new, t_new), 1)
    causal = row >= col
    for h in range(N_HEADS):
        hs = slice(h * HEAD_DIM, (h + 1) * HEAD_DIM)
        q = q_ref[:, hs]
        init = (jnp.full((t_new, 1), -jnp.inf, F32), jnp.zeros((t_new, 1), F32),
                jnp.zeros((t_new, HEAD_DIM), F32))
        state = _fox_step(q, ck_ref[0, :, hs].astype(BF16), cv_ref[0, :, hs].astype(BF16),
                          f_ref[0, h:h + 1, :n_cache], *init, None)
        _, l, acc = _fox_step(q, kn_ref[:, hs], vn_ref[:, hs],
                              f_ref[0, h:h + 1, n_cache:n_cache + t_new], *state, causal)
        o_ref[:, hs] = (acc / l).astype(BF16)


def _fox_sample(qkv, cache_k, cache_v, f_rows, batch, t_new):
    n_cache = cache_k.shape[1]
    f_len = f_rows.shape[-1]
    blocks = 4 * t_new * W_MIX * 2 + 2 * n_cache * W_MIX * 4 + N_HEADS * f_len * 4
    return pl.pallas_call(
        _fox_sample_kernel,
        grid=(batch,),
        in_specs=[
            pl.BlockSpec((t_new, W_MIX), lambda b: (b, 3)),
            pl.BlockSpec((t_new, W_MIX), lambda b: (b, 4)),
            pl.BlockSpec((t_new, W_MIX), lambda b: (b, 5)),
            pl.BlockSpec((1, n_cache, W_MIX), lambda b: (b, 0, 0)),
            pl.BlockSpec((1, n_cache, W_MIX), lambda b: (b, 0, 0)),
            pl.BlockSpec((1, N_HEADS, f_len), lambda b: (b, 0, 0)),
        ],
        out_specs=pl.BlockSpec((t_new, W_MIX), lambda b: (b, 0)),
        out_shape=jax.ShapeDtypeStruct((batch * t_new, W_MIX), BF16),
        compiler_params=pltpu.CompilerParams(
            dimension_semantics=("parallel",),
            vmem_limit_bytes=_vmem_limit(blocks, 0, 8 * MIB)),
        name="fox_sample",
    )(qkv, qkv, qkv, cache_k, cache_v, f_rows)


def _merge_kernel(oa_ref, ob_ref, gate_ref, x_ref, wa_ref, wb_ref, wo_ref, h_ref):
    ya = jnp.dot(oa_ref[...], wa_ref[...], preferred_element_type=F32)
    yb = jnp.dot(ob_ref[...], wb_ref[...], preferred_element_type=F32)
    m = (gate_ref[:, :D_MODEL].astype(F32) * ya + gate_ref[:, D_MODEL:].astype(F32) * yb)
    h_ref[...] = x_ref[...] + jnp.dot(m.astype(BF16), wo_ref[...], preferred_element_type=F32)


def _resident(shape):
    return pl.BlockSpec(shape, lambda *_: (0,) * len(shape), pipeline_mode=pl.Buffered(1))


def _merge(oa, ob, gates, x, w_a, w_b, w_o, *, tm):
    t = x.shape[0]
    blocks = 2 * tm * W_MIX * 2 + tm * 2 * D_MODEL * 2 + 2 * tm * D_MODEL * 4
    weights = (2 * W_MIX * D_MODEL + D_MODEL * D_MODEL) * 2
    return pl.pallas_call(
        _merge_kernel,
        grid=(t // tm,),
        in_specs=[
            pl.BlockSpec((tm, W_MIX), lambda i: (i, 0)),
            pl.BlockSpec((tm, W_MIX), lambda i: (i, 0)),
            pl.BlockSpec((tm, 2 * D_MODEL), lambda i: (i, 0)),
            pl.BlockSpec((tm, D_MODEL), lambda i: (i, 0)),
            _resident((W_MIX, D_MODEL)),
            _resident((W_MIX, D_MODEL)),
            _resident((D_MODEL, D_MODEL)),
        ],
        out_specs=pl.BlockSpec((tm, D_MODEL), lambda i: (i, 0)),
        out_shape=jax.ShapeDtypeStruct((t, D_MODEL), F32),
        compiler_params=pltpu.CompilerParams(
            dimension_semantics=("parallel",),
            vmem_limit_bytes=_vmem_limit(blocks, weights, 4 * tm * D_MODEL * 4)),
        name="merge",
    )(oa, ob, gates, x, w_a, w_b, w_o)


def _ffn_kernel(h_ref, g_ref, wu_ref, wd_ref, o_ref, n_ref):
    @pl.when(pl.program_id(1) == 0)
    def _():
        _norm_to_bf16(h_ref, g_ref, n_ref)
        o_ref[...] = h_ref[...]

    a = jnp.dot(n_ref[...], wu_ref[...], preferred_element_type=F32)
    r = jnp.square(jnp.maximum(a, 0.0)).astype(BF16)
    o_ref[...] += jnp.dot(r, wd_ref[...], preferred_element_type=F32)


def _ffn(h, g, w_up, w_down, *, tm, tf):
    t = h.shape[0]
    blocks = 2 * tm * D_MODEL * 4 + 2 * D_MODEL * tf * 2
    scratch = tm * D_MODEL * 2
    return pl.pallas_call(
        _ffn_kernel,
        grid=(t // tm, D_FF // tf),
        in_specs=[
            pl.BlockSpec((tm, D_MODEL), lambda i, f: (i, 0)),
            pl.BlockSpec((1, D_MODEL), lambda i, f: (0, 0)),
            pl.BlockSpec((D_MODEL, tf), lambda i, f: (0, f)),
            pl.BlockSpec((tf, D_MODEL), lambda i, f: (f, 0)),
        ],
        out_specs=pl.BlockSpec((tm, D_MODEL), lambda i, f: (i, 0)),
        out_shape=jax.ShapeDtypeStruct((t, D_MODEL), F32),
        scratch_shapes=[pltpu.VMEM((tm, D_MODEL), BF16)],
        compiler_params=pltpu.CompilerParams(
            dimension_semantics=("parallel", "arbitrary"),
            vmem_limit_bytes=_vmem_limit(blocks, scratch, 2 * tm * tf * 4)),
        name="ffn",
    )(h, g, w_up, w_down)


def _ple_kernel(h_ref, p_ref, gp_ref, gf_ref, wg_ref, wp_ref, y_ref):
    h = h_ref[...]
    n = _rmsnorm_rows(h, gp_ref[...]).astype(BF16)
    gate = jax.nn.sigmoid(jnp.dot(n, wg_ref[...], preferred_element_type=F32))
    proj = jnp.dot(p_ref[...].astype(BF16), wp_ref[...], preferred_element_type=F32)
    y_ref[...] = _rmsnorm_rows(h + proj * gate, gf_ref[...])


def _ple_final(h, p, g_ple, g_final, w_gate, w_proj, *, tm):
    t = h.shape[0]
    blocks = 2 * tm * D_MODEL * 4 + tm * D_PLE * 4
    weights = (D_MODEL * D_MODEL + D_PLE * D_MODEL) * 2
    return pl.pallas_call(
        _ple_kernel,
        grid=(t // tm,),
        in_specs=[
            pl.BlockSpec((tm, D_MODEL), lambda i: (i, 0)),
            pl.BlockSpec((tm, D_PLE), lambda i: (i, 0)),
            pl.BlockSpec((1, D_MODEL), lambda i: (0, 0)),
            pl.BlockSpec((1, D_MODEL), lambda i: (0, 0)),
            _resident((D_MODEL, D_MODEL)),
            _resident((D_PLE, D_MODEL)),
        ],
        out_specs=pl.BlockSpec((tm, D_MODEL), lambda i: (i, 0)),
        out_shape=jax.ShapeDtypeStruct((t, D_MODEL), F32),
        compiler_params=pltpu.CompilerParams(
            dimension_semantics=("parallel",),
            vmem_limit_bytes=_vmem_limit(blocks, weights, 4 * tm * D_MODEL * 4)),
        name="ple_final",
    )(h, p, g_ple, g_final, w_gate, w_proj)


def _rel_bias(table, q_pos, k_pos):
    idx = np.clip(q_pos[:, None] - k_pos[None, :], -REL_CLIP, REL_CLIP) + REL_CLIP
    return jnp.take(table.astype(F32), jnp.asarray(idx), axis=1)


def _band_prompt_bias(table):
    q_pos = np.arange(BAND_TQ)
    k_pos = np.arange(BAND_W) - A_REACH
    q_chunk = q_pos // CHUNK
    k_chunk = k_pos // CHUNK
    in_band = ((k_chunk[None, :] >= q_chunk[:, None] - N_PAST_CHUNKS)
               & (k_chunk[None, :] <= q_chunk[:, None]))
    return jnp.where(jnp.asarray(in_band)[None], _rel_bias(table, q_pos, k_pos), NEG)


def _pad_lanes(x, multiple):
    pad = (-x.shape[-1]) % multiple
    return jnp.pad(x, ((0, 0),) * (x.ndim - 1) + ((0, pad),)) if pad else x


def _token_tile(t, preferred):
    return preferred if t % preferred == 0 else t


def kernel(x_prompt, x_sample, p_prompt, p_sample, cache_a_k, cache_a_v, cache_b_k, cache_b_v,
           cache_b_logf, g_mix, w_in, b_f, rel_bias, w_a_proj, w_b_proj, w_o, g_mlp, w_up, w_down,
           g_ple, w_ple_gate, w_ple_proj, g_final):
    depth = w_in.shape[0]
    assert depth == 1, "single-layer step"
    batch, s_len, _ = x_prompt.shape
    dec_batch, t_new, _ = x_sample.shape
    n_cache_a = cache_a_k.shape[2]
    n_cache_b = cache_b_k.shape[2]
    assert s_len % FOX_T == 0 and s_len >= A_REACH and t_new == CHUNK and n_cache_a == A_REACH

    w = w_in[0]
    n_qkv = 6 * SEG_COLS
    w_main = jnp.concatenate([w[:, :n_qkv], w[:, n_qkv + N_HEADS:]], axis=1).astype(BF16)
    w_f = _pad_lanes(w[:, n_qkv:n_qkv + N_HEADS], V7X_LANES).astype(BF16)
    b_f_row = _pad_lanes(b_f[0][None, :].astype(F32), V7X_LANES)
    g_mix_row = g_mix[0][None, :].astype(F32)
    g_mlp_row = g_mlp[0][None, :].astype(F32)
    g_ple_row = g_ple[0][None, :].astype(F32)
    g_final_row = g_final[None, :].astype(F32)
    w_a = w_a_proj[0].astype(BF16)
    w_b = w_b_proj[0].astype(BF16)
    w_o_b = w_o[0].astype(BF16)
    w_up_b = w_up[0].astype(BF16)
    w_down_b = w_down[0].astype(BF16)
    w_pg = w_ple_gate[0].astype(BF16)
    w_pp = w_ple_proj[0].astype(BF16)

    def finish(x2d, p2d, oa, ob, gates):
        t = x2d.shape[0]
        h1 = _merge(oa, ob, gates, x2d, w_a, w_b, w_o_b, tm=_token_tile(t, 256))
        h2 = _ffn(h1, g_mlp_row, w_up_b, w_down_b, tm=_token_tile(t, 1024), tf=512)
        return _ple_final(h2, p2d, g_ple_row, g_final_row, w_pg, w_pp, tm=_token_tile(t, 512))

    tp = batch * s_len
    xp = x_prompt.reshape(tp, D_MODEL)
    tm_p = _token_tile(s_len, 1024)
    qkv_p, ka_p, va_p, kb_p, vb_p, gates_p, lf_p = _in_proj(
        xp, g_mix_row, w_main, w_f, b_f_row,
        a_rows=min(A_REACH, tm_p), a_period=s_len // tm_p, tm=tm_p, tn=512)
    lf_rows = lf_p.reshape(batch, s_len, N_HEADS).transpose(0, 2, 1).reshape(batch * N_HEADS, s_len)
    f_p = _cumsum_lanes(lf_rows).reshape(batch, N_HEADS, s_len // FOX_T, FOX_T)
    oa_p = _band_prompt(qkv_p, _band_prompt_bias(rel_bias[0]), batch, s_len)
    ob_p = _fox_prompt(qkv_p, f_p, batch, s_len)
    y_prompt = finish(xp, p_prompt[0].reshape(tp, D_PLE), oa_p, ob_p, gates_p)

    ts = dec_batch * t_new
    xs = x_sample.reshape(ts, D_MODEL)
    qkv_s, ka_s, va_s, kb_s, vb_s, gates_s, lf_s = _in_proj(
        xs, g_mix_row, w_main, w_f, b_f_row, a_rows=ts, a_period=1, tm=ts, tn=512)
    lf_new = lf_s.reshape(dec_batch, t_new, N_HEADS)
    lf_all = jnp.concatenate([cache_b_logf[0].astype(F32), lf_new], axis=1)
    lf_all = _pad_lanes(lf_all.transpose(0, 2, 1), V7X_LANES)
    f_s = _cumsum_lanes(lf_all.reshape(dec_batch * N_HEADS, -1)).reshape(dec_batch, N_HEADS, -1)
    k_pos = np.concatenate([np.arange(n_cache_a) - n_cache_a, np.arange(t_new)])
    bias_s = _rel_bias(rel_bias[0], np.arange(t_new), k_pos)
    ck_a = cache_a_k[0].reshape(dec_batch, n_cache_a, W_MIX)
    cv_a = cache_a_v[0].reshape(dec_batch, n_cache_a, W_MIX)
    oa_s = _band_sample(qkv_s, ck_a, cv_a, bias_s, dec_batch, t_new)
    ob_s = _fox_sample(qkv_s, cache_b_k[0].reshape(dec_batch, n_cache_b, W_MIX),
                       cache_b_v[0].reshape(dec_batch, n_cache_b, W_MIX), f_s, dec_batch, t_new)
    y_sample = finish(xs, p_sample[0].reshape(ts, D_PLE), oa_s, ob_s, gates_s)

    def heads5(a, b, n):
        return a.reshape(1, b, n, N_HEADS, HEAD_DIM)

    def roll_in(cache, new):
        return jnp.concatenate([cache[:, t_new:], new.reshape(dec_batch, t_new, W_MIX)], axis=1)

    n_state_a = min(A_REACH, s_len)
    return (y_prompt.reshape(batch, s_len, D_MODEL),
            y_sample.reshape(dec_batch, t_new, D_MODEL),
            heads5(ka_p, batch, n_state_a), heads5(va_p, batch, n_state_a),
            heads5(kb_p, batch, s_len), heads5(vb_p, batch, s_len),
            lf_p.reshape(1, batch, s_len, N_HEADS),
            heads5(roll_in(ck_a, ka_s), dec_batch, n_cache_a),
            heads5(roll_in(cv_a, va_s), dec_batch, n_cache_a),
            heads5(kb_s, dec_batch, t_new), heads5(vb_s, dec_batch, t_new),
            lf_new[None])
```

```python
import functools
import math

import numpy as np
import jax
import jax.numpy as jnp
from jax import lax
from jax.experimental import pallas as pl
from jax.experimental.pallas import tpu as pltpu

F32 = jnp.float32
BF16 = jnp.bfloat16

D_MODEL = 2048
CHUNK = 64
N_PAST_CHUNKS = 8
A_REACH = N_PAST_CHUNKS * CHUNK
HEAD_DIM = 128
N_HEADS = 8
W_MIX = N_HEADS * HEAD_DIM
REL_CLIP = 128
D_FF = 4 * D_MODEL
D_PLE = 256
RMS_EPS = 1e-6
SCALE = HEAD_DIM ** -0.5
NEG = -1e30
LOG2E = math.log2(math.e)
SCALE2 = SCALE * LOG2E

V7X_LANES = 128
V7X_VMEM_LIMIT_CAP = 56 * 1024 * 1024
MIB = 1024 * 1024

NORM_ROWS = 128
BAND_TQ = 128
BAND_W = A_REACH + BAND_TQ
BAND_G = 768
BAND_UNROLL = 4
FOX_T = 256
SEG_COLS = 1024

_NT = (((1,), (1,)), ((), ()))


def _vmem_limit(block_bytes, scratch_bytes, temp_bytes):
    est = 2 * block_bytes + scratch_bytes + temp_bytes
    return int(min(max(est, 16 * MIB), V7X_VMEM_LIMIT_CAP))


def _rmsnorm_rows(x, g):
    ms = jnp.mean(x * x, axis=-1, keepdims=True)
    return (x * lax.rsqrt(ms + RMS_EPS)) * g


def _norm_to_bf16(x_ref, g_ref, n_ref):
    rows = min(NORM_ROWS, x_ref.shape[0])

    def body(c, carry):
        r = pl.ds(pl.multiple_of(c * rows, rows), rows)
        n_ref[r, :] = _rmsnorm_rows(x_ref[r, :], g_ref[...]).astype(BF16)
        return carry

    lax.fori_loop(0, x_ref.shape[0] // rows, body, 0)


def _softmax_pv(parts):
    m = functools.reduce(jnp.maximum, [jnp.max(s, axis=-1, keepdims=True) for s, _ in parts])
    ps = [jnp.exp2(s - m) for s, _ in parts]
    l = functools.reduce(jnp.add, [jnp.sum(p, axis=-1, keepdims=True) for p in ps])
    o = functools.reduce(jnp.add, [jnp.dot(p.astype(BF16), v, preferred_element_type=F32)
                                   for p, (_, v) in zip(ps, parts)])
    return o / l


def _inproj_kernel(x_ref, g_ref, w_ref, wf_ref, bf_ref,
                   qkv_ref, kva_ref, kvb_ref, gate_ref, lf_ref, n_ref, *, a_rows, a_period):
    i = pl.program_id(0)
    j = pl.program_id(1)
    tm = x_ref.shape[0]

    @pl.when(j == 0)
    def _():
        _norm_to_bf16(x_ref, g_ref, n_ref)
        z = jnp.dot(n_ref[...], wf_ref[...], preferred_element_type=F32) + bf_ref[...]
        lf = jnp.minimum(z, 0.0) - jnp.log1p(jnp.exp(-jnp.abs(z)))
        lf_ref[...] = lf[:, :N_HEADS]

    def project():
        return jnp.dot(n_ref[...], w_ref[...], preferred_element_type=F32)

    is_tail = (i % a_period) == (a_period - 1)
    is_a_kv = jnp.logical_or(j == 1, j == 2)

    @pl.when(jnp.logical_or(j == 0, j == 3))
    def _():
        qkv_ref[...] = project().astype(BF16)

    @pl.when(jnp.logical_and(is_a_kv, jnp.logical_not(is_tail)))
    def _():
        qkv_ref[...] = project().astype(BF16)

    @pl.when(jnp.logical_and(is_a_kv, is_tail))
    def _():
        r = project()
        qkv_ref[...] = r.astype(BF16)
        kva_ref[...] = r[tm - a_rows:, :]

    @pl.when(jnp.logical_or(j == 4, j == 5))
    def _():
        r = project()
        qkv_ref[...] = r.astype(BF16)
        kvb_ref[...] = r

    @pl.when(j >= 6)
    def _():
        gate_ref[...] = jax.nn.sigmoid(project()).astype(BF16)


def _in_proj(x, g, w_main, w_f, b_f, *, a_rows, a_period, tm):
    t = x.shape[0]
    tn = SEG_COLS
    n_a = (t // tm) // a_period * a_rows

    def tail_map(i, j):
        return (i // a_period, jnp.where(i % a_period == a_period - 1, jnp.clip(j - 1, 0, 1), 0))

    blocks = (tm * D_MODEL * 4 + D_MODEL * tn * 2 + D_MODEL * V7X_LANES * 2
              + 2 * tm * tn * 2 + a_rows * tn * 4 + tm * tn * 4 + tm * V7X_LANES * 4)
    scratch = tm * D_MODEL * 2
    kern = functools.partial(_inproj_kernel, a_rows=a_rows, a_period=a_period)
    return pl.pallas_call(
        kern,
        grid=(t // tm, 10),
        in_specs=[
            pl.BlockSpec((tm, D_MODEL), lambda i, j: (i, 0)),
            pl.BlockSpec((1, D_MODEL), lambda i, j: (0, 0)),
            pl.BlockSpec((D_MODEL, tn), lambda i, j: (0, j)),
            pl.BlockSpec((D_MODEL, V7X_LANES), lambda i, j: (0, 0)),
            pl.BlockSpec((1, V7X_LANES), lambda i, j: (0, 0)),
        ],
        out_specs=[
            pl.BlockSpec((tm, tn), lambda i, j: (i, jnp.minimum(j, 5))),
            pl.BlockSpec((a_rows, tn), tail_map),
            pl.BlockSpec((tm, tn), lambda i, j: (i, jnp.clip(j - 4, 0, 1))),
            pl.BlockSpec((tm, tn), lambda i, j: (i, jnp.clip(j - 6, 0, 3))),
            pl.BlockSpec((tm, N_HEADS), lambda i, j: (i, 0)),
        ],
        out_shape=[
            jax.ShapeDtypeStruct((t, 6 * SEG_COLS), BF16),
            jax.ShapeDtypeStruct((n_a, 2 * SEG_COLS), F32),
            jax.ShapeDtypeStruct((t, 2 * SEG_COLS), F32),
            jax.ShapeDtypeStruct((t, 4 * SEG_COLS), BF16),
            jax.ShapeDtypeStruct((t, N_HEADS), F32),
        ],
        scratch_shapes=[pltpu.VMEM((tm, D_MODEL), BF16)],
        compiler_params=pltpu.CompilerParams(
            dimension_semantics=("parallel", "arbitrary"),
            vmem_limit_bytes=_vmem_limit(blocks, scratch, 6 * MIB)),
        name="in_proj",
    )(x, g, w_main, w_f, b_f)


def _cumsum_kernel(x_ref, o_ref):
    rows, length = x_ref.shape
    r_i = lax.broadcasted_iota(jnp.int32, (V7X_LANES, V7X_LANES), 0)
    c_i = lax.broadcasted_iota(jnp.int32, (V7X_LANES, V7X_LANES), 1)
    tri = (r_i <= c_i).astype(BF16)
    carry = jnp.zeros((rows, 1), F32)
    for c in range(length // V7X_LANES):
        x = x_ref[:, c * V7X_LANES:(c + 1) * V7X_LANES]
        hi = x.astype(BF16)
        r1 = x - hi.astype(F32)
        mid = r1.astype(BF16)
        lo = (r1 - mid.astype(F32)).astype(BF16)
        blk = (jnp.dot(hi, tri, preferred_element_type=F32)
               + jnp.dot(mid, tri, preferred_element_type=F32)
               + jnp.dot(lo, tri, preferred_element_type=F32)) + carry
        o_ref[:, c * V7X_LANES:(c + 1) * V7X_LANES] = blk
        carry = blk[:, V7X_LANES - 1:V7X_LANES]


def _cumsum_lanes(x):
    return pl.pallas_call(
        _cumsum_kernel,
        out_shape=jax.ShapeDtypeStruct(x.shape, F32),
        name="logf_cumsum",
    )(x)


def _rel_bias_kernel(g_ref, raw_ref, band_ref):
    row = lax.broadcasted_iota(jnp.int32, (BAND_TQ, BAND_W), 0)
    col = lax.broadcasted_iota(jnp.int32, (BAND_TQ, BAND_W), 1)
    q_chunk = row // CHUNK
    k_chunk = col // CHUNK - N_PAST_CHUNKS
    in_band = jnp.logical_and(k_chunk >= q_chunk - N_PAST_CHUNKS, k_chunk <= q_chunk)
    for h in range(N_HEADS):
        rows = jnp.broadcast_to(g_ref[h:h + 1, :], (BAND_TQ, BAND_G))
        toeplitz = pltpu.roll(rows, 0, 1, stride=1, stride_axis=0)[:, :BAND_W] * LOG2E
        raw_ref[h] = toeplitz
        band_ref[h] = jnp.where(in_band, toeplitz, NEG)


def _rel_bias_tiles(table):
    far = table[:, 2 * REL_CLIP:]
    near = jnp.flip(table[:, 1:2 * REL_CLIP], axis=1)
    n_far = A_REACH - REL_CLIP + 1
    g = jnp.concatenate([jnp.broadcast_to(far, (N_HEADS, n_far)), near,
                         jnp.broadcast_to(far, (N_HEADS, BAND_G - n_far - near.shape[1]))], axis=1)
    shape = jax.ShapeDtypeStruct((N_HEADS, BAND_TQ, BAND_W), F32)
    return pl.pallas_call(_rel_bias_kernel, out_shape=[shape, shape], name="rel_bias")(g.astype(F32))


def _band_prompt_kernel(q_ref, k_ref, v_ref, bias_ref, o_ref):
    s_len = q_ref.shape[0]
    n_tiles = s_len // BAND_TQ
    n_head = A_REACH // BAND_TQ

    def tile(q0, k0, n_keys, bias):
        q = q_ref[pl.ds(q0, BAND_TQ), :]
        s = lax.dot_general(q, k_ref[pl.ds(k0, n_keys), :], _NT, preferred_element_type=F32)
        o = _softmax_pv([(s * SCALE2 + bias, v_ref[pl.ds(k0, n_keys), :])])
        o_ref[pl.ds(q0, BAND_TQ), :] = o.astype(BF16)

    for t in range(n_head):
        n_keys = (t + 1) * BAND_TQ
        tile(t * BAND_TQ, 0, n_keys, bias_ref[0, :, BAND_W - n_keys:])

    def body(t, carry):
        q0 = pl.multiple_of(t * BAND_TQ, BAND_TQ)
        tile(q0, pl.multiple_of(q0 - A_REACH, BAND_TQ), BAND_W, bias_ref[0])
        return carry

    lax.fori_loop(n_head, n_tiles, body, 0, unroll=BAND_UNROLL)


def _band_prompt(qkv, bias, batch, s_len):
    t = batch * s_len
    blocks = 4 * s_len * HEAD_DIM * 2 + BAND_TQ * BAND_W * 4
    return pl.pallas_call(
        _band_prompt_kernel,
        grid=(batch, N_HEADS),
        in_specs=[
            pl.BlockSpec((s_len, HEAD_DIM), lambda b, h: (b, h)),
            pl.BlockSpec((s_len, HEAD_DIM), lambda b, h: (b, N_HEADS + h)),
            pl.BlockSpec((s_len, HEAD_DIM), lambda b, h: (b, 2 * N_HEADS + h)),
            pl.BlockSpec((1, BAND_TQ, BAND_W), lambda b, h: (h, 0, 0)),
        ],
        out_specs=pl.BlockSpec((s_len, HEAD_DIM), lambda b, h: (b, h)),
        out_shape=jax.ShapeDtypeStruct((t, W_MIX), BF16),
        compiler_params=pltpu.CompilerParams(
            dimension_semantics=("parallel", "parallel"),
            vmem_limit_bytes=_vmem_limit(blocks, 0, 8 * MIB)),
        name="band_prompt",
    )(qkv, qkv, qkv, bias)


def _band_sample_kernel(q_ref, kn_ref, vn_ref, ck_ref, cv_ref, bias_ref, o_ref):
    n_cache = ck_ref.shape[1]
    t_new = q_ref.shape[0]
    for h in range(N_HEADS):
        hs = slice(h * HEAD_DIM, (h + 1) * HEAD_DIM)
        q = q_ref[:, hs]
        kc = ck_ref[0, :, hs].astype(BF16)
        vc = cv_ref[0, :, hs].astype(BF16)
        s1 = (lax.dot_general(q, kc, _NT, preferred_element_type=F32) * SCALE2
              + bias_ref[h, :t_new, :n_cache])
        s2 = (lax.dot_general(q, kn_ref[:, hs], _NT, preferred_element_type=F32) * SCALE2
              + bias_ref[h, :t_new, n_cache:n_cache + t_new])
        o_ref[:, hs] = _softmax_pv([(s1, vc), (s2, vn_ref[:, hs])]).astype(BF16)


def _band_sample(qkv, cache_k, cache_v, bias, batch, t_new):
    n_cache = cache_k.shape[1]
    blocks = 4 * t_new * W_MIX * 2 + 2 * n_cache * W_MIX * 4 + bias.size * 4
    return pl.pallas_call(
        _band_sample_kernel,
        grid=(batch,),
        in_specs=[
            pl.BlockSpec((t_new, W_MIX), lambda b: (b, 0)),
            pl.BlockSpec((t_new, W_MIX), lambda b: (b, 1)),
            pl.BlockSpec((t_new, W_MIX), lambda b: (b, 2)),
            pl.BlockSpec((1, n_cache, W_MIX), lambda b: (b, 0, 0)),
            pl.BlockSpec((1, n_cache, W_MIX), lambda b: (b, 0, 0)),
            pl.BlockSpec(bias.shape, lambda b: (0, 0, 0)),
        ],
        out_specs=pl.BlockSpec((t_new, W_MIX), lambda b: (b, 0)),
        out_shape=jax.ShapeDtypeStruct((batch * t_new, W_MIX), BF16),
        compiler_params=pltpu.CompilerParams(
            dimension_semantics=("parallel",),
            vmem_limit_bytes=_vmem_limit(blocks, 0, 4 * MIB)),
        name="band_sample",
    )(qkv, qkv, qkv, cache_k, cache_v, bias)


def _fox_prompt_kernel(q_ref, k_ref, v_ref, f_ref, o_ref):
    s_len = q_ref.shape[0]
    row = lax.broadcasted_iota(jnp.int32, (FOX_T, FOX_T), 0)
    col = lax.broadcasted_iota(jnp.int32, (FOX_T, FOX_T), 1)
    causal = row >= col
    f2 = f_ref[0] * LOG2E
    for qi in range(s_len // FOX_T):
        q0 = qi * FOX_T
        q = q_ref[q0:q0 + FOX_T, :]
        s_diag = (lax.dot_general(q, k_ref[q0:q0 + FOX_T, :], _NT, preferred_element_type=F32) * SCALE2
                  - f2[:, q0:q0 + FOX_T])
        parts = [(jnp.where(causal, s_diag, NEG), v_ref[q0:q0 + FOX_T, :])]
        if qi:
            s_past = (lax.dot_general(q, k_ref[0:q0, :], _NT, preferred_element_type=F32) * SCALE2
                      - f2[:, 0:q0])
            parts.append((s_past, v_ref[0:q0, :]))
        o_ref[q0:q0 + FOX_T, :] = _softmax_pv(parts).astype(BF16)


def _fox_prompt(qkv, f_rows, batch, s_len):
    t = batch * s_len
    blocks = 4 * s_len * HEAD_DIM * 2 + 8 * s_len * 4
    return pl.pallas_call(
        _fox_prompt_kernel,
        grid=(batch, N_HEADS),
        in_specs=[
            pl.BlockSpec((s_len, HEAD_DIM), lambda b, h: (b, 3 * N_HEADS + h)),
            pl.BlockSpec((s_len, HEAD_DIM), lambda b, h: (b, 4 * N_HEADS + h)),
            pl.BlockSpec((s_len, HEAD_DIM), lambda b, h: (b, 5 * N_HEADS + h)),
            pl.BlockSpec((1, 1, s_len), lambda b, h: (b * N_HEADS + h, 0, 0)),
        ],
        out_specs=pl.BlockSpec((s_len, HEAD_DIM), lambda b, h: (b, h)),
        out_shape=jax.ShapeDtypeStruct((t, W_MIX), BF16),
        compiler_params=pltpu.CompilerParams(
            dimension_semantics=("parallel", "parallel"),
            vmem_limit_bytes=_vmem_limit(blocks, 0, 6 * FOX_T * s_len * 4)),
        name="fox_prompt",
    )(qkv, qkv, qkv, f_rows)


def _fox_sample_kernel(q_ref, kn_ref, vn_ref, ck_ref, cv_ref, f_ref, o_ref):
    n_cache = ck_ref.shape[1]
    t_new = q_ref.shape[0]
    row = lax.broadcasted_iota(jnp.int32, (t_new, t_new), 0)
    col = lax.broadcasted_iota(jnp.int32, (t_new, t_new), 1)
    causal = row >= col
    for h in range(N_HEADS):
        hs = slice(h * HEAD_DIM, (h + 1) * HEAD_DIM)
        q = q_ref[:, hs]
        f2 = f_ref[0, h:h + 1, :] * LOG2E
        s1 = (lax.dot_general(q, ck_ref[0, :, hs].astype(BF16), _NT, preferred_element_type=F32) * SCALE2
              - f2[:, :n_cache])
        s2 = (lax.dot_general(q, kn_ref[:, hs], _NT, preferred_element_type=F32) * SCALE2
              - f2[:, n_cache:n_cache + t_new])
        parts = [(s1, cv_ref[0, :, hs].astype(BF16)), (jnp.where(causal, s2, NEG), vn_ref[:, hs])]
        o_ref[:, hs] = _softmax_pv(parts).astype(BF16)


def _fox_sample(qkv, cache_k, cache_v, f_rows, batch, t_new):
    n_cache = cache_k.shape[1]
    f_len = f_rows.shape[-1]
    blocks = 4 * t_new * W_MIX * 2 + 2 * n_cache * W_MIX * 4 + N_HEADS * f_len * 4
    return pl.pallas_call(
        _fox_sample_kernel,
        grid=(batch,),
        in_specs=[
            pl.BlockSpec((t_new, W_MIX), lambda b: (b, 3)),
            pl.BlockSpec((t_new, W_MIX), lambda b: (b, 4)),
            pl.BlockSpec((t_new, W_MIX), lambda b: (b, 5)),
            pl.BlockSpec((1, n_cache, W_MIX), lambda b: (b, 0, 0)),
            pl.BlockSpec((1, n_cache, W_MIX), lambda b: (b, 0, 0)),
            pl.BlockSpec((1, N_HEADS, f_len), lambda b: (b, 0, 0)),
        ],
        out_specs=pl.BlockSpec((t_new, W_MIX), lambda b: (b, 0)),
        out_shape=jax.ShapeDtypeStruct((batch * t_new, W_MIX), BF16),
        compiler_params=pltpu.CompilerParams(
            dimension_semantics=("parallel",),
            vmem_limit_bytes=_vmem_limit(blocks, 0, 8 * MIB)),
        name="fox_sample",
    )(qkv, qkv, qkv, cache_k, cache_v, f_rows)


def _merge_kernel(oa_ref, ob_ref, gate_ref, x_ref, wa_ref, wb_ref, wo_ref, h_ref):
    ya = jnp.dot(oa_ref[...], wa_ref[...], preferred_element_type=F32)
    yb = jnp.dot(ob_ref[...], wb_ref[...], preferred_element_type=F32)
    m = (gate_ref[:, :D_MODEL].astype(F32) * ya + gate_ref[:, D_MODEL:].astype(F32) * yb)
    h_ref[...] = x_ref[...] + jnp.dot(m.astype(BF16), wo_ref[...], preferred_element_type=F32)


def _resident(shape):
    return pl.BlockSpec(shape, lambda *_: (0,) * len(shape), pipeline_mode=pl.Buffered(1))


def _merge(oa, ob, gates, x, w_a, w_b, w_o, *, tm):
    t = x.shape[0]
    blocks = 2 * tm * W_MIX * 2 + tm * 2 * D_MODEL * 2 + 2 * tm * D_MODEL * 4
    weights = (2 * W_MIX * D_MODEL + D_MODEL * D_MODEL) * 2
    return pl.pallas_call(
        _merge_kernel,
        grid=(t // tm,),
        in_specs=[
            pl.BlockSpec((tm, W_MIX), lambda i: (i, 0)),
            pl.BlockSpec((tm, W_MIX), lambda i: (i, 0)),
            pl.BlockSpec((tm, 2 * D_MODEL), lambda i: (i, 0)),
            pl.BlockSpec((tm, D_MODEL), lambda i: (i, 0)),
            _resident((W_MIX, D_MODEL)),
            _resident((W_MIX, D_MODEL)),
            _resident((D_MODEL, D_MODEL)),
        ],
        out_specs=pl.BlockSpec((tm, D_MODEL), lambda i: (i, 0)),
        out_shape=jax.ShapeDtypeStruct((t, D_MODEL), F32),
        compiler_params=pltpu.CompilerParams(
            dimension_semantics=("parallel",),
            vmem_limit_bytes=_vmem_limit(blocks, weights, 4 * tm * D_MODEL * 4)),
        name="merge",
    )(oa, ob, gates, x, w_a, w_b, w_o)


def _ffn_kernel(h_ref, g_ref, wu_ref, wd_ref, o_ref, n_ref):
    @pl.when(pl.program_id(1) == 0)
    def _():
        _norm_to_bf16(h_ref, g_ref, n_ref)
        o_ref[...] = h_ref[...]

    a = jnp.dot(n_ref[...], wu_ref[...], preferred_element_type=F32)
    r = jnp.square(jnp.maximum(a, 0.0)).astype(BF16)
    o_ref[...] += jnp.dot(r, wd_ref[...], preferred_element_type=F32)


def _ffn(h, g, w_up, w_down, *, tm, tf):
    t = h.shape[0]
    blocks = 2 * tm * D_MODEL * 4 + 2 * D_MODEL * tf * 2
    scratch = tm * D_MODEL * 2
    return pl.pallas_call(
        _ffn_kernel,
        grid=(t // tm, D_FF // tf),
        in_specs=[
            pl.BlockSpec((tm, D_MODEL), lambda i, f: (i, 0)),
            pl.BlockSpec((1, D_MODEL), lambda i, f: (0, 0)),
            pl.BlockSpec((D_MODEL, tf), lambda i, f: (0, f)),
            pl.BlockSpec((tf, D_MODEL), lambda i, f: (f, 0)),
        ],
        out_specs=pl.BlockSpec((tm, D_MODEL), lambda i, f: (i, 0)),
        out_shape=jax.ShapeDtypeStruct((t, D_MODEL), F32),
        scratch_shapes=[pltpu.VMEM((tm, D_MODEL), BF16)],
        compiler_params=pltpu.CompilerParams(
            dimension_semantics=("parallel", "arbitrary"),
            vmem_limit_bytes=_vmem_limit(blocks, scratch, 2 * tm * tf * 4)),
        name="ffn",
    )(h, g, w_up, w_down)


def _ple_kernel(h_ref, p_ref, gp_ref, gf_ref, wg_ref, wp_ref, y_ref):
    h = h_ref[...]
    n = _rmsnorm_rows(h, gp_ref[...]).astype(BF16)
    gate = jax.nn.sigmoid(jnp.dot(n, wg_ref[...], preferred_element_type=F32))
    proj = jnp.dot(p_ref[...].astype(BF16), wp_ref[...], preferred_element_type=F32)
    y_ref[...] = _rmsnorm_rows(h + proj * gate, gf_ref[...])


def _ple_final(h, p, g_ple, g_final, w_gate, w_proj, *, tm):
    t = h.shape[0]
    blocks = 2 * tm * D_MODEL * 4 + tm * D_PLE * 4
    weights = (D_MODEL * D_MODEL + D_PLE * D_MODEL) * 2
    return pl.pallas_call(
        _ple_kernel,
        grid=(t // tm,),
        in_specs=[
            pl.BlockSpec((tm, D_MODEL), lambda i: (i, 0)),
            pl.BlockSpec((tm, D_PLE), lambda i: (i, 0)),
            pl.BlockSpec((1, D_MODEL), lambda i: (0, 0)),
            pl.BlockSpec((1, D_MODEL), lambda i: (0, 0)),
            _resident((D_MODEL, D_MODEL)),
            _resident((D_PLE, D_MODEL)),
        ],
        out_specs=pl.BlockSpec((tm, D_MODEL), lambda i: (i, 0)),
        out_shape=jax.ShapeDtypeStruct((t, D_MODEL), F32),
        compiler_params=pltpu.CompilerParams(
            dimension_semantics=("parallel",),
            vmem_limit_bytes=_vmem_limit(blocks, weights, 4 * tm * D_MODEL * 4)),
        name="ple_final",
    )(h, p, g_ple, g_final, w_gate, w_proj)


def _pad_lanes(x, multiple):
    pad = (-x.shape[-1]) % multiple
    return jnp.pad(x, ((0, 0),) * (x.ndim - 1) + ((0, pad),)) if pad else x


def _token_tile(t, preferred):
    return preferred if t % preferred == 0 else t


def kernel(x_prompt, x_sample, p_prompt, p_sample, cache_a_k, cache_a_v, cache_b_k, cache_b_v,
           cache_b_logf, g_mix, w_in, b_f, rel_bias, w_a_proj, w_b_proj, w_o, g_mlp, w_up, w_down,
           g_ple, w_ple_gate, w_ple_proj, g_final):
    depth = w_in.shape[0]
    assert depth == 1, "single-layer step"
    batch, s_len, _ = x_prompt.shape
    dec_batch, t_new, _ = x_sample.shape
    n_cache_a = cache_a_k.shape[2]
    n_cache_b = cache_b_k.shape[2]
    assert s_len % FOX_T == 0 and s_len >= A_REACH and t_new == CHUNK and n_cache_a == A_REACH

    w = w_in[0]
    n_qkv = 6 * SEG_COLS
    w_main = jnp.concatenate([w[:, :n_qkv], w[:, n_qkv + N_HEADS:]], axis=1).astype(BF16)
    w_f = _pad_lanes(w[:, n_qkv:n_qkv + N_HEADS], V7X_LANES).astype(BF16)
    b_f_row = _pad_lanes(b_f[0][None, :].astype(F32), V7X_LANES)
    g_mix_row = g_mix[0][None, :].astype(F32)
    g_mlp_row = g_mlp[0][None, :].astype(F32)
    g_ple_row = g_ple[0][None, :].astype(F32)
    g_final_row = g_final[None, :].astype(F32)
    w_a = w_a_proj[0].astype(BF16)
    w_b = w_b_proj[0].astype(BF16)
    w_o_b = w_o[0].astype(BF16)
    w_up_b = w_up[0].astype(BF16)
    w_down_b = w_down[0].astype(BF16)
    w_pg = w_ple_gate[0].astype(BF16)
    w_pp = w_ple_proj[0].astype(BF16)
    bias_raw, bias_band = _rel_bias_tiles(rel_bias[0])

    def finish(x2d, p2d, oa, ob, gates):
        t = x2d.shape[0]
        h1 = _merge(oa, ob, gates, x2d, w_a, w_b, w_o_b, tm=_token_tile(t, 256))
        h2 = _ffn(h1, g_mlp_row, w_up_b, w_down_b, tm=_token_tile(t, 1024), tf=512)
        return _ple_final(h2, p2d, g_ple_row, g_final_row, w_pg, w_pp, tm=_token_tile(t, 512))

    tp = batch * s_len
    xp = x_prompt.reshape(tp, D_MODEL)
    tm_p = _token_tile(s_len, 1024)
    qkv_p, kva_p, kvb_p, gates_p, lf_p = _in_proj(
        xp, g_mix_row, w_main, w_f, b_f_row,
        a_rows=min(A_REACH, tm_p), a_period=s_len // tm_p, tm=tm_p)
    lf_rows = lf_p.reshape(batch, s_len, N_HEADS).transpose(0, 2, 1).reshape(batch * N_HEADS, s_len)
    f_p = _cumsum_lanes(lf_rows).reshape(batch * N_HEADS, 1, s_len)
    oa_p = _band_prompt(qkv_p, bias_band, batch, s_len)
    ob_p = _fox_prompt(qkv_p, f_p, batch, s_len)
    y_prompt = finish(xp, p_prompt[0].reshape(tp, D_PLE), oa_p, ob_p, gates_p)

    ts = dec_batch * t_new
    xs = x_sample.reshape(ts, D_MODEL)
    qkv_s, kva_s, kvb_s, gates_s, lf_s = _in_proj(
        xs, g_mix_row, w_main, w_f, b_f_row, a_rows=ts, a_period=1, tm=ts)
    lf_new = lf_s.reshape(dec_batch, t_new, N_HEADS)
    lf_all = jnp.concatenate([cache_b_logf[0].astype(F32), lf_new], axis=1)
    lf_all = _pad_lanes(lf_all.transpose(0, 2, 1), V7X_LANES)
    f_s = _cumsum_lanes(lf_all.reshape(dec_batch * N_HEADS, -1)).reshape(dec_batch, N_HEADS, -1)
    ck_a = cache_a_k[0].reshape(dec_batch, n_cache_a, W_MIX)
    cv_a = cache_a_v[0].reshape(dec_batch, n_cache_a, W_MIX)
    oa_s = _band_sample(qkv_s, ck_a, cv_a, bias_raw, dec_batch, t_new)
    ob_s = _fox_sample(qkv_s, cache_b_k[0].reshape(dec_batch, n_cache_b, W_MIX),
                       cache_b_v[0].reshape(dec_batch, n_cache_b, W_MIX), f_s, dec_batch, t_new)
    y_sample = finish(xs, p_sample[0].reshape(ts, D_PLE), oa_s, ob_s, gates_s)

    def heads5(a, b, n):
        return a.reshape(1, b, n, N_HEADS, HEAD_DIM)

    def roll_in(cache, new):
        return jnp.concatenate([cache[:, t_new:], new.reshape(dec_batch, t_new, W_MIX)], axis=1)

    n_state_a = min(A_REACH, s_len)
    return (y_prompt.reshape(batch, s_len, D_MODEL),
            y_sample.reshape(dec_batch, t_new, D_MODEL),
            heads5(kva_p[:, :W_MIX], batch, n_state_a), heads5(kva_p[:, W_MIX:], batch, n_state_a),
            heads5(kvb_p[:, :W_MIX], batch, s_len), heads5(kvb_p[:, W_MIX:], batch, s_len),
            lf_p.reshape(1, batch, s_len, N_HEADS),
            heads5(roll_in(ck_a, kva_s[:, :W_MIX]), dec_batch, n_cache_a),
            heads5(roll_in(cv_a, kva_s[:, W_MIX:]), dec_batch, n_cache_a),
            heads5(kvb_s[:, :W_MIX], dec_batch, t_new), heads5(kvb_s[:, W_MIX:], dec_batch, t_new),
            lf_new[None])
```

```python
import functools
import math

import jax
import jax.numpy as jnp
from jax import lax
from jax.experimental import pallas as pl
from jax.experimental.pallas import tpu as pltpu

F32 = jnp.float32
BF16 = jnp.bfloat16

D_MODEL = 2048
CHUNK = 64
N_PAST_CHUNKS = 8
A_REACH = N_PAST_CHUNKS * CHUNK
HEAD_DIM = 128
N_HEADS = 8
W_MIX = N_HEADS * HEAD_DIM
REL_CLIP = 128
D_FF = 4 * D_MODEL
D_PLE = 256
RMS_EPS = 1e-6
SCALE = HEAD_DIM ** -0.5
NEG = -1e30
LOG2E = math.log2(math.e)
SCALE2 = SCALE * LOG2E

V7X_LANES = 128
V7X_VMEM_LIMIT_CAP = 56 * 1024 * 1024
MIB = 1024 * 1024

NORM_ROWS = 128
BAND_TQ = 128
BAND_W = A_REACH + BAND_TQ
BAND_G = 768
FOX_T = 256
SEG_COLS = 1024

_NT = (((1,), (1,)), ((), ()))


def _vmem_limit(block_bytes, scratch_bytes, temp_bytes):
    est = 2 * block_bytes + scratch_bytes + temp_bytes
    return int(min(max(est, 16 * MIB), V7X_VMEM_LIMIT_CAP))


def _rmsnorm_rows(x, g):
    ms = jnp.mean(x * x, axis=-1, keepdims=True)
    return (x * lax.rsqrt(ms + RMS_EPS)) * g


def _norm_to_bf16(x_ref, g_ref, n_ref):
    rows = min(NORM_ROWS, x_ref.shape[0])

    def body(c, carry):
        r = pl.ds(pl.multiple_of(c * rows, rows), rows)
        n_ref[r, :] = _rmsnorm_rows(x_ref[r, :], g_ref[...]).astype(BF16)
        return carry

    lax.fori_loop(0, x_ref.shape[0] // rows, body, 0)


def _softmax_pv(parts):
    m = functools.reduce(jnp.maximum, [jnp.max(s, axis=-1, keepdims=True) for s, _ in parts])
    ps = [jnp.exp2(s - m) for s, _ in parts]
    l = functools.reduce(jnp.add, [jnp.sum(p, axis=-1, keepdims=True) for p in ps])
    o = functools.reduce(jnp.add, [jnp.dot(p.astype(BF16), v, preferred_element_type=F32)
                                   for p, (_, v) in zip(ps, parts)])
    return o / l


def _store_heads(ref, rows):
    for h in range(N_HEADS):
        ref[:, h, :] = rows[:, h * HEAD_DIM:(h + 1) * HEAD_DIM]


def _inproj_q_kernel(x_ref, g_ref, w_ref, wf_ref, bf_ref, q_ref, gate_ref, lf_ref, n_ref):
    j = pl.program_id(1)

    @pl.when(j == 0)
    def _():
        _norm_to_bf16(x_ref, g_ref, n_ref)
        z = jnp.dot(n_ref[...], wf_ref[...], preferred_element_type=F32) + bf_ref[...]
        lf = jnp.minimum(z, 0.0) - jnp.log1p(jnp.exp(-jnp.abs(z)))
        lf_ref[...] = lf[:, :N_HEADS]

    @pl.when(j < 2)
    def _():
        q_ref[...] = jnp.dot(n_ref[...], w_ref[...], preferred_element_type=F32).astype(BF16)

    @pl.when(j >= 2)
    def _():
        a = jnp.dot(n_ref[...], w_ref[...], preferred_element_type=F32)
        gate_ref[...] = jax.nn.sigmoid(a).astype(BF16)


def _in_proj_q(x, g, w_qg, w_f, b_f, *, tm):
    t = x.shape[0]
    tn = SEG_COLS
    blocks = (tm * D_MODEL * 4 + D_MODEL * tn * 2 + D_MODEL * V7X_LANES * 2
              + 2 * tm * tn * 2 + tm * V7X_LANES * 4 + tm * D_MODEL * 2)
    return pl.pallas_call(
        _inproj_q_kernel,
        grid=(t // tm, 6),
        in_specs=[
            pl.BlockSpec((tm, D_MODEL), lambda i, j: (i, 0)),
            pl.BlockSpec((1, D_MODEL), lambda i, j: (0, 0)),
            pl.BlockSpec((D_MODEL, tn), lambda i, j: (0, j)),
            pl.BlockSpec((D_MODEL, V7X_LANES), lambda i, j: (0, 0)),
            pl.BlockSpec((1, V7X_LANES), lambda i, j: (0, 0)),
        ],
        out_specs=[
            pl.BlockSpec((tm, tn), lambda i, j: (i, jnp.minimum(j, 1))),
            pl.BlockSpec((tm, tn), lambda i, j: (i, jnp.clip(j - 2, 0, 3))),
            pl.BlockSpec((tm, N_HEADS), lambda i, j: (i, 0)),
            pl.BlockSpec((tm, D_MODEL), lambda i, j: (i, 0)),
        ],
        out_shape=[
            jax.ShapeDtypeStruct((t, 2 * SEG_COLS), BF16),
            jax.ShapeDtypeStruct((t, 4 * SEG_COLS), BF16),
            jax.ShapeDtypeStruct((t, N_HEADS), F32),
            jax.ShapeDtypeStruct((t, D_MODEL), BF16),
        ],
        compiler_params=pltpu.CompilerParams(
            dimension_semantics=("parallel", "arbitrary"),
            vmem_limit_bytes=_vmem_limit(blocks, 0, 6 * MIB)),
        name="in_proj_q",
    )(x, g, w_qg, w_f, b_f)


def _inproj_kv_kernel(n_ref, w_ref, kv_ref, ka_ref, va_ref, kb_ref, vb_ref, *, a_rows, a_period):
    i = pl.program_id(0)
    j = pl.program_id(1)
    tm = n_ref.shape[0]
    is_tail = (i % a_period) == (a_period - 1)

    def project():
        return jnp.dot(n_ref[...], w_ref[...], preferred_element_type=F32)

    @pl.when(jnp.logical_and(j < 2, jnp.logical_not(is_tail)))
    def _():
        kv_ref[...] = project().astype(BF16)

    for col, tail_ref in ((0, ka_ref), (1, va_ref)):
        @pl.when(jnp.logical_and(j == col, is_tail))
        def _(tail_ref=tail_ref):
            r = project()
            kv_ref[...] = r.astype(BF16)
            _store_heads(tail_ref, r[tm - a_rows:, :])

    for col, full_ref in ((2, kb_ref), (3, vb_ref)):
        @pl.when(j == col)
        def _(full_ref=full_ref):
            r = project()
            kv_ref[...] = r.astype(BF16)
            _store_heads(full_ref, r)


def _in_proj_kv(n, w_kv, *, a_rows, a_period, tm):
    t = n.shape[0]
    tn = SEG_COLS
    n_a = (t // tm) // a_period * a_rows
    blocks = (tm * D_MODEL * 2 + D_MODEL * tn * 2 + tm * tn * 2
              + 2 * a_rows * tn * 4 + 2 * tm * tn * 4)
    kern = functools.partial(_inproj_kv_kernel, a_rows=a_rows, a_period=a_period)
    head_block = lambda rows: (rows, N_HEADS, HEAD_DIM)
    return pl.pallas_call(
        kern,
        grid=(t // tm, 4),
        in_specs=[
            pl.BlockSpec((tm, D_MODEL), lambda i, j: (i, 0)),
            pl.BlockSpec((D_MODEL, tn), lambda i, j: (0, j)),
        ],
        out_specs=[
            pl.BlockSpec((tm, tn), lambda i, j: (i, j)),
            pl.BlockSpec(head_block(a_rows), lambda i, j: (i // a_period, 0, 0)),
            pl.BlockSpec(head_block(a_rows), lambda i, j: (i // a_period, 0, 0)),
            pl.BlockSpec(head_block(tm), lambda i, j: (i, 0, 0)),
            pl.BlockSpec(head_block(tm), lambda i, j: (i, 0, 0)),
        ],
        out_shape=[
            jax.ShapeDtypeStruct((t, 4 * SEG_COLS), BF16),
            jax.ShapeDtypeStruct(head_block(n_a), F32),
            jax.ShapeDtypeStruct(head_block(n_a), F32),
            jax.ShapeDtypeStruct(head_block(t), F32),
            jax.ShapeDtypeStruct(head_block(t), F32),
        ],
        compiler_params=pltpu.CompilerParams(
            dimension_semantics=("parallel", "arbitrary"),
            vmem_limit_bytes=_vmem_limit(blocks, 0, 6 * MIB)),
        name="in_proj_kv",
    )(n, w_kv)


def _cumsum_kernel(x_ref, o_ref):
    rows, length = x_ref.shape
    r_i = lax.broadcasted_iota(jnp.int32, (V7X_LANES, V7X_LANES), 0)
    c_i = lax.broadcasted_iota(jnp.int32, (V7X_LANES, V7X_LANES), 1)
    tri = (r_i <= c_i).astype(BF16)
    carry = jnp.zeros((rows, 1), F32)
    for c in range(length // V7X_LANES):
        x = x_ref[:, c * V7X_LANES:(c + 1) * V7X_LANES]
        hi = x.astype(BF16)
        r1 = x - hi.astype(F32)
        mid = r1.astype(BF16)
        lo = (r1 - mid.astype(F32)).astype(BF16)
        blk = (jnp.dot(hi, tri, preferred_element_type=F32)
               + jnp.dot(mid, tri, preferred_element_type=F32)
               + jnp.dot(lo, tri, preferred_element_type=F32)) + carry
        o_ref[:, c * V7X_LANES:(c + 1) * V7X_LANES] = blk
        carry = blk[:, V7X_LANES - 1:V7X_LANES]


def _cumsum_lanes(x):
    return pl.pallas_call(
        _cumsum_kernel,
        out_shape=jax.ShapeDtypeStruct(x.shape, F32),
        name="logf_cumsum",
    )(x)


def _rel_bias_kernel(g_ref, raw_ref, band_ref):
    row = lax.broadcasted_iota(jnp.int32, (BAND_TQ, BAND_W), 0)
    col = lax.broadcasted_iota(jnp.int32, (BAND_TQ, BAND_W), 1)
    q_chunk = row // CHUNK
    k_chunk = col // CHUNK - N_PAST_CHUNKS
    in_band = jnp.logical_and(k_chunk >= q_chunk - N_PAST_CHUNKS, k_chunk <= q_chunk)
    for h in range(N_HEADS):
        rows = jnp.broadcast_to(g_ref[h:h + 1, :], (BAND_TQ, BAND_G))
        toeplitz = pltpu.roll(rows, 0, 1, stride=1, stride_axis=0)[:, :BAND_W] * LOG2E
        raw_ref[h] = toeplitz
        band_ref[h] = jnp.where(in_band, toeplitz, NEG)


def _rel_bias_tiles(table):
    far = table[:, 2 * REL_CLIP:]
    near = jnp.flip(table[:, 1:2 * REL_CLIP], axis=1)
    n_far = A_REACH - REL_CLIP + 1
    g = jnp.concatenate([jnp.broadcast_to(far, (N_HEADS, n_far)), near,
                         jnp.broadcast_to(far, (N_HEADS, BAND_G - n_far - near.shape[1]))], axis=1)
    shape = jax.ShapeDtypeStruct((N_HEADS, BAND_TQ, BAND_W), F32)
    return pl.pallas_call(_rel_bias_kernel, out_shape=[shape, shape], name="rel_bias")(g.astype(F32))


def _band_prompt_kernel(q_ref, k_ref, v_ref, bias_ref, o_ref):
    s_len = q_ref.shape[0]
    for t in range(s_len // BAND_TQ):
        q0 = t * BAND_TQ
        k0 = max(q0 - A_REACH, 0)
        n_keys = q0 + BAND_TQ - k0
        s = lax.dot_general(q_ref[q0:q0 + BAND_TQ, :], k_ref[k0:k0 + n_keys, :], _NT,
                            preferred_element_type=F32)
        s = s * SCALE2 + bias_ref[0, :, BAND_W - n_keys:]
        o = _softmax_pv([(s, v_ref[k0:k0 + n_keys, :])])
        o_ref[q0:q0 + BAND_TQ, :] = o.astype(BF16)


def _band_prompt(q, kv, bias, batch, s_len):
    t = batch * s_len
    blocks = 4 * s_len * HEAD_DIM * 2 + BAND_TQ * BAND_W * 4
    return pl.pallas_call(
        _band_prompt_kernel,
        grid=(batch, N_HEADS),
        in_specs=[
            pl.BlockSpec((s_len, HEAD_DIM), lambda b, h: (b, h)),
            pl.BlockSpec((s_len, HEAD_DIM), lambda b, h: (b, h)),
            pl.BlockSpec((s_len, HEAD_DIM), lambda b, h: (b, N_HEADS + h)),
            pl.BlockSpec((1, BAND_TQ, BAND_W), lambda b, h: (h, 0, 0)),
        ],
        out_specs=pl.BlockSpec((s_len, HEAD_DIM), lambda b, h: (b, h)),
        out_shape=jax.ShapeDtypeStruct((t, W_MIX), BF16),
        compiler_params=pltpu.CompilerParams(
            dimension_semantics=("parallel", "parallel"),
            vmem_limit_bytes=_vmem_limit(blocks, 0, 8 * MIB)),
        name="band_prompt",
    )(q, kv, kv, bias)


def _band_sample_kernel(q_ref, kn_ref, vn_ref, kn32_ref, vn32_ref, ck_ref, cv_ref, bias_ref,
                        o_ref, sk_ref, sv_ref):
    n_cache = ck_ref.shape[1]
    t_new = q_ref.shape[0]
    for h in range(N_HEADS):
        hs = slice(h * HEAD_DIM, (h + 1) * HEAD_DIM)
        q = q_ref[:, hs]
        kc = ck_ref[0, :, h, :].astype(BF16)
        vc = cv_ref[0, :, h, :].astype(BF16)
        s1 = (lax.dot_general(q, kc, _NT, preferred_element_type=F32) * SCALE2
              + bias_ref[h, :t_new, :n_cache])
        s2 = (lax.dot_general(q, kn_ref[:, hs], _NT, preferred_element_type=F32) * SCALE2
              + bias_ref[h, :t_new, n_cache:n_cache + t_new])
        o_ref[:, hs] = _softmax_pv([(s1, vc), (s2, vn_ref[:, hs])]).astype(BF16)
    for cache_ref, new_ref, state_ref in ((ck_ref, kn32_ref, sk_ref), (cv_ref, vn32_ref, sv_ref)):
        state_ref[0, :n_cache - t_new] = cache_ref[0, t_new:]
        state_ref[0, n_cache - t_new:] = new_ref[...]


def _band_sample(q, kv, ka32, va32, cache_k, cache_v, bias, batch, t_new):
    n_cache = cache_k.shape[1]
    cache_block = (1, n_cache, N_HEADS, HEAD_DIM)
    new_block = (t_new, N_HEADS, HEAD_DIM)
    blocks = 4 * t_new * W_MIX * 2 + 2 * t_new * W_MIX * 4 + 4 * n_cache * W_MIX * 4 + bias.size * 4
    return pl.pallas_call(
        _band_sample_kernel,
        grid=(batch,),
        in_specs=[
            pl.BlockSpec((t_new, W_MIX), lambda b: (b, 0)),
            pl.BlockSpec((t_new, W_MIX), lambda b: (b, 0)),
            pl.BlockSpec((t_new, W_MIX), lambda b: (b, 1)),
            pl.BlockSpec(new_block, lambda b: (b, 0, 0)),
            pl.BlockSpec(new_block, lambda b: (b, 0, 0)),
            pl.BlockSpec(cache_block, lambda b: (b, 0, 0, 0)),
            pl.BlockSpec(cache_block, lambda b: (b, 0, 0, 0)),
            pl.BlockSpec(bias.shape, lambda b: (0, 0, 0)),
        ],
        out_specs=[
            pl.BlockSpec((t_new, W_MIX), lambda b: (b, 0)),
            pl.BlockSpec(cache_block, lambda b: (b, 0, 0, 0)),
            pl.BlockSpec(cache_block, lambda b: (b, 0, 0, 0)),
        ],
        out_shape=[
            jax.ShapeDtypeStruct((batch * t_new, W_MIX), BF16),
            jax.ShapeDtypeStruct(cache_k.shape, F32),
            jax.ShapeDtypeStruct(cache_v.shape, F32),
        ],
        compiler_params=pltpu.CompilerParams(
            dimension_semantics=("parallel",),
            vmem_limit_bytes=_vmem_limit(blocks, 0, 4 * MIB)),
        name="band_sample",
    )(q, kv, kv, ka32, va32, cache_k, cache_v, bias)


def _fox_prompt_kernel(q_ref, k_ref, v_ref, f_ref, o_ref):
    s_len = q_ref.shape[0]
    row = lax.broadcasted_iota(jnp.int32, (FOX_T, FOX_T), 0)
    col = lax.broadcasted_iota(jnp.int32, (FOX_T, FOX_T), 1)
    causal = row >= col
    f2 = f_ref[0] * LOG2E
    for qi in range(s_len // FOX_T):
        q0 = qi * FOX_T
        q = q_ref[q0:q0 + FOX_T, :]
        s_diag = (lax.dot_general(q, k_ref[q0:q0 + FOX_T, :], _NT, preferred_element_type=F32) * SCALE2
                  - f2[:, q0:q0 + FOX_T])
        parts = [(jnp.where(causal, s_diag, NEG), v_ref[q0:q0 + FOX_T, :])]
        if qi:
            s_past = (lax.dot_general(q, k_ref[0:q0, :], _NT, preferred_element_type=F32) * SCALE2
                      - f2[:, 0:q0])
            parts.append((s_past, v_ref[0:q0, :]))
        o_ref[q0:q0 + FOX_T, :] = _softmax_pv(parts).astype(BF16)


def _fox_prompt(q, kv, f_rows, batch, s_len):
    t = batch * s_len
    blocks = 4 * s_len * HEAD_DIM * 2 + 8 * s_len * 4
    return pl.pallas_call(
        _fox_prompt_kernel,
        grid=(batch, N_HEADS),
        in_specs=[
            pl.BlockSpec((s_len, HEAD_DIM), lambda b, h: (b, N_HEADS + h)),
            pl.BlockSpec((s_len, HEAD_DIM), lambda b, h: (b, 2 * N_HEADS + h)),
            pl.BlockSpec((s_len, HEAD_DIM), lambda b, h: (b, 3 * N_HEADS + h)),
            pl.BlockSpec((1, 1, s_len), lambda b, h: (b * N_HEADS + h, 0, 0)),
        ],
        out_specs=pl.BlockSpec((s_len, HEAD_DIM), lambda b, h: (b, h)),
        out_shape=jax.ShapeDtypeStruct((t, W_MIX), BF16),
        compiler_params=pltpu.CompilerParams(
            dimension_semantics=("parallel", "parallel"),
            vmem_limit_bytes=_vmem_limit(blocks, 0, 6 * FOX_T * s_len * 4)),
        name="fox_prompt",
    )(q, kv, kv, f_rows)


def _fox_sample_kernel(q_ref, kn_ref, vn_ref, ck_ref, cv_ref, f_ref, o_ref):
    n_cache = ck_ref.shape[1]
    t_new = q_ref.shape[0]
    row = lax.broadcasted_iota(jnp.int32, (t_new, t_new), 0)
    col = lax.broadcasted_iota(jnp.int32, (t_new, t_new), 1)
    causal = row >= col
    for h in range(N_HEADS):
        hs = slice(h * HEAD_DIM, (h + 1) * HEAD_DIM)
        q = q_ref[:, hs]
        f2 = f_ref[0, h:h + 1, :] * LOG2E
        kc = ck_ref[0, :, h, :].astype(BF16)
        vc = cv_ref[0, :, h, :].astype(BF16)
        s1 = lax.dot_general(q, kc, _NT, preferred_element_type=F32) * SCALE2 - f2[:, :n_cache]
        s2 = (lax.dot_general(q, kn_ref[:, hs], _NT, preferred_element_type=F32) * SCALE2
              - f2[:, n_cache:n_cache + t_new])
        parts = [(s1, vc), (jnp.where(causal, s2, NEG), vn_ref[:, hs])]
        o_ref[:, hs] = _softmax_pv(parts).astype(BF16)


def _fox_sample(q, kv, cache_k, cache_v, f_rows, batch, t_new):
    n_cache = cache_k.shape[1]
    f_len = f_rows.shape[-1]
    cache_block = (1, n_cache, N_HEADS, HEAD_DIM)
    blocks = 4 * t_new * W_MIX * 2 + 2 * n_cache * W_MIX * 4 + N_HEADS * f_len * 4
    return pl.pallas_call(
        _fox_sample_kernel,
        grid=(batch,),
        in_specs=[
            pl.BlockSpec((t_new, W_MIX), lambda b: (b, 1)),
            pl.BlockSpec((t_new, W_MIX), lambda b: (b, 2)),
            pl.BlockSpec((t_new, W_MIX), lambda b: (b, 3)),
            pl.BlockSpec(cache_block, lambda b: (b, 0, 0, 0)),
            pl.BlockSpec(cache_block, lambda b: (b, 0, 0, 0)),
            pl.BlockSpec((1, N_HEADS, f_len), lambda b: (b, 0, 0)),
        ],
        out_specs=pl.BlockSpec((t_new, W_MIX), lambda b: (b, 0)),
        out_shape=jax.ShapeDtypeStruct((batch * t_new, W_MIX), BF16),
        compiler_params=pltpu.CompilerParams(
            dimension_semantics=("parallel",),
            vmem_limit_bytes=_vmem_limit(blocks, 0, 8 * MIB)),
        name="fox_sample",
    )(q, kv, kv, cache_k, cache_v, f_rows)


def _merge_kernel(oa_ref, ob_ref, gate_ref, x_ref, wa_ref, wb_ref, wo_ref, h_ref):
    ya = jnp.dot(oa_ref[...], wa_ref[...], preferred_element_type=F32)
    yb = jnp.dot(ob_ref[...], wb_ref[...], preferred_element_type=F32)
    m = (gate_ref[:, :D_MODEL].astype(F32) * ya + gate_ref[:, D_MODEL:].astype(F32) * yb)
    h_ref[...] = x_ref[...] + jnp.dot(m.astype(BF16), wo_ref[...], preferred_element_type=F32)


def _resident(shape):
    return pl.BlockSpec(shape, lambda *_: (0,) * len(shape), pipeline_mode=pl.Buffered(1))


def _merge(oa, ob, gates, x, w_a, w_b, w_o, *, tm):
    t = x.shape[0]
    blocks = 2 * tm * W_MIX * 2 + tm * 2 * D_MODEL * 2 + 2 * tm * D_MODEL * 4
    weights = (2 * W_MIX * D_MODEL + D_MODEL * D_MODEL) * 2
    return pl.pallas_call(
        _merge_kernel,
        grid=(t // tm,),
        in_specs=[
            pl.BlockSpec((tm, W_MIX), lambda i: (i, 0)),
            pl.BlockSpec((tm, W_MIX), lambda i: (i, 0)),
            pl.BlockSpec((tm, 2 * D_MODEL), lambda i: (i, 0)),
            pl.BlockSpec((tm, D_MODEL), lambda i: (i, 0)),
            _resident((W_MIX, D_MODEL)),
            _resident((W_MIX, D_MODEL)),
            _resident((D_MODEL, D_MODEL)),
        ],
        out_specs=pl.BlockSpec((tm, D_MODEL), lambda i: (i, 0)),
        out_shape=jax.ShapeDtypeStruct((t, D_MODEL), F32),
        compiler_params=pltpu.CompilerParams(
            dimension_semantics=("parallel",),
            vmem_limit_bytes=_vmem_limit(blocks, weights, 4 * tm * D_MODEL * 4)),
        name="merge",
    )(oa, ob, gates, x, w_a, w_b, w_o)


def _ffn_kernel(h_ref, g_ref, wu_ref, wd_ref, o_ref, n_ref):
    @pl.when(pl.program_id(1) == 0)
    def _():
        _norm_to_bf16(h_ref, g_ref, n_ref)
        o_ref[...] = h_ref[...]

    a = jnp.dot(n_ref[...], wu_ref[...], preferred_element_type=F32)
    r = jnp.square(jnp.maximum(a, 0.0)).astype(BF16)
    o_ref[...] += jnp.dot(r, wd_ref[...], preferred_element_type=F32)


def _ffn(h, g, w_up, w_down, *, tm, tf):
    t = h.shape[0]
    blocks = 2 * tm * D_MODEL * 4 + 2 * D_MODEL * tf * 2
    scratch = tm * D_MODEL * 2
    return pl.pallas_call(
        _ffn_kernel,
        grid=(t // tm, D_FF // tf),
        in_specs=[
            pl.BlockSpec((tm, D_MODEL), lambda i, f: (i, 0)),
            pl.BlockSpec((1, D_MODEL), lambda i, f: (0, 0)),
            pl.BlockSpec((D_MODEL, tf), lambda i, f: (0, f)),
            pl.BlockSpec((tf, D_MODEL), lambda i, f: (f, 0)),
        ],
        out_specs=pl.BlockSpec((tm, D_MODEL), lambda i, f: (i, 0)),
        out_shape=jax.ShapeDtypeStruct((t, D_MODEL), F32),
        scratch_shapes=[pltpu.VMEM((tm, D_MODEL), BF16)],
        compiler_params=pltpu.CompilerParams(
            dimension_semantics=("parallel", "arbitrary"),
            vmem_limit_bytes=_vmem_limit(blocks, scratch, 2 * tm * tf * 4)),
        name="ffn",
    )(h, g, w_up, w_down)


def _ple_kernel(h_ref, p_ref, gp_ref, gf_ref, wg_ref, wp_ref, y_ref):
    h = h_ref[...]
    n = _rmsnorm_rows(h, gp_ref[...]).astype(BF16)
    gate = jax.nn.sigmoid(jnp.dot(n, wg_ref[...], preferred_element_type=F32))
    proj = jnp.dot(p_ref[...].astype(BF16), wp_ref[...], preferred_element_type=F32)
    y_ref[...] = _rmsnorm_rows(h + proj * gate, gf_ref[...])


def _ple_final(h, p, g_ple, g_final, w_gate, w_proj, *, tm):
    t = h.shape[0]
    blocks = 2 * tm * D_MODEL * 4 + tm * D_PLE * 4
    weights = (D_MODEL * D_MODEL + D_PLE * D_MODEL) * 2
    return pl.pallas_call(
        _ple_kernel,
        grid=(t // tm,),
        in_specs=[
            pl.BlockSpec((tm, D_MODEL), lambda i: (i, 0)),
            pl.BlockSpec((tm, D_PLE), lambda i: (i, 0)),
            pl.BlockSpec((1, D_MODEL), lambda i: (0, 0)),
            pl.BlockSpec((1, D_MODEL), lambda i: (0, 0)),
            _resident((D_MODEL, D_MODEL)),
            _resident((D_PLE, D_MODEL)),
        ],
        out_specs=pl.BlockSpec((tm, D_MODEL), lambda i: (i, 0)),
        out_shape=jax.ShapeDtypeStruct((t, D_MODEL), F32),
        compiler_params=pltpu.CompilerParams(
            dimension_semantics=("parallel",),
            vmem_limit_bytes=_vmem_limit(blocks, weights, 4 * tm * D_MODEL * 4)),
        name="ple_final",
    )(h, p, g_ple, g_final, w_gate, w_proj)


def _pad_lanes(x, multiple):
    pad = (-x.shape[-1]) % multiple
    return jnp.pad(x, ((0, 0),) * (x.ndim - 1) + ((0, pad),)) if pad else x


def _token_tile(t, preferred):
    return preferred if t % preferred == 0 else t


def kernel(x_prompt, x_sample, p_prompt, p_sample, cache_a_k, cache_a_v, cache_b_k, cache_b_v,
           cache_b_logf, g_mix, w_in, b_f, rel_bias, w_a_proj, w_b_proj, w_o, g_mlp, w_up, w_down,
           g_ple, w_ple_gate, w_ple_proj, g_final):
    depth = w_in.shape[0]
    assert depth == 1, "single-layer step"
    batch, s_len, _ = x_prompt.shape
    dec_batch, t_new, _ = x_sample.shape
    n_cache_a = cache_a_k.shape[2]
    assert s_len % FOX_T == 0 and s_len >= A_REACH and t_new == CHUNK and n_cache_a == A_REACH

    w = w_in[0]
    seg = lambda k: w[:, k * SEG_COLS:(k + 1) * SEG_COLS]
    n_qkv = 6 * SEG_COLS
    w_qg = jnp.concatenate([seg(0), seg(3), w[:, n_qkv + N_HEADS:]], axis=1).astype(BF16)
    w_kv = jnp.concatenate([seg(1), seg(2), seg(4), seg(5)], axis=1).astype(BF16)
    w_f = _pad_lanes(w[:, n_qkv:n_qkv + N_HEADS], V7X_LANES).astype(BF16)
    b_f_row = _pad_lanes(b_f[0][None, :].astype(F32), V7X_LANES)
    g_mix_row = g_mix[0][None, :].astype(F32)
    g_mlp_row = g_mlp[0][None, :].astype(F32)
    g_ple_row = g_ple[0][None, :].astype(F32)
    g_final_row = g_final[None, :].astype(F32)
    w_a = w_a_proj[0].astype(BF16)
    w_b = w_b_proj[0].astype(BF16)
    w_o_b = w_o[0].astype(BF16)
    w_up_b = w_up[0].astype(BF16)
    w_down_b = w_down[0].astype(BF16)
    w_pg = w_ple_gate[0].astype(BF16)
    w_pp = w_ple_proj[0].astype(BF16)
    bias_raw, bias_band = _rel_bias_tiles(rel_bias[0])

    def project(x2d, *, a_rows, a_period, tm):
        q, gates, lf, n = _in_proj_q(x2d, g_mix_row, w_qg, w_f, b_f_row, tm=tm)
        return (q, gates, lf) + tuple(_in_proj_kv(n, w_kv, a_rows=a_rows, a_period=a_period, tm=tm))

    def finish(x2d, p2d, oa, ob, gates):
        t = x2d.shape[0]
        h1 = _merge(oa, ob, gates, x2d, w_a, w_b, w_o_b, tm=_token_tile(t, 256))
        h2 = _ffn(h1, g_mlp_row, w_up_b, w_down_b, tm=_token_tile(t, 1024), tf=512)
        return _ple_final(h2, p2d, g_ple_row, g_final_row, w_pg, w_pp, tm=_token_tile(t, 512))

    tp = batch * s_len
    xp = x_prompt.reshape(tp, D_MODEL)
    tm_p = _token_tile(s_len, 1024)
    q_p, gates_p, lf_p, kv_p, ka_p, va_p, kb_p, vb_p = project(
        xp, a_rows=min(A_REACH, tm_p), a_period=s_len // tm_p, tm=tm_p)
    lf_rows = lf_p.reshape(batch, s_len, N_HEADS).transpose(0, 2, 1).reshape(batch * N_HEADS, s_len)
    f_p = _cumsum_lanes(lf_rows).reshape(batch * N_HEADS, 1, s_len)
    oa_p = _band_prompt(q_p, kv_p, bias_band, batch, s_len)
    ob_p = _fox_prompt(q_p, kv_p, f_p, batch, s_len)
    y_prompt = finish(xp, p_prompt[0].reshape(tp, D_PLE), oa_p, ob_p, gates_p)

    ts = dec_batch * t_new
    xs = x_sample.reshape(ts, D_MODEL)
    q_s, gates_s, lf_s, kv_s, ka_s, va_s, kb_s, vb_s = project(xs, a_rows=ts, a_period=1, tm=ts)
    lf_new = lf_s.reshape(dec_batch, t_new, N_HEADS)
    lf_all = jnp.concatenate([cache_b_logf[0].astype(F32), lf_new], axis=1)
    lf_all = _pad_lanes(lf_all.transpose(0, 2, 1), V7X_LANES)
    f_s = _cumsum_lanes(lf_all.reshape(dec_batch * N_HEADS, -1)).reshape(dec_batch, N_HEADS, -1)
    oa_s, sk_s, sv_s = _band_sample(q_s, kv_s, ka_s, va_s, cache_a_k[0], cache_a_v[0], bias_raw,
                                    dec_batch, t_new)
    ob_s = _fox_sample(q_s, kv_s, cache_b_k[0], cache_b_v[0], f_s, dec_batch, t_new)
    y_sample = finish(xs, p_sample[0].reshape(ts, D_PLE), oa_s, ob_s, gates_s)

    def state(a, b):
        return a.reshape(1, b, -1, N_HEADS, HEAD_DIM)

    return (y_prompt.reshape(batch, s_len, D_MODEL),
            y_sample.reshape(dec_batch, t_new, D_MODEL),
            state(ka_p, batch), state(va_p, batch), state(kb_p, batch), state(vb_p, batch),
            lf_p.reshape(1, batch, s_len, N_HEADS),
            sk_s[None], sv_s[None],
            state(kb_s, dec_batch), state(vb_s, dec_batch),
            lf_new[None])
```

```python
import functools
import math

import jax
import jax.numpy as jnp
from jax import lax
from jax.experimental import pallas as pl
from jax.experimental.pallas import tpu as pltpu

F32 = jnp.float32
BF16 = jnp.bfloat16

D_MODEL = 2048
CHUNK = 64
N_PAST_CHUNKS = 8
A_REACH = N_PAST_CHUNKS * CHUNK
HEAD_DIM = 128
N_HEADS = 8
W_MIX = N_HEADS * HEAD_DIM
REL_CLIP = 128
D_FF = 4 * D_MODEL
D_PLE = 256
RMS_EPS = 1e-6
SCALE = HEAD_DIM ** -0.5
NEG = -1e30
LOG2E = math.log2(math.e)
SCALE2 = SCALE * LOG2E

V7X_LANES = 128
V7X_VMEM_LIMIT_CAP = 56 * 1024 * 1024
MIB = 1024 * 1024

NORM_ROWS = 128
BAND_TQ = 256
BAND_W = A_REACH + BAND_TQ
BAND_G = BAND_W + BAND_TQ
PV_COLS = 2 * HEAD_DIM
FOX_T = 256
SEG_COLS = 1024

_NT = (((1,), (1,)), ((), ()))


def _vmem_limit(block_bytes, scratch_bytes, temp_bytes):
    est = 2 * block_bytes + scratch_bytes + temp_bytes
    return int(min(max(est, 16 * MIB), V7X_VMEM_LIMIT_CAP))


def _rmsnorm_rows(x, g):
    ms = jnp.mean(x * x, axis=-1, keepdims=True)
    return (x * lax.rsqrt(ms + RMS_EPS)) * g


def _norm_to_bf16(x_ref, g_ref, n_ref):
    rows = min(NORM_ROWS, x_ref.shape[0])

    def body(c, carry):
        r = pl.ds(pl.multiple_of(c * rows, rows), rows)
        n_ref[r, :] = _rmsnorm_rows(x_ref[r, :], g_ref[...]).astype(BF16)
        return carry

    lax.fori_loop(0, x_ref.shape[0] // rows, body, 0)


def _softmax_pv(parts):
    m = functools.reduce(jnp.maximum, [jnp.max(s, axis=-1, keepdims=True) for s, _ in parts])
    ps = [jnp.exp2(s - m) for s, _ in parts]
    l = functools.reduce(jnp.add, [jnp.sum(p, axis=-1, keepdims=True) for p in ps])
    o = functools.reduce(jnp.add, [jnp.dot(p.astype(BF16), v, preferred_element_type=F32)
                                   for p, (_, v) in zip(ps, parts)])
    return o / l


def _widen_values(v_ref, vp_ref):
    lane = lax.broadcasted_iota(jnp.int32, v_ref.shape, 1)
    vp_ref[:, :HEAD_DIM] = v_ref[...]
    vp_ref[:, HEAD_DIM:] = jnp.where(lane == 0, 1.0, 0.0).astype(BF16)


def _softmax_weights(parts):
    m = functools.reduce(jnp.maximum, [jnp.max(s, axis=-1, keepdims=True) for s, _ in parts])
    return [(jnp.exp2(s - m).astype(BF16), v) for s, v in parts]


def _weighted_values_wide(weighted):
    o = functools.reduce(jnp.add, [jnp.dot(p, v, preferred_element_type=F32) for p, v in weighted])
    return o[:, :HEAD_DIM] / o[:, HEAD_DIM:HEAD_DIM + 1]


def _attention_tiles(n_tiles, logits, store):
    stage1 = {0: logits(0)}
    if n_tiles > 1:
        stage1[1] = logits(1)
    stage2 = {0: _softmax_weights(stage1.pop(0))}
    for t in range(n_tiles):
        if t + 2 < n_tiles:
            stage1[t + 2] = logits(t + 2)
        if t + 1 < n_tiles:
            stage2[t + 1] = _softmax_weights(stage1.pop(t + 1))
        store(t, _weighted_values_wide(stage2.pop(t)))


def _store_heads(ref, rows):
    for h in range(N_HEADS):
        ref[:, h, :] = rows[:, h * HEAD_DIM:(h + 1) * HEAD_DIM]


def _inproj_q_kernel(x_ref, g_ref, w_ref, wf_ref, bf_ref, q_ref, gate_ref, lf_ref, n_ref):
    j = pl.program_id(1)

    @pl.when(j == 0)
    def _():
        _norm_to_bf16(x_ref, g_ref, n_ref)
        z = jnp.dot(n_ref[...], wf_ref[...], preferred_element_type=F32) + bf_ref[...]
        lf = jnp.minimum(z, 0.0) - jnp.log1p(jnp.exp(-jnp.abs(z)))
        lf_ref[...] = lf[:, :N_HEADS]

    @pl.when(j < 2)
    def _():
        q_ref[...] = jnp.dot(n_ref[...], w_ref[...], preferred_element_type=F32).astype(BF16)

    @pl.when(j >= 2)
    def _():
        a = jnp.dot(n_ref[...], w_ref[...], preferred_element_type=F32)
        gate_ref[...] = jax.nn.sigmoid(a).astype(BF16)


def _in_proj_q(x, g, w_qg, w_f, b_f, *, tm):
    t = x.shape[0]
    tn = SEG_COLS
    blocks = (tm * D_MODEL * 4 + D_MODEL * tn * 2 + D_MODEL * V7X_LANES * 2
              + 2 * tm * tn * 2 + tm * V7X_LANES * 4 + tm * D_MODEL * 2)
    return pl.pallas_call(
        _inproj_q_kernel,
        grid=(t // tm, 6),
        in_specs=[
            pl.BlockSpec((tm, D_MODEL), lambda i, j: (i, 0)),
            pl.BlockSpec((1, D_MODEL), lambda i, j: (0, 0)),
            pl.BlockSpec((D_MODEL, tn), lambda i, j: (0, j)),
            pl.BlockSpec((D_MODEL, V7X_LANES), lambda i, j: (0, 0)),
            pl.BlockSpec((1, V7X_LANES), lambda i, j: (0, 0)),
        ],
        out_specs=[
            pl.BlockSpec((tm, tn), lambda i, j: (i, jnp.minimum(j, 1))),
            pl.BlockSpec((tm, tn), lambda i, j: (i, jnp.clip(j - 2, 0, 3))),
            pl.BlockSpec((tm, N_HEADS), lambda i, j: (i, 0)),
            pl.BlockSpec((tm, D_MODEL), lambda i, j: (i, 0)),
        ],
        out_shape=[
            jax.ShapeDtypeStruct((t, 2 * SEG_COLS), BF16),
            jax.ShapeDtypeStruct((t, 4 * SEG_COLS), BF16),
            jax.ShapeDtypeStruct((t, N_HEADS), F32),
            jax.ShapeDtypeStruct((t, D_MODEL), BF16),
        ],
        compiler_params=pltpu.CompilerParams(
            dimension_semantics=("parallel", "arbitrary"),
            vmem_limit_bytes=_vmem_limit(blocks, 0, 6 * MIB)),
        name="in_proj_q",
    )(x, g, w_qg, w_f, b_f)


def _inproj_kv_kernel(n_ref, w_ref, kv_ref, ka_ref, va_ref, kb_ref, vb_ref, *, a_rows, a_period):
    i = pl.program_id(0)
    j = pl.program_id(1)
    tm = n_ref.shape[0]
    is_tail = (i % a_period) == (a_period - 1)

    def project():
        return jnp.dot(n_ref[...], w_ref[...], preferred_element_type=F32)

    @pl.when(jnp.logical_and(j < 2, jnp.logical_not(is_tail)))
    def _():
        kv_ref[...] = project().astype(BF16)

    for col, tail_ref in ((0, ka_ref), (1, va_ref)):
        @pl.when(jnp.logical_and(j == col, is_tail))
        def _(tail_ref=tail_ref):
            r = project()
            kv_ref[...] = r.astype(BF16)
            _store_heads(tail_ref, r[tm - a_rows:, :])

    for col, full_ref in ((2, kb_ref), (3, vb_ref)):
        @pl.when(j == col)
        def _(full_ref=full_ref):
            r = project()
            kv_ref[...] = r.astype(BF16)
            _store_heads(full_ref, r)


def _in_proj_kv(n, w_kv, *, a_rows, a_period, tm):
    t = n.shape[0]
    tn = SEG_COLS
    n_a = (t // tm) // a_period * a_rows
    blocks = (tm * D_MODEL * 2 + D_MODEL * tn * 2 + tm * tn * 2
              + 2 * a_rows * tn * 4 + 2 * tm * tn * 4)
    kern = functools.partial(_inproj_kv_kernel, a_rows=a_rows, a_period=a_period)
    head_block = lambda rows: (rows, N_HEADS, HEAD_DIM)
    return pl.pallas_call(
        kern,
        grid=(t // tm, 4),
        in_specs=[
            pl.BlockSpec((tm, D_MODEL), lambda i, j: (i, 0)),
            pl.BlockSpec((D_MODEL, tn), lambda i, j: (0, j)),
        ],
        out_specs=[
            pl.BlockSpec((tm, tn), lambda i, j: (i, j)),
            pl.BlockSpec(head_block(a_rows), lambda i, j: (i // a_period, 0, 0)),
            pl.BlockSpec(head_block(a_rows), lambda i, j: (i // a_period, 0, 0)),
            pl.BlockSpec(head_block(tm), lambda i, j: (i, 0, 0)),
            pl.BlockSpec(head_block(tm), lambda i, j: (i, 0, 0)),
        ],
        out_shape=[
            jax.ShapeDtypeStruct((t, 4 * SEG_COLS), BF16),
            jax.ShapeDtypeStruct(head_block(n_a), F32),
            jax.ShapeDtypeStruct(head_block(n_a), F32),
            jax.ShapeDtypeStruct(head_block(t), F32),
            jax.ShapeDtypeStruct(head_block(t), F32),
        ],
        compiler_params=pltpu.CompilerParams(
            dimension_semantics=("parallel", "arbitrary"),
            vmem_limit_bytes=_vmem_limit(blocks, 0, 6 * MIB)),
        name="in_proj_kv",
    )(n, w_kv)


def _cumsum_kernel(x_ref, o_ref):
    rows, length = x_ref.shape
    r_i = lax.broadcasted_iota(jnp.int32, (V7X_LANES, V7X_LANES), 0)
    c_i = lax.broadcasted_iota(jnp.int32, (V7X_LANES, V7X_LANES), 1)
    tri = (r_i <= c_i).astype(BF16)
    carry = jnp.zeros((rows, 1), F32)
    for c in range(length // V7X_LANES):
        x = x_ref[:, c * V7X_LANES:(c + 1) * V7X_LANES]
        hi = x.astype(BF16)
        r1 = x - hi.astype(F32)
        mid = r1.astype(BF16)
        lo = (r1 - mid.astype(F32)).astype(BF16)
        blk = (jnp.dot(hi, tri, preferred_element_type=F32)
               + jnp.dot(mid, tri, preferred_element_type=F32)
               + jnp.dot(lo, tri, preferred_element_type=F32)) + carry
        o_ref[:, c * V7X_LANES:(c + 1) * V7X_LANES] = blk
        carry = blk[:, V7X_LANES - 1:V7X_LANES]


def _cumsum_lanes(x):
    return pl.pallas_call(
        _cumsum_kernel,
        out_shape=jax.ShapeDtypeStruct(x.shape, F32),
        name="logf_cumsum",
    )(x)


def _rel_bias_kernel(g_ref, raw_ref, band_ref):
    row = lax.broadcasted_iota(jnp.int32, (BAND_TQ, BAND_W), 0)
    col = lax.broadcasted_iota(jnp.int32, (BAND_TQ, BAND_W), 1)
    q_chunk = row // CHUNK
    k_chunk = col // CHUNK - N_PAST_CHUNKS
    in_band = jnp.logical_and(k_chunk >= q_chunk - N_PAST_CHUNKS, k_chunk <= q_chunk)
    for h in range(N_HEADS):
        rows = jnp.broadcast_to(g_ref[h:h + 1, :], (BAND_TQ, BAND_G))
        toeplitz = pltpu.roll(rows, 0, 1, stride=1, stride_axis=0)[:, :BAND_W] * LOG2E
        raw_ref[h] = toeplitz
        band_ref[h] = jnp.where(in_band, toeplitz, NEG)


def _rel_bias_tiles(table):
    far_past = table[:, 2 * REL_CLIP:]
    far_future = table[:, :1]
    near = jnp.flip(table[:, :2 * REL_CLIP], axis=1)
    n_past = A_REACH - REL_CLIP + 1
    n_future = BAND_W - n_past - near.shape[1]
    g = jnp.concatenate([jnp.broadcast_to(far_past, (N_HEADS, n_past)), near,
                         jnp.broadcast_to(far_future, (N_HEADS, n_future)),
                         jnp.broadcast_to(far_past, (N_HEADS, BAND_G - BAND_W))], axis=1)
    shape = jax.ShapeDtypeStruct((N_HEADS, BAND_TQ, BAND_W), F32)
    return pl.pallas_call(_rel_bias_kernel, out_shape=[shape, shape], name="rel_bias")(g.astype(F32))


def _band_prompt_kernel(q_ref, k_ref, v_ref, bias_ref, o_ref, vp_ref):
    s_len = q_ref.shape[0]
    _widen_values(v_ref, vp_ref)
    n_tiles = s_len // BAND_TQ

    def logits(t):
        q0 = t * BAND_TQ
        k0 = max(q0 - A_REACH, 0)
        n_keys = q0 + BAND_TQ - k0
        s = lax.dot_general(q_ref[q0:q0 + BAND_TQ, :], k_ref[k0:k0 + n_keys, :], _NT,
                            preferred_element_type=F32)
        return [(s * SCALE2 + bias_ref[0, :, BAND_W - n_keys:], vp_ref[k0:k0 + n_keys, :])]

    def store(t, o):
        o_ref[t * BAND_TQ:(t + 1) * BAND_TQ, :] = o.astype(BF16)

    _attention_tiles(n_tiles, logits, store)


def _band_prompt(q, kv, bias, batch, s_len):
    t = batch * s_len
    blocks = 4 * s_len * HEAD_DIM * 2 + BAND_TQ * BAND_W * 4
    return pl.pallas_call(
        _band_prompt_kernel,
        grid=(batch, N_HEADS),
        in_specs=[
            pl.BlockSpec((s_len, HEAD_DIM), lambda b, h: (b, h)),
            pl.BlockSpec((s_len, HEAD_DIM), lambda b, h: (b, h)),
            pl.BlockSpec((s_len, HEAD_DIM), lambda b, h: (b, N_HEADS + h)),
            pl.BlockSpec((1, BAND_TQ, BAND_W), lambda b, h: (h, 0, 0)),
        ],
        out_specs=pl.BlockSpec((s_len, HEAD_DIM), lambda b, h: (b, h)),
        out_shape=jax.ShapeDtypeStruct((t, W_MIX), BF16),
        scratch_shapes=[pltpu.VMEM((s_len, PV_COLS), BF16)],
        compiler_params=pltpu.CompilerParams(
            dimension_semantics=("parallel", "parallel"),
            vmem_limit_bytes=_vmem_limit(blocks, s_len * PV_COLS * 2, 8 * MIB)),
        name="band_prompt",
    )(q, kv, kv, bias)


def _band_sample_kernel(q_ref, kn_ref, vn_ref, kn32_ref, vn32_ref, ck_ref, cv_ref, bias_ref,
                        o_ref, sk_ref, sv_ref):
    n_cache = ck_ref.shape[1]
    t_new = q_ref.shape[0]
    for h in range(N_HEADS):
        hs = slice(h * HEAD_DIM, (h + 1) * HEAD_DIM)
        q = q_ref[:, hs]
        kc = ck_ref[0, :, h, :].astype(BF16)
        vc = cv_ref[0, :, h, :].astype(BF16)
        s1 = (lax.dot_general(q, kc, _NT, preferred_element_type=F32) * SCALE2
              + bias_ref[h, :t_new, :n_cache])
        s2 = (lax.dot_general(q, kn_ref[:, hs], _NT, preferred_element_type=F32) * SCALE2
              + bias_ref[h, :t_new, n_cache:n_cache + t_new])
        o_ref[:, hs] = _softmax_pv([(s1, vc), (s2, vn_ref[:, hs])]).astype(BF16)
    for cache_ref, new_ref, state_ref in ((ck_ref, kn32_ref, sk_ref), (cv_ref, vn32_ref, sv_ref)):
        state_ref[0, :n_cache - t_new] = cache_ref[0, t_new:]
        state_ref[0, n_cache - t_new:] = new_ref[...]


def _band_sample(q, kv, ka32, va32, cache_k, cache_v, bias, batch, t_new):
    n_cache = cache_k.shape[1]
    cache_block = (1, n_cache, N_HEADS, HEAD_DIM)
    new_block = (t_new, N_HEADS, HEAD_DIM)
    blocks = 4 * t_new * W_MIX * 2 + 2 * t_new * W_MIX * 4 + 4 * n_cache * W_MIX * 4 + bias.size * 4
    return pl.pallas_call(
        _band_sample_kernel,
        grid=(batch,),
        in_specs=[
            pl.BlockSpec((t_new, W_MIX), lambda b: (b, 0)),
            pl.BlockSpec((t_new, W_MIX), lambda b: (b, 0)),
            pl.BlockSpec((t_new, W_MIX), lambda b: (b, 1)),
            pl.BlockSpec(new_block, lambda b: (b, 0, 0)),
            pl.BlockSpec(new_block, lambda b: (b, 0, 0)),
            pl.BlockSpec(cache_block, lambda b: (b, 0, 0, 0)),
            pl.BlockSpec(cache_block, lambda b: (b, 0, 0, 0)),
            pl.BlockSpec(bias.shape, lambda b: (0, 0, 0)),
        ],
        out_specs=[
            pl.BlockSpec((t_new, W_MIX), lambda b: (b, 0)),
            pl.BlockSpec(cache_block, lambda b: (b, 0, 0, 0)),
            pl.BlockSpec(cache_block, lambda b: (b, 0, 0, 0)),
        ],
        out_shape=[
            jax.ShapeDtypeStruct((batch * t_new, W_MIX), BF16),
            jax.ShapeDtypeStruct(cache_k.shape, F32),
            jax.ShapeDtypeStruct(cache_v.shape, F32),
        ],
        compiler_params=pltpu.CompilerParams(
            dimension_semantics=("parallel",),
            vmem_limit_bytes=_vmem_limit(blocks, 0, 4 * MIB)),
        name="band_sample",
    )(q, kv, kv, ka32, va32, cache_k, cache_v, bias)


def _fox_prompt_kernel(q_ref, k_ref, v_ref, f_ref, o_ref, vp_ref):
    s_len = q_ref.shape[0]
    _widen_values(v_ref, vp_ref)
    row = lax.broadcasted_iota(jnp.int32, (FOX_T, FOX_T), 0)
    col = lax.broadcasted_iota(jnp.int32, (FOX_T, FOX_T), 1)
    causal = row >= col
    f2 = f_ref[0] * LOG2E
    n_tiles = s_len // FOX_T

    def logits(qi):
        q0 = qi * FOX_T
        q = q_ref[q0:q0 + FOX_T, :]
        s_diag = (lax.dot_general(q, k_ref[q0:q0 + FOX_T, :], _NT, preferred_element_type=F32) * SCALE2
                  - f2[:, q0:q0 + FOX_T])
        parts = [(jnp.where(causal, s_diag, NEG), vp_ref[q0:q0 + FOX_T, :])]
        if qi:
            s_past = (lax.dot_general(q, k_ref[0:q0, :], _NT, preferred_element_type=F32) * SCALE2
                      - f2[:, 0:q0])
            parts.append((s_past, vp_ref[0:q0, :]))
        return parts

    def store(qi, o):
        o_ref[qi * FOX_T:(qi + 1) * FOX_T, :] = o.astype(BF16)

    _attention_tiles(n_tiles, logits, store)


def _fox_prompt(q, kv, f_rows, batch, s_len):
    t = batch * s_len
    blocks = 4 * s_len * HEAD_DIM * 2 + 8 * s_len * 4
    return pl.pallas_call(
        _fox_prompt_kernel,
        grid=(batch, N_HEADS),
        in_specs=[
            pl.BlockSpec((s_len, HEAD_DIM), lambda b, h: (b, N_HEADS + h)),
            pl.BlockSpec((s_len, HEAD_DIM), lambda b, h: (b, 2 * N_HEADS + h)),
            pl.BlockSpec((s_len, HEAD_DIM), lambda b, h: (b, 3 * N_HEADS + h)),
            pl.BlockSpec((1, 1, s_len), lambda b, h: (b * N_HEADS + h, 0, 0)),
        ],
        out_specs=pl.BlockSpec((s_len, HEAD_DIM), lambda b, h: (b, h)),
        out_shape=jax.ShapeDtypeStruct((t, W_MIX), BF16),
        scratch_shapes=[pltpu.VMEM((s_len, PV_COLS), BF16)],
        compiler_params=pltpu.CompilerParams(
            dimension_semantics=("parallel", "parallel"),
            vmem_limit_bytes=_vmem_limit(blocks, s_len * PV_COLS * 2, 6 * FOX_T * s_len * 4)),
        name="fox_prompt",
    )(q, kv, kv, f_rows)


def _fox_sample_kernel(q_ref, kn_ref, vn_ref, ck_ref, cv_ref, f_ref, o_ref):
    n_cache = ck_ref.shape[1]
    t_new = q_ref.shape[0]
    row = lax.broadcasted_iota(jnp.int32, (t_new, t_new), 0)
    col = lax.broadcasted_iota(jnp.int32, (t_new, t_new), 1)
    causal = row >= col
    for h in range(N_HEADS):
        hs = slice(h * HEAD_DIM, (h + 1) * HEAD_DIM)
        q = q_ref[:, hs]
        f2 = f_ref[0, h:h + 1, :] * LOG2E
        kc = ck_ref[0, :, h, :].astype(BF16)
        vc = cv_ref[0, :, h, :].astype(BF16)
        s1 = lax.dot_general(q, kc, _NT, preferred_element_type=F32) * SCALE2 - f2[:, :n_cache]
        s2 = (lax.dot_general(q, kn_ref[:, hs], _NT, preferred_element_type=F32) * SCALE2
              - f2[:, n_cache:n_cache + t_new])
        parts = [(s1, vc), (jnp.where(causal, s2, NEG), vn_ref[:, hs])]
        o_ref[:, hs] = _softmax_pv(parts).astype(BF16)


def _fox_sample(q, kv, cache_k, cache_v, f_rows, batch, t_new):
    n_cache = cache_k.shape[1]
    f_len = f_rows.shape[-1]
    cache_block = (1, n_cache, N_HEADS, HEAD_DIM)
    blocks = 4 * t_new * W_MIX * 2 + 2 * n_cache * W_MIX * 4 + N_HEADS * f_len * 4
    return pl.pallas_call(
        _fox_sample_kernel,
        grid=(batch,),
        in_specs=[
            pl.BlockSpec((t_new, W_MIX), lambda b: (b, 1)),
            pl.BlockSpec((t_new, W_MIX), lambda b: (b, 2)),
            pl.BlockSpec((t_new, W_MIX), lambda b: (b, 3)),
            pl.BlockSpec(cache_block, lambda b: (b, 0, 0, 0)),
            pl.BlockSpec(cache_block, lambda b: (b, 0, 0, 0)),
            pl.BlockSpec((1, N_HEADS, f_len), lambda b: (b, 0, 0)),
        ],
        out_specs=pl.BlockSpec((t_new, W_MIX), lambda b: (b, 0)),
        out_shape=jax.ShapeDtypeStruct((batch * t_new, W_MIX), BF16),
        compiler_params=pltpu.CompilerParams(
            dimension_semantics=("parallel",),
            vmem_limit_bytes=_vmem_limit(blocks, 0, 8 * MIB)),
        name="fox_sample",
    )(q, kv, kv, cache_k, cache_v, f_rows)


def _merge_kernel(oa_ref, ob_ref, gate_ref, x_ref, wa_ref, wb_ref, wo_ref, h_ref):
    ya = jnp.dot(oa_ref[...], wa_ref[...], preferred_element_type=F32)
    yb = jnp.dot(ob_ref[...], wb_ref[...], preferred_element_type=F32)
    m = (gate_ref[:, :D_MODEL].astype(F32) * ya + gate_ref[:, D_MODEL:].astype(F32) * yb)
    h_ref[...] = x_ref[...] + jnp.dot(m.astype(BF16), wo_ref[...], preferred_element_type=F32)


def _resident(shape):
    return pl.BlockSpec(shape, lambda *_: (0,) * len(shape), pipeline_mode=pl.Buffered(1))


def _merge(oa, ob, gates, x, w_a, w_b, w_o, *, tm):
    t = x.shape[0]
    blocks = 2 * tm * W_MIX * 2 + tm * 2 * D_MODEL * 2 + 2 * tm * D_MODEL * 4
    weights = (2 * W_MIX * D_MODEL + D_MODEL * D_MODEL) * 2
    return pl.pallas_call(
        _merge_kernel,
        grid=(t // tm,),
        in_specs=[
            pl.BlockSpec((tm, W_MIX), lambda i: (i, 0)),
            pl.BlockSpec((tm, W_MIX), lambda i: (i, 0)),
            pl.BlockSpec((tm, 2 * D_MODEL), lambda i: (i, 0)),
            pl.BlockSpec((tm, D_MODEL), lambda i: (i, 0)),
            _resident((W_MIX, D_MODEL)),
            _resident((W_MIX, D_MODEL)),
            _resident((D_MODEL, D_MODEL)),
        ],
        out_specs=pl.BlockSpec((tm, D_MODEL), lambda i: (i, 0)),
        out_shape=jax.ShapeDtypeStruct((t, D_MODEL), F32),
        compiler_params=pltpu.CompilerParams(
            dimension_semantics=("parallel",),
            vmem_limit_bytes=_vmem_limit(blocks, weights, 4 * tm * D_MODEL * 4)),
        name="merge",
    )(oa, ob, gates, x, w_a, w_b, w_o)


def _ffn_kernel(h_ref, g_ref, wu_ref, wd_ref, o_ref, n_ref):
    @pl.when(pl.program_id(1) == 0)
    def _():
        _norm_to_bf16(h_ref, g_ref, n_ref)
        o_ref[...] = h_ref[...]

    a = jnp.dot(n_ref[...], wu_ref[...], preferred_element_type=F32)
    r = jnp.square(jnp.maximum(a, 0.0)).astype(BF16)
    o_ref[...] += jnp.dot(r, wd_ref[...], preferred_element_type=F32)


def _ffn(h, g, w_up, w_down, *, tm, tf):
    t = h.shape[0]
    blocks = 2 * tm * D_MODEL * 4 + 2 * D_MODEL * tf * 2
    scratch = tm * D_MODEL * 2
    return pl.pallas_call(
        _ffn_kernel,
        grid=(t // tm, D_FF // tf),
        in_specs=[
            pl.BlockSpec((tm, D_MODEL), lambda i, f: (i, 0)),
            pl.BlockSpec((1, D_MODEL), lambda i, f: (0, 0)),
            pl.BlockSpec((D_MODEL, tf), lambda i, f: (0, f)),
            pl.BlockSpec((tf, D_MODEL), lambda i, f: (f, 0)),
        ],
        out_specs=pl.BlockSpec((tm, D_MODEL), lambda i, f: (i, 0)),
        out_shape=jax.ShapeDtypeStruct((t, D_MODEL), F32),
        scratch_shapes=[pltpu.VMEM((tm, D_MODEL), BF16)],
        compiler_params=pltpu.CompilerParams(
            dimension_semantics=("parallel", "arbitrary"),
            vmem_limit_bytes=_vmem_limit(blocks, scratch, 2 * tm * tf * 4)),
        name="ffn",
    )(h, g, w_up, w_down)


def _ple_kernel(h_ref, p_ref, gp_ref, gf_ref, wg_ref, wp_ref, y_ref):
    h = h_ref[...]
    n = _rmsnorm_rows(h, gp_ref[...]).astype(BF16)
    gate = jax.nn.sigmoid(jnp.dot(n, wg_ref[...], preferred_element_type=F32))
    proj = jnp.dot(p_ref[...].astype(BF16), wp_ref[...], preferred_element_type=F32)
    y_ref[...] = _rmsnorm_rows(h + proj * gate, gf_ref[...])


def _ple_final(h, p, g_ple, g_final, w_gate, w_proj, *, tm):
    t = h.shape[0]
    blocks = 2 * tm * D_MODEL * 4 + tm * D_PLE * 4
    weights = (D_MODEL * D_MODEL + D_PLE * D_MODEL) * 2
    return pl.pallas_call(
        _ple_kernel,
        grid=(t // tm,),
        in_specs=[
            pl.BlockSpec((tm, D_MODEL), lambda i: (i, 0)),
            pl.BlockSpec((tm, D_PLE), lambda i: (i, 0)),
            pl.BlockSpec((1, D_MODEL), lambda i: (0, 0)),
            pl.BlockSpec((1, D_MODEL), lambda i: (0, 0)),
            _resident((D_MODEL, D_MODEL)),
            _resident((D_PLE, D_MODEL)),
        ],
        out_specs=pl.BlockSpec((tm, D_MODEL), lambda i: (i, 0)),
        out_shape=jax.ShapeDtypeStruct((t, D_MODEL), F32),
        compiler_params=pltpu.CompilerParams(
            dimension_semantics=("parallel",),
            vmem_limit_bytes=_vmem_limit(blocks, weights, 4 * tm * D_MODEL * 4)),
        name="ple_final",
    )(h, p, g_ple, g_final, w_gate, w_proj)


def _pad_lanes(x, multiple):
    pad = (-x.shape[-1]) % multiple
    return jnp.pad(x, ((0, 0),) * (x.ndim - 1) + ((0, pad),)) if pad else x


def _token_tile(t, preferred):
    return preferred if t % preferred == 0 else t


def kernel(x_prompt, x_sample, p_prompt, p_sample, cache_a_k, cache_a_v, cache_b_k, cache_b_v,
           cache_b_logf, g_mix, w_in, b_f, rel_bias, w_a_proj, w_b_proj, w_o, g_mlp, w_up, w_down,
           g_ple, w_ple_gate, w_ple_proj, g_final):
    depth = w_in.shape[0]
    assert depth == 1, "single-layer step"
    batch, s_len, _ = x_prompt.shape
    dec_batch, t_new, _ = x_sample.shape
    n_cache_a = cache_a_k.shape[2]
    assert s_len % FOX_T == 0 and s_len >= A_REACH and t_new == CHUNK and n_cache_a == A_REACH

    w = w_in[0]
    seg = lambda k: w[:, k * SEG_COLS:(k + 1) * SEG_COLS]
    n_qkv = 6 * SEG_COLS
    w_qg = jnp.concatenate([seg(0), seg(3), w[:, n_qkv + N_HEADS:]], axis=1).astype(BF16)
    w_kv = jnp.concatenate([seg(1), seg(2), seg(4), seg(5)], axis=1).astype(BF16)
    w_f = _pad_lanes(w[:, n_qkv:n_qkv + N_HEADS], V7X_LANES).astype(BF16)
    b_f_row = _pad_lanes(b_f[0][None, :].astype(F32), V7X_LANES)
    g_mix_row = g_mix[0][None, :].astype(F32)
    g_mlp_row = g_mlp[0][None, :].astype(F32)
    g_ple_row = g_ple[0][None, :].astype(F32)
    g_final_row = g_final[None, :].astype(F32)
    w_a = w_a_proj[0].astype(BF16)
    w_b = w_b_proj[0].astype(BF16)
    w_o_b = w_o[0].astype(BF16)
    w_up_b = w_up[0].astype(BF16)
    w_down_b = w_down[0].astype(BF16)
    w_pg = w_ple_gate[0].astype(BF16)
    w_pp = w_ple_proj[0].astype(BF16)
    bias_raw, bias_band = _rel_bias_tiles(rel_bias[0])

    def project(x2d, *, a_rows, a_period, tm):
        q, gates, lf, n = _in_proj_q(x2d, g_mix_row, w_qg, w_f, b_f_row, tm=tm)
        return (q, gates, lf) + tuple(_in_proj_kv(n, w_kv, a_rows=a_rows, a_period=a_period, tm=tm))

    def finish(x2d, p2d, oa, ob, gates):
        t = x2d.shape[0]
        h1 = _merge(oa, ob, gates, x2d, w_a, w_b, w_o_b, tm=_token_tile(t, 256))
        h2 = _ffn(h1, g_mlp_row, w_up_b, w_down_b, tm=_token_tile(t, 1024), tf=512)
        return _ple_final(h2, p2d, g_ple_row, g_final_row, w_pg, w_pp, tm=_token_tile(t, 512))

    tp = batch * s_len
    xp = x_prompt.reshape(tp, D_MODEL)
    tm_p = _token_tile(s_len, 1024)
    q_p, gates_p, lf_p, kv_p, ka_p, va_p, kb_p, vb_p = project(
        xp, a_rows=min(A_REACH, tm_p), a_period=s_len // tm_p, tm=tm_p)
    lf_rows = lf_p.reshape(batch, s_len, N_HEADS).transpose(0, 2, 1).reshape(batch * N_HEADS, s_len)
    f_p = _cumsum_lanes(lf_rows).reshape(batch * N_HEADS, 1, s_len)
    oa_p = _band_prompt(q_p, kv_p, bias_band, batch, s_len)
    ob_p = _fox_prompt(q_p, kv_p, f_p, batch, s_len)
    y_prompt = finish(xp, p_prompt[0].reshape(tp, D_PLE), oa_p, ob_p, gates_p)

    ts = dec_batch * t_new
    xs = x_sample.reshape(ts, D_MODEL)
    q_s, gates_s, lf_s, kv_s, ka_s, va_s, kb_s, vb_s = project(xs, a_rows=ts, a_period=1, tm=ts)
    lf_new = lf_s.reshape(dec_batch, t_new, N_HEADS)
    lf_all = jnp.concatenate([cache_b_logf[0].astype(F32), lf_new], axis=1)
    lf_all = _pad_lanes(lf_all.transpose(0, 2, 1), V7X_LANES)
    f_s = _cumsum_lanes(lf_all.reshape(dec_batch * N_HEADS, -1)).reshape(dec_batch, N_HEADS, -1)
    oa_s, sk_s, sv_s = _band_sample(q_s, kv_s, ka_s, va_s, cache_a_k[0], cache_a_v[0], bias_raw,
                                    dec_batch, t_new)
    ob_s = _fox_sample(q_s, kv_s, cache_b_k[0], cache_b_v[0], f_s, dec_batch, t_new)
    y_sample = finish(xs, p_sample[0].reshape(ts, D_PLE), oa_s, ob_s, gates_s)

    def state(a, b):
        return a.reshape(1, b, -1, N_HEADS, HEAD_DIM)

    return (y_prompt.reshape(batch, s_len, D_MODEL),
            y_sample.reshape(dec_batch, t_new, D_MODEL),
            state(ka_p, batch), state(va_p, batch), state(kb_p, batch), state(vb_p, batch),
            lf_p.reshape(1, batch, s_len, N_HEADS),
            sk_s[None], sv_s[None],
            state(kb_s, dec_batch), state(vb_s, dec_batch),
            lf_new[None])
```

```python
import functools
import math

import jax
import jax.numpy as jnp
from jax import lax
from jax.experimental import pallas as pl
from jax.experimental.pallas import tpu as pltpu

F32 = jnp.float32
BF16 = jnp.bfloat16

D_MODEL = 2048
CHUNK = 64
N_PAST_CHUNKS = 8
A_REACH = N_PAST_CHUNKS * CHUNK
HEAD_DIM = 128
N_HEADS = 8
W_MIX = N_HEADS * HEAD_DIM
REL_CLIP = 128
D_FF = 4 * D_MODEL
D_PLE = 256
RMS_EPS = 1e-6
SCALE = HEAD_DIM ** -0.5
NEG = -1e30
LOG2E = math.log2(math.e)
SCALE2 = SCALE * LOG2E

V7X_LANES = 128
V7X_VMEM_LIMIT_CAP = 56 * 1024 * 1024
MIB = 1024 * 1024

NORM_ROWS = 128
BAND_TQ = 256
BAND_W = A_REACH + BAND_TQ
BAND_G = BAND_W + BAND_TQ
PV_COLS = 2 * HEAD_DIM
FOX_T = 256
SEG_COLS = 1024

_NT = (((1,), (1,)), ((), ()))


def _vmem_limit(block_bytes, scratch_bytes, temp_bytes):
    est = 2 * block_bytes + scratch_bytes + temp_bytes
    return int(min(max(est, 16 * MIB), V7X_VMEM_LIMIT_CAP))


def _rmsnorm_rows(x, g):
    ms = jnp.mean(x * x, axis=-1, keepdims=True)
    return (x * lax.rsqrt(ms + RMS_EPS)) * g


def _norm_to_bf16(x_ref, g_ref, n_ref):
    rows = min(NORM_ROWS, x_ref.shape[0])

    def body(c, carry):
        r = pl.ds(pl.multiple_of(c * rows, rows), rows)
        n_ref[r, :] = _rmsnorm_rows(x_ref[r, :], g_ref[...]).astype(BF16)
        return carry

    lax.fori_loop(0, x_ref.shape[0] // rows, body, 0)


def _softmax_pv(parts):
    m = functools.reduce(jnp.maximum, [jnp.max(s, axis=-1, keepdims=True) for s, _ in parts])
    ps = [jnp.exp2(s - m) for s, _ in parts]
    l = functools.reduce(jnp.add, [jnp.sum(p, axis=-1, keepdims=True) for p in ps])
    o = functools.reduce(jnp.add, [jnp.dot(p.astype(BF16), v, preferred_element_type=F32)
                                   for p, (_, v) in zip(ps, parts)])
    return o / l


def _widen_values(v_ref, vp_ref):
    lane = lax.broadcasted_iota(jnp.int32, v_ref.shape, 1)
    vp_ref[:, :HEAD_DIM] = v_ref[...]
    vp_ref[:, HEAD_DIM:] = jnp.where(lane == 0, 1.0, 0.0).astype(BF16)


def _softmax_weights(parts):
    m = functools.reduce(jnp.maximum, [jnp.max(s, axis=-1, keepdims=True) for s, _ in parts])
    return [(jnp.exp2(s - m).astype(BF16), v) for s, v in parts]


def _weighted_values_wide(weighted):
    o = functools.reduce(jnp.add, [jnp.dot(p, v, preferred_element_type=F32) for p, v in weighted])
    return o[:, :HEAD_DIM] / o[:, HEAD_DIM:HEAD_DIM + 1]


def _attention_tiles(n_tiles, logits, store):
    stage1 = {0: logits(0)}
    if n_tiles > 1:
        stage1[1] = logits(1)
    stage2 = {0: _softmax_weights(stage1.pop(0))}
    for t in range(n_tiles):
        if t + 2 < n_tiles:
            stage1[t + 2] = logits(t + 2)
        if t + 1 < n_tiles:
            stage2[t + 1] = _softmax_weights(stage1.pop(t + 1))
        store(t, _weighted_values_wide(stage2.pop(t)))


def _store_heads(ref, rows):
    stacked = jnp.stack([rows[:, h * HEAD_DIM:(h + 1) * HEAD_DIM] for h in range(N_HEADS)], axis=0)
    ref[...] = pltpu.einshape("htd->thd", stacked)


def _head_major(rows):
    return pltpu.einshape("thd->htd", rows)


def _inproj_q_kernel(x_ref, g_ref, w_ref, wf_ref, bf_ref, q_ref, gate_ref, lf_ref, n_ref):
    j = pl.program_id(1)

    @pl.when(j == 0)
    def _():
        _norm_to_bf16(x_ref, g_ref, n_ref)
        z = jnp.dot(n_ref[...], wf_ref[...], preferred_element_type=F32) + bf_ref[...]
        lf = jnp.minimum(z, 0.0) - jnp.log1p(jnp.exp(-jnp.abs(z)))
        lf_ref[...] = lf[:, :N_HEADS]

    @pl.when(j < 2)
    def _():
        q_ref[...] = jnp.dot(n_ref[...], w_ref[...], preferred_element_type=F32).astype(BF16)

    @pl.when(j >= 2)
    def _():
        a = jnp.dot(n_ref[...], w_ref[...], preferred_element_type=F32)
        gate_ref[...] = jax.nn.sigmoid(a).astype(BF16)


def _in_proj_q(x, g, w_qg, w_f, b_f, *, tm):
    t = x.shape[0]
    tn = SEG_COLS
    blocks = (tm * D_MODEL * 4 + D_MODEL * tn * 2 + D_MODEL * V7X_LANES * 2
              + 2 * tm * tn * 2 + tm * V7X_LANES * 4 + tm * D_MODEL * 2)
    return pl.pallas_call(
        _inproj_q_kernel,
        grid=(t // tm, 6),
        in_specs=[
            pl.BlockSpec((tm, D_MODEL), lambda i, j: (i, 0)),
            pl.BlockSpec((1, D_MODEL), lambda i, j: (0, 0)),
            pl.BlockSpec((D_MODEL, tn), lambda i, j: (0, j)),
            pl.BlockSpec((D_MODEL, V7X_LANES), lambda i, j: (0, 0)),
            pl.BlockSpec((1, V7X_LANES), lambda i, j: (0, 0)),
        ],
        out_specs=[
            pl.BlockSpec((tm, tn), lambda i, j: (i, jnp.minimum(j, 1))),
            pl.BlockSpec((tm, tn), lambda i, j: (i, jnp.clip(j - 2, 0, 3))),
            pl.BlockSpec((tm, N_HEADS), lambda i, j: (i, 0)),
            pl.BlockSpec((tm, D_MODEL), lambda i, j: (i, 0)),
        ],
        out_shape=[
            jax.ShapeDtypeStruct((t, 2 * SEG_COLS), BF16),
            jax.ShapeDtypeStruct((t, 4 * SEG_COLS), BF16),
            jax.ShapeDtypeStruct((t, N_HEADS), F32),
            jax.ShapeDtypeStruct((t, D_MODEL), BF16),
        ],
        compiler_params=pltpu.CompilerParams(
            dimension_semantics=("parallel", "arbitrary"),
            vmem_limit_bytes=_vmem_limit(blocks, 0, 6 * MIB)),
        name="in_proj_q",
    )(x, g, w_qg, w_f, b_f)


def _inproj_kv_kernel(n_ref, w_ref, kv_ref, ka_ref, va_ref, kb_ref, vb_ref, *, a_rows, a_period):
    i = pl.program_id(0)
    j = pl.program_id(1)
    tm = n_ref.shape[0]
    is_tail = (i % a_period) == (a_period - 1)

    def project():
        return jnp.dot(n_ref[...], w_ref[...], preferred_element_type=F32)

    @pl.when(jnp.logical_and(j < 2, jnp.logical_not(is_tail)))
    def _():
        kv_ref[...] = project().astype(BF16)

    for col, tail_ref in ((0, ka_ref), (1, va_ref)):
        @pl.when(jnp.logical_and(j == col, is_tail))
        def _(tail_ref=tail_ref):
            r = project()
            kv_ref[...] = r.astype(BF16)
            _store_heads(tail_ref, r[tm - a_rows:, :])

    for col, full_ref in ((2, kb_ref), (3, vb_ref)):
        @pl.when(j == col)
        def _(full_ref=full_ref):
            r = project()
            kv_ref[...] = r.astype(BF16)
            _store_heads(full_ref, r)


def _in_proj_kv(n, w_kv, *, a_rows, a_period, tm):
    t = n.shape[0]
    tn = SEG_COLS
    n_a = (t // tm) // a_period * a_rows
    blocks = (tm * D_MODEL * 2 + D_MODEL * tn * 2 + tm * tn * 2
              + 2 * a_rows * tn * 4 + 2 * tm * tn * 4)
    kern = functools.partial(_inproj_kv_kernel, a_rows=a_rows, a_period=a_period)
    head_block = lambda rows: (rows, N_HEADS, HEAD_DIM)
    return pl.pallas_call(
        kern,
        grid=(t // tm, 4),
        in_specs=[
            pl.BlockSpec((tm, D_MODEL), lambda i, j: (i, 0)),
            pl.BlockSpec((D_MODEL, tn), lambda i, j: (0, j)),
        ],
        out_specs=[
            pl.BlockSpec((tm, tn), lambda i, j: (i, j)),
            pl.BlockSpec(head_block(a_rows), lambda i, j: (i // a_period, 0, 0)),
            pl.BlockSpec(head_block(a_rows), lambda i, j: (i // a_period, 0, 0)),
            pl.BlockSpec(head_block(tm), lambda i, j: (i, 0, 0)),
            pl.BlockSpec(head_block(tm), lambda i, j: (i, 0, 0)),
        ],
        out_shape=[
            jax.ShapeDtypeStruct((t, 4 * SEG_COLS), BF16),
            jax.ShapeDtypeStruct(head_block(n_a), F32),
            jax.ShapeDtypeStruct(head_block(n_a), F32),
            jax.ShapeDtypeStruct(head_block(t), F32),
            jax.ShapeDtypeStruct(head_block(t), F32),
        ],
        compiler_params=pltpu.CompilerParams(
            dimension_semantics=("parallel", "arbitrary"),
            vmem_limit_bytes=_vmem_limit(blocks, 0, 6 * MIB)),
        name="in_proj_kv",
    )(n, w_kv)


def _cumsum_kernel(x_ref, o_ref):
    rows, length = x_ref.shape
    r_i = lax.broadcasted_iota(jnp.int32, (V7X_LANES, V7X_LANES), 0)
    c_i = lax.broadcasted_iota(jnp.int32, (V7X_LANES, V7X_LANES), 1)
    tri = (r_i <= c_i).astype(BF16)
    carry = jnp.zeros((rows, 1), F32)
    for c in range(length // V7X_LANES):
        x = x_ref[:, c * V7X_LANES:(c + 1) * V7X_LANES]
        hi = x.astype(BF16)
        r1 = x - hi.astype(F32)
        mid = r1.astype(BF16)
        lo = (r1 - mid.astype(F32)).astype(BF16)
        blk = (jnp.dot(hi, tri, preferred_element_type=F32)
               + jnp.dot(mid, tri, preferred_element_type=F32)
               + jnp.dot(lo, tri, preferred_element_type=F32)) + carry
        o_ref[:, c * V7X_LANES:(c + 1) * V7X_LANES] = blk
        carry = blk[:, V7X_LANES - 1:V7X_LANES]


def _cumsum_lanes(x):
    return pl.pallas_call(
        _cumsum_kernel,
        out_shape=jax.ShapeDtypeStruct(x.shape, F32),
        name="logf_cumsum",
    )(x)


def _rel_bias_kernel(g_ref, raw_ref, band_ref):
    row = lax.broadcasted_iota(jnp.int32, (BAND_TQ, BAND_W), 0)
    col = lax.broadcasted_iota(jnp.int32, (BAND_TQ, BAND_W), 1)
    q_chunk = row // CHUNK
    k_chunk = col // CHUNK - N_PAST_CHUNKS
    in_band = jnp.logical_and(k_chunk >= q_chunk - N_PAST_CHUNKS, k_chunk <= q_chunk)
    for h in range(N_HEADS):
        rows = jnp.broadcast_to(g_ref[h:h + 1, :], (BAND_TQ, BAND_G))
        toeplitz = pltpu.roll(rows, 0, 1, stride=1, stride_axis=0)[:, :BAND_W] * LOG2E
        raw_ref[h] = toeplitz
        band_ref[h] = jnp.where(in_band, toeplitz, NEG)


def _rel_bias_tiles(table):
    far_past = table[:, 2 * REL_CLIP:]
    far_future = table[:, :1]
    near = jnp.flip(table[:, :2 * REL_CLIP], axis=1)
    n_past = A_REACH - REL_CLIP + 1
    n_future = BAND_W - n_past - near.shape[1]
    g = jnp.concatenate([jnp.broadcast_to(far_past, (N_HEADS, n_past)), near,
                         jnp.broadcast_to(far_future, (N_HEADS, n_future)),
                         jnp.broadcast_to(far_past, (N_HEADS, BAND_G - BAND_W))], axis=1)
    shape = jax.ShapeDtypeStruct((N_HEADS, BAND_TQ, BAND_W), F32)
    return pl.pallas_call(_rel_bias_kernel, out_shape=[shape, shape], name="rel_bias")(g.astype(F32))


def _band_prompt_kernel(q_ref, k_ref, v_ref, bias_ref, o_ref, vp_ref):
    s_len = q_ref.shape[0]
    _widen_values(v_ref, vp_ref)
    n_tiles = s_len // BAND_TQ

    def logits(t):
        q0 = t * BAND_TQ
        k0 = max(q0 - A_REACH, 0)
        n_keys = q0 + BAND_TQ - k0
        s = lax.dot_general(q_ref[q0:q0 + BAND_TQ, :], k_ref[k0:k0 + n_keys, :], _NT,
                            preferred_element_type=F32)
        return [(s * SCALE2 + bias_ref[0, :, BAND_W - n_keys:], vp_ref[k0:k0 + n_keys, :])]

    def store(t, o):
        o_ref[t * BAND_TQ:(t + 1) * BAND_TQ, :] = o.astype(BF16)

    _attention_tiles(n_tiles, logits, store)


def _band_prompt(q, kv, bias, batch, s_len):
    t = batch * s_len
    blocks = 4 * s_len * HEAD_DIM * 2 + BAND_TQ * BAND_W * 4
    return pl.pallas_call(
        _band_prompt_kernel,
        grid=(batch, N_HEADS),
        in_specs=[
            pl.BlockSpec((s_len, HEAD_DIM), lambda b, h: (b, h)),
            pl.BlockSpec((s_len, HEAD_DIM), lambda b, h: (b, h)),
            pl.BlockSpec((s_len, HEAD_DIM), lambda b, h: (b, N_HEADS + h)),
            pl.BlockSpec((1, BAND_TQ, BAND_W), lambda b, h: (h, 0, 0)),
        ],
        out_specs=pl.BlockSpec((s_len, HEAD_DIM), lambda b, h: (b, h)),
        out_shape=jax.ShapeDtypeStruct((t, W_MIX), BF16),
        scratch_shapes=[pltpu.VMEM((s_len, PV_COLS), BF16)],
        compiler_params=pltpu.CompilerParams(
            dimension_semantics=("parallel", "parallel"),
            vmem_limit_bytes=_vmem_limit(blocks, s_len * PV_COLS * 2, 8 * MIB)),
        name="band_prompt",
    )(q, kv, kv, bias)


def _band_sample_kernel(q_ref, kn_ref, vn_ref, kn32_ref, vn32_ref, ck_ref, cv_ref, bias_ref,
                        o_ref, sk_ref, sv_ref):
    n_cache = ck_ref.shape[1]
    t_new = q_ref.shape[0]
    k_heads = _head_major(ck_ref[0])
    v_heads = _head_major(cv_ref[0])
    for h in range(N_HEADS):
        hs = slice(h * HEAD_DIM, (h + 1) * HEAD_DIM)
        q = q_ref[:, hs]
        kc = k_heads[h].astype(BF16)
        vc = v_heads[h].astype(BF16)
        s1 = (lax.dot_general(q, kc, _NT, preferred_element_type=F32) * SCALE2
              + bias_ref[h, :t_new, :n_cache])
        s2 = (lax.dot_general(q, kn_ref[:, hs], _NT, preferred_element_type=F32) * SCALE2
              + bias_ref[h, :t_new, n_cache:n_cache + t_new])
        o_ref[:, hs] = _softmax_pv([(s1, vc), (s2, vn_ref[:, hs])]).astype(BF16)
    for cache_ref, new_ref, state_ref in ((ck_ref, kn32_ref, sk_ref), (cv_ref, vn32_ref, sv_ref)):
        state_ref[0, :n_cache - t_new] = cache_ref[0, t_new:]
        state_ref[0, n_cache - t_new:] = new_ref[...]


def _band_sample(q, kv, ka32, va32, cache_k, cache_v, bias, batch, t_new):
    n_cache = cache_k.shape[1]
    cache_block = (1, n_cache, N_HEADS, HEAD_DIM)
    new_block = (t_new, N_HEADS, HEAD_DIM)
    blocks = 4 * t_new * W_MIX * 2 + 2 * t_new * W_MIX * 4 + 4 * n_cache * W_MIX * 4 + bias.size * 4
    return pl.pallas_call(
        _band_sample_kernel,
        grid=(batch,),
        in_specs=[
            pl.BlockSpec((t_new, W_MIX), lambda b: (b, 0)),
            pl.BlockSpec((t_new, W_MIX), lambda b: (b, 0)),
            pl.BlockSpec((t_new, W_MIX), lambda b: (b, 1)),
            pl.BlockSpec(new_block, lambda b: (b, 0, 0)),
            pl.BlockSpec(new_block, lambda b: (b, 0, 0)),
            pl.BlockSpec(cache_block, lambda b: (b, 0, 0, 0)),
            pl.BlockSpec(cache_block, lambda b: (b, 0, 0, 0)),
            pl.BlockSpec(bias.shape, lambda b: (0, 0, 0)),
        ],
        out_specs=[
            pl.BlockSpec((t_new, W_MIX), lambda b: (b, 0)),
            pl.BlockSpec(cache_block, lambda b: (b, 0, 0, 0)),
            pl.BlockSpec(cache_block, lambda b: (b, 0, 0, 0)),
        ],
        out_shape=[
            jax.ShapeDtypeStruct((batch * t_new, W_MIX), BF16),
            jax.ShapeDtypeStruct(cache_k.shape, F32),
            jax.ShapeDtypeStruct(cache_v.shape, F32),
        ],
        compiler_params=pltpu.CompilerParams(
            dimension_semantics=("parallel",),
            vmem_limit_bytes=_vmem_limit(blocks, 0, 4 * MIB)),
        name="band_sample",
    )(q, kv, kv, ka32, va32, cache_k, cache_v, bias)


def _fox_prompt_kernel(q_ref, k_ref, v_ref, f_ref, o_ref, vp_ref):
    s_len = q_ref.shape[0]
    _widen_values(v_ref, vp_ref)
    row = lax.broadcasted_iota(jnp.int32, (FOX_T, FOX_T), 0)
    col = lax.broadcasted_iota(jnp.int32, (FOX_T, FOX_T), 1)
    causal = row >= col
    f2 = f_ref[0] * LOG2E
    n_tiles = s_len // FOX_T

    def logits(qi):
        q0 = qi * FOX_T
        q = q_ref[q0:q0 + FOX_T, :]
        s_diag = (lax.dot_general(q, k_ref[q0:q0 + FOX_T, :], _NT, preferred_element_type=F32) * SCALE2
                  - f2[:, q0:q0 + FOX_T])
        parts = [(jnp.where(causal, s_diag, NEG), vp_ref[q0:q0 + FOX_T, :])]
        if qi:
            s_past = (lax.dot_general(q, k_ref[0:q0, :], _NT, preferred_element_type=F32) * SCALE2
                      - f2[:, 0:q0])
            parts.append((s_past, vp_ref[0:q0, :]))
        return parts

    def store(qi, o):
        o_ref[qi * FOX_T:(qi + 1) * FOX_T, :] = o.astype(BF16)

    _attention_tiles(n_tiles, logits, store)


def _fox_prompt(q, kv, f_rows, batch, s_len):
    t = batch * s_len
    blocks = 4 * s_len * HEAD_DIM * 2 + 8 * s_len * 4
    return pl.pallas_call(
        _fox_prompt_kernel,
        grid=(batch, N_HEADS),
        in_specs=[
            pl.BlockSpec((s_len, HEAD_DIM), lambda b, h: (b, N_HEADS + h)),
            pl.BlockSpec((s_len, HEAD_DIM), lambda b, h: (b, 2 * N_HEADS + h)),
            pl.BlockSpec((s_len, HEAD_DIM), lambda b, h: (b, 3 * N_HEADS + h)),
            pl.BlockSpec((1, 1, s_len), lambda b, h: (b * N_HEADS + h, 0, 0)),
        ],
        out_specs=pl.BlockSpec((s_len, HEAD_DIM), lambda b, h: (b, h)),
        out_shape=jax.ShapeDtypeStruct((t, W_MIX), BF16),
        scratch_shapes=[pltpu.VMEM((s_len, PV_COLS), BF16)],
        compiler_params=pltpu.CompilerParams(
            dimension_semantics=("parallel", "parallel"),
            vmem_limit_bytes=_vmem_limit(blocks, s_len * PV_COLS * 2, 6 * FOX_T * s_len * 4)),
        name="fox_prompt",
    )(q, kv, kv, f_rows)


def _fox_sample_kernel(q_ref, kn_ref, vn_ref, ck_ref, cv_ref, f_ref, o_ref):
    n_cache = ck_ref.shape[1]
    t_new = q_ref.shape[0]
    row = lax.broadcasted_iota(jnp.int32, (t_new, t_new), 0)
    col = lax.broadcasted_iota(jnp.int32, (t_new, t_new), 1)
    causal = row >= col
    k_heads = _head_major(ck_ref[0])
    v_heads = _head_major(cv_ref[0])
    for h in range(N_HEADS):
        hs = slice(h * HEAD_DIM, (h + 1) * HEAD_DIM)
        q = q_ref[:, hs]
        f2 = f_ref[0, h:h + 1, :] * LOG2E
        kc = k_heads[h].astype(BF16)
        vc = v_heads[h].astype(BF16)
        s1 = lax.dot_general(q, kc, _NT, preferred_element_type=F32) * SCALE2 - f2[:, :n_cache]
        s2 = (lax.dot_general(q, kn_ref[:, hs], _NT, preferred_element_type=F32) * SCALE2
              - f2[:, n_cache:n_cache + t_new])
        parts = [(s1, vc), (jnp.where(causal, s2, NEG), vn_ref[:, hs])]
        o_ref[:, hs] = _softmax_pv(parts).astype(BF16)


def _fox_sample(q, kv, cache_k, cache_v, f_rows, batch, t_new):
    n_cache = cache_k.shape[1]
    f_len = f_rows.shape[-1]
    cache_block = (1, n_cache, N_HEADS, HEAD_DIM)
    blocks = 4 * t_new * W_MIX * 2 + 2 * n_cache * W_MIX * 4 + N_HEADS * f_len * 4
    return pl.pallas_call(
        _fox_sample_kernel,
        grid=(batch,),
        in_specs=[
            pl.BlockSpec((t_new, W_MIX), lambda b: (b, 1)),
            pl.BlockSpec((t_new, W_MIX), lambda b: (b, 2)),
            pl.BlockSpec((t_new, W_MIX), lambda b: (b, 3)),
            pl.BlockSpec(cache_block, lambda b: (b, 0, 0, 0)),
            pl.BlockSpec(cache_block, lambda b: (b, 0, 0, 0)),
            pl.BlockSpec((1, N_HEADS, f_len), lambda b: (b, 0, 0)),
        ],
        out_specs=pl.BlockSpec((t_new, W_MIX), lambda b: (b, 0)),
        out_shape=jax.ShapeDtypeStruct((batch * t_new, W_MIX), BF16),
        compiler_params=pltpu.CompilerParams(
            dimension_semantics=("parallel",),
            vmem_limit_bytes=_vmem_limit(blocks, 0, 8 * MIB)),
        name="fox_sample",
    )(q, kv, kv, cache_k, cache_v, f_rows)


def _merge_kernel(oa_ref, ob_ref, gate_ref, x_ref, wa_ref, wb_ref, wo_ref, h_ref):
    ya = jnp.dot(oa_ref[...], wa_ref[...], preferred_element_type=F32)
    yb = jnp.dot(ob_ref[...], wb_ref[...], preferred_element_type=F32)
    m = (gate_ref[:, :D_MODEL].astype(F32) * ya + gate_ref[:, D_MODEL:].astype(F32) * yb)
    h_ref[...] = x_ref[...] + jnp.dot(m.astype(BF16), wo_ref[...], preferred_element_type=F32)


def _resident(shape):
    return pl.BlockSpec(shape, lambda *_: (0,) * len(shape), pipeline_mode=pl.Buffered(1))


def _merge(oa, ob, gates, x, w_a, w_b, w_o, *, tm):
    t = x.shape[0]
    blocks = 2 * tm * W_MIX * 2 + tm * 2 * D_MODEL * 2 + 2 * tm * D_MODEL * 4
    weights = (2 * W_MIX * D_MODEL + D_MODEL * D_MODEL) * 2
    return pl.pallas_call(
        _merge_kernel,
        grid=(t // tm,),
        in_specs=[
            pl.BlockSpec((tm, W_MIX), lambda i: (i, 0)),
            pl.BlockSpec((tm, W_MIX), lambda i: (i, 0)),
            pl.BlockSpec((tm, 2 * D_MODEL), lambda i: (i, 0)),
            pl.BlockSpec((tm, D_MODEL), lambda i: (i, 0)),
            _resident((W_MIX, D_MODEL)),
            _resident((W_MIX, D_MODEL)),
            _resident((D_MODEL, D_MODEL)),
        ],
        out_specs=pl.BlockSpec((tm, D_MODEL), lambda i: (i, 0)),
        out_shape=jax.ShapeDtypeStruct((t, D_MODEL), F32),
        compiler_params=pltpu.CompilerParams(
            dimension_semantics=("parallel",),
            vmem_limit_bytes=_vmem_limit(blocks, weights, 4 * tm * D_MODEL * 4)),
        name="merge",
    )(oa, ob, gates, x, w_a, w_b, w_o)


def _ffn_kernel(h_ref, g_ref, wu_ref, wd_ref, o_ref, n_ref):
    @pl.when(pl.program_id(1) == 0)
    def _():
        _norm_to_bf16(h_ref, g_ref, n_ref)
        o_ref[...] = h_ref[...]

    a = jnp.dot(n_ref[...], wu_ref[...], preferred_element_type=F32)
    r = jnp.square(jnp.maximum(a, 0.0)).astype(BF16)
    o_ref[...] += jnp.dot(r, wd_ref[...], preferred_element_type=F32)


def _ffn(h, g, w_up, w_down, *, tm, tf):
    t = h.shape[0]
    blocks = 2 * tm * D_MODEL * 4 + 2 * D_MODEL * tf * 2
    scratch = tm * D_MODEL * 2
    return pl.pallas_call(
        _ffn_kernel,
        grid=(t // tm, D_FF // tf),
        in_specs=[
            pl.BlockSpec((tm, D_MODEL), lambda i, f: (i, 0)),
            pl.BlockSpec((1, D_MODEL), lambda i, f: (0, 0)),
            pl.BlockSpec((D_MODEL, tf), lambda i, f: (0, f)),
            pl.BlockSpec((tf, D_MODEL), lambda i, f: (f, 0)),
        ],
        out_specs=pl.BlockSpec((tm, D_MODEL), lambda i, f: (i, 0)),
        out_shape=jax.ShapeDtypeStruct((t, D_MODEL), F32),
        scratch_shapes=[pltpu.VMEM((tm, D_MODEL), BF16)],
        compiler_params=pltpu.CompilerParams(
            dimension_semantics=("parallel", "arbitrary"),
            vmem_limit_bytes=_vmem_limit(blocks, scratch, 2 * tm * tf * 4)),
        name="ffn",
    )(h, g, w_up, w_down)


def _ple_kernel(h_ref, p_ref, gp_ref, gf_ref, wg_ref, wp_ref, y_ref):
    h = h_ref[...]
    n = _rmsnorm_rows(h, gp_ref[...]).astype(BF16)
    gate = jax.nn.sigmoid(jnp.dot(n, wg_ref[...], preferred_element_type=F32))
    proj = jnp.dot(p_ref[...].astype(BF16), wp_ref[...], preferred_element_type=F32)
    y_ref[...] = _rmsnorm_rows(h + proj * gate, gf_ref[...])


def _ple_final(h, p, g_ple, g_final, w_gate, w_proj, *, tm):
    t = h.shape[0]
    blocks = 2 * tm * D_MODEL * 4 + tm * D_PLE * 4
    weights = (D_MODEL * D_MODEL + D_PLE * D_MODEL) * 2
    return pl.pallas_call(
        _ple_kernel,
        grid=(t // tm,),
        in_specs=[
            pl.BlockSpec((tm, D_MODEL), lambda i: (i, 0)),
            pl.BlockSpec((tm, D_PLE), lambda i: (i, 0)),
            pl.BlockSpec((1, D_MODEL), lambda i: (0, 0)),
            pl.BlockSpec((1, D_MODEL), lambda i: (0, 0)),
            _resident((D_MODEL, D_MODEL)),
            _resident((D_PLE, D_MODEL)),
        ],
        out_specs=pl.BlockSpec((tm, D_MODEL), lambda i: (i, 0)),
        out_shape=jax.ShapeDtypeStruct((t, D_MODEL), F32),
        compiler_params=pltpu.CompilerParams(
            dimension_semantics=("parallel",),
            vmem_limit_bytes=_vmem_limit(blocks, weights, 4 * tm * D_MODEL * 4)),
        name="ple_final",
    )(h, p, g_ple, g_final, w_gate, w_proj)


def _pad_lanes(x, multiple):
    pad = (-x.shape[-1]) % multiple
    return jnp.pad(x, ((0, 0),) * (x.ndim - 1) + ((0, pad),)) if pad else x


def _token_tile(t, preferred):
    return preferred if t % preferred == 0 else t


def kernel(x_prompt, x_sample, p_prompt, p_sample, cache_a_k, cache_a_v, cache_b_k, cache_b_v,
           cache_b_logf, g_mix, w_in, b_f, rel_bias, w_a_proj, w_b_proj, w_o, g_mlp, w_up, w_down,
           g_ple, w_ple_gate, w_ple_proj, g_final):
    depth = w_in.shape[0]
    assert depth == 1, "single-layer step"
    batch, s_len, _ = x_prompt.shape
    dec_batch, t_new, _ = x_sample.shape
    n_cache_a = cache_a_k.shape[2]
    assert s_len % FOX_T == 0 and s_len >= A_REACH and t_new == CHUNK and n_cache_a == A_REACH

    w = w_in[0]
    seg = lambda k: w[:, k * SEG_COLS:(k + 1) * SEG_COLS]
    n_qkv = 6 * SEG_COLS
    w_qg = jnp.concatenate([seg(0), seg(3), w[:, n_qkv + N_HEADS:]], axis=1).astype(BF16)
    w_kv = jnp.concatenate([seg(1), seg(2), seg(4), seg(5)], axis=1).astype(BF16)
    w_f = _pad_lanes(w[:, n_qkv:n_qkv + N_HEADS], V7X_LANES).astype(BF16)
    b_f_row = _pad_lanes(b_f[0][None, :].astype(F32), V7X_LANES)
    g_mix_row = g_mix[0][None, :].astype(F32)
    g_mlp_row = g_mlp[0][None, :].astype(F32)
    g_ple_row = g_ple[0][None, :].astype(F32)
    g_final_row = g_final[None, :].astype(F32)
    w_a = w_a_proj[0].astype(BF16)
    w_b = w_b_proj[0].astype(BF16)
    w_o_b = w_o[0].astype(BF16)
    w_up_b = w_up[0].astype(BF16)
    w_down_b = w_down[0].astype(BF16)
    w_pg = w_ple_gate[0].astype(BF16)
    w_pp = w_ple_proj[0].astype(BF16)
    bias_raw, bias_band = _rel_bias_tiles(rel_bias[0])

    def project(x2d, *, a_rows, a_period, tm):
        q, gates, lf, n = _in_proj_q(x2d, g_mix_row, w_qg, w_f, b_f_row, tm=tm)
        return (q, gates, lf) + tuple(_in_proj_kv(n, w_kv, a_rows=a_rows, a_period=a_period, tm=tm))

    def finish(x2d, p2d, oa, ob, gates):
        t = x2d.shape[0]
        h1 = _merge(oa, ob, gates, x2d, w_a, w_b, w_o_b, tm=_token_tile(t, 256))
        h2 = _ffn(h1, g_mlp_row, w_up_b, w_down_b, tm=_token_tile(t, 1024), tf=512)
        return _ple_final(h2, p2d, g_ple_row, g_final_row, w_pg, w_pp, tm=_token_tile(t, 512))

    tp = batch * s_len
    xp = x_prompt.reshape(tp, D_MODEL)
    tm_p = _token_tile(s_len, 1024)
    q_p, gates_p, lf_p, kv_p, ka_p, va_p, kb_p, vb_p = project(
        xp, a_rows=min(A_REACH, tm_p), a_period=s_len // tm_p, tm=tm_p)
    lf_rows = lf_p.reshape(batch, s_len, N_HEADS).transpose(0, 2, 1).reshape(batch * N_HEADS, s_len)
    f_p = _cumsum_lanes(lf_rows).reshape(batch * N_HEADS, 1, s_len)
    oa_p = _band_prompt(q_p, kv_p, bias_band, batch, s_len)
    ob_p = _fox_prompt(q_p, kv_p, f_p, batch, s_len)
    y_prompt = finish(xp, p_prompt[0].reshape(tp, D_PLE), oa_p, ob_p, gates_p)

    ts = dec_batch * t_new
    xs = x_sample.reshape(ts, D_MODEL)
    q_s, gates_s, lf_s, kv_s, ka_s, va_s, kb_s, vb_s = project(xs, a_rows=ts, a_period=1, tm=ts)
    lf_new = lf_s.reshape(dec_batch, t_new, N_HEADS)
    lf_all = jnp.concatenate([cache_b_logf[0].astype(F32), lf_new], axis=1)
    lf_all = _pad_lanes(lf_all.transpose(0, 2, 1), V7X_LANES)
    f_s = _cumsum_lanes(lf_all.reshape(dec_batch * N_HEADS, -1)).reshape(dec_batch, N_HEADS, -1)
    oa_s, sk_s, sv_s = _band_sample(q_s, kv_s, ka_s, va_s, cache_a_k[0], cache_a_v[0], bias_raw,
                                    dec_batch, t_new)
    ob_s = _fox_sample(q_s, kv_s, cache_b_k[0], cache_b_v[0], f_s, dec_batch, t_new)
    y_sample = finish(xs, p_sample[0].reshape(ts, D_PLE), oa_s, ob_s, gates_s)

    def state(a, b):
        return a.reshape(1, b, -1, N_HEADS, HEAD_DIM)

    return (y_prompt.reshape(batch, s_len, D_MODEL),
            y_sample.reshape(dec_batch, t_new, D_MODEL),
            state(ka_p, batch), state(va_p, batch), state(kb_p, batch), state(vb_p, batch),
            lf_p.reshape(1, batch, s_len, N_HEADS),
            sk_s[None], sv_s[None],
            state(kb_s, dec_batch), state(vb_s, dec_batch),
            lf_new[None])
```

```python
import functools
import math

import jax
import jax.numpy as jnp
from jax import lax
from jax.experimental import pallas as pl
from jax.experimental.pallas import tpu as pltpu

F32 = jnp.float32
BF16 = jnp.bfloat16

D_MODEL = 2048
CHUNK = 64
N_PAST_CHUNKS = 8
A_REACH = N_PAST_CHUNKS * CHUNK
HEAD_DIM = 128
N_HEADS = 8
W_MIX = N_HEADS * HEAD_DIM
REL_CLIP = 128
D_FF = 4 * D_MODEL
D_PLE = 256
RMS_EPS = 1e-6
SCALE = HEAD_DIM ** -0.5
NEG = -1e30
LOG2E = math.log2(math.e)
SCALE2 = SCALE * LOG2E

V7X_LANES = 128
V7X_VMEM_LIMIT_CAP = 56 * 1024 * 1024
MIB = 1024 * 1024

NORM_ROWS = 128
BAND_TQ = 256
BAND_W = A_REACH + BAND_TQ
BAND_G = BAND_W + BAND_TQ
PV_COLS = 2 * HEAD_DIM
FOX_T = 256
SEG_COLS = 1024

_NT = (((1,), (1,)), ((), ()))


def _vmem_limit(block_bytes, scratch_bytes, temp_bytes):
    est = 2 * block_bytes + scratch_bytes + temp_bytes
    return int(min(max(est, 16 * MIB), V7X_VMEM_LIMIT_CAP))


def _sigmoid(x):
    return 0.5 * jnp.tanh(0.5 * x) + 0.5


def _rmsnorm_rows(x, g):
    ms = jnp.mean(x * x, axis=-1, keepdims=True)
    return (x * lax.rsqrt(ms + RMS_EPS)) * g


def _norm_to_bf16(x_ref, g_ref, n_ref):
    rows = min(NORM_ROWS, x_ref.shape[0])

    def body(c, carry):
        r = pl.ds(pl.multiple_of(c * rows, rows), rows)
        n_ref[r, :] = _rmsnorm_rows(x_ref[r, :], g_ref[...]).astype(BF16)
        return carry

    lax.fori_loop(0, x_ref.shape[0] // rows, body, 0)


def _softmax_pv(parts):
    m = functools.reduce(jnp.maximum, [jnp.max(s, axis=-1, keepdims=True) for s, _ in parts])
    ps = [jnp.exp2(s - m) for s, _ in parts]
    l = functools.reduce(jnp.add, [jnp.sum(p, axis=-1, keepdims=True) for p in ps])
    o = functools.reduce(jnp.add, [jnp.dot(p.astype(BF16), v, preferred_element_type=F32)
                                   for p, (_, v) in zip(ps, parts)])
    return o / l


def _widen_values(v_ref, vp_ref):
    lane = lax.broadcasted_iota(jnp.int32, v_ref.shape, 1)
    vp_ref[:, :HEAD_DIM] = v_ref[...]
    vp_ref[:, HEAD_DIM:] = jnp.where(lane == 0, 1.0, 0.0).astype(BF16)


def _softmax_weights(parts):
    m = functools.reduce(jnp.maximum, [jnp.max(s, axis=-1, keepdims=True) for s, _ in parts])
    return [(jnp.exp2(s - m).astype(BF16), v) for s, v in parts]


def _weighted_values_wide(weighted):
    o = functools.reduce(jnp.add, [jnp.dot(p, v, preferred_element_type=F32) for p, v in weighted])
    return o[:, :HEAD_DIM] / o[:, HEAD_DIM:HEAD_DIM + 1]


def _attention_tiles(n_tiles, logits, store):
    stage1 = {0: logits(0)}
    if n_tiles > 1:
        stage1[1] = logits(1)
    stage2 = {0: _softmax_weights(stage1.pop(0))}
    for t in range(n_tiles):
        if t + 2 < n_tiles:
            stage1[t + 2] = logits(t + 2)
        if t + 1 < n_tiles:
            stage2[t + 1] = _softmax_weights(stage1.pop(t + 1))
        store(t, _weighted_values_wide(stage2.pop(t)))


def _store_heads(ref, rows):
    stacked = jnp.stack([rows[:, h * HEAD_DIM:(h + 1) * HEAD_DIM] for h in range(N_HEADS)], axis=0)
    ref[...] = jnp.swapaxes(stacked, 0, 1)


def _head_major(rows):
    return jnp.swapaxes(rows, 0, 1)


def _inproj_q_kernel(x_ref, g_ref, w_ref, wf_ref, bf_ref, q_ref, gate_ref, lf_ref, n_ref):
    j = pl.program_id(1)

    @pl.when(j == 0)
    def _():
        _norm_to_bf16(x_ref, g_ref, n_ref)
        z = jnp.dot(n_ref[...], wf_ref[...], preferred_element_type=F32) + bf_ref[...]
        lf = jnp.minimum(z, 0.0) - jnp.log1p(jnp.exp(-jnp.abs(z)))
        lf_ref[...] = lf[:, :N_HEADS]

    @pl.when(j < 2)
    def _():
        q_ref[...] = jnp.dot(n_ref[...], w_ref[...], preferred_element_type=F32).astype(BF16)

    @pl.when(j >= 2)
    def _():
        a = jnp.dot(n_ref[...], w_ref[...], preferred_element_type=F32)
        gate_ref[...] = _sigmoid(a).astype(BF16)


def _in_proj_q(x, g, w_qg, w_f, b_f, *, tm):
    t = x.shape[0]
    tn = SEG_COLS
    blocks = (tm * D_MODEL * 4 + D_MODEL * tn * 2 + D_MODEL * V7X_LANES * 2
              + 2 * tm * tn * 2 + tm * V7X_LANES * 4 + tm * D_MODEL * 2)
    return pl.pallas_call(
        _inproj_q_kernel,
        grid=(t // tm, 6),
        in_specs=[
            pl.BlockSpec((tm, D_MODEL), lambda i, j: (i, 0)),
            pl.BlockSpec((1, D_MODEL), lambda i, j: (0, 0)),
            pl.BlockSpec((D_MODEL, tn), lambda i, j: (0, j)),
            pl.BlockSpec((D_MODEL, V7X_LANES), lambda i, j: (0, 0)),
            pl.BlockSpec((1, V7X_LANES), lambda i, j: (0, 0)),
        ],
        out_specs=[
            pl.BlockSpec((tm, tn), lambda i, j: (i, jnp.minimum(j, 1))),
            pl.BlockSpec((tm, tn), lambda i, j: (i, jnp.clip(j - 2, 0, 3))),
            pl.BlockSpec((tm, N_HEADS), lambda i, j: (i, 0)),
            pl.BlockSpec((tm, D_MODEL), lambda i, j: (i, 0)),
        ],
        out_shape=[
            jax.ShapeDtypeStruct((t, 2 * SEG_COLS), BF16),
            jax.ShapeDtypeStruct((t, 4 * SEG_COLS), BF16),
            jax.ShapeDtypeStruct((t, N_HEADS), F32),
            jax.ShapeDtypeStruct((t, D_MODEL), BF16),
        ],
        compiler_params=pltpu.CompilerParams(
            dimension_semantics=("parallel", "arbitrary"),
            vmem_limit_bytes=_vmem_limit(blocks, 0, 6 * MIB)),
        name="in_proj_q",
    )(x, g, w_qg, w_f, b_f)


def _inproj_kv_kernel(n_ref, w_ref, kv_ref, ka_ref, va_ref, kb_ref, vb_ref, *, a_rows, a_period):
    i = pl.program_id(0)
    j = pl.program_id(1)
    tm = n_ref.shape[0]
    is_tail = (i % a_period) == (a_period - 1)

    def project():
        return jnp.dot(n_ref[...], w_ref[...], preferred_element_type=F32)

    @pl.when(jnp.logical_and(j < 2, jnp.logical_not(is_tail)))
    def _():
        kv_ref[...] = project().astype(BF16)

    for col, tail_ref in ((0, ka_ref), (1, va_ref)):
        @pl.when(jnp.logical_and(j == col, is_tail))
        def _(tail_ref=tail_ref):
            r = project()
            kv_ref[...] = r.astype(BF16)
            _store_heads(tail_ref, r[tm - a_rows:, :])

    for col, full_ref in ((2, kb_ref), (3, vb_ref)):
        @pl.when(j == col)
        def _(full_ref=full_ref):
            r = project()
            kv_ref[...] = r.astype(BF16)
            _store_heads(full_ref, r)


def _in_proj_kv(n, w_kv, *, a_rows, a_period, tm):
    t = n.shape[0]
    tn = SEG_COLS
    n_a = (t // tm) // a_period * a_rows
    blocks = (tm * D_MODEL * 2 + D_MODEL * tn * 2 + tm * tn * 2
              + 2 * a_rows * tn * 4 + 2 * tm * tn * 4)
    kern = functools.partial(_inproj_kv_kernel, a_rows=a_rows, a_period=a_period)
    head_block = lambda rows: (rows, N_HEADS, HEAD_DIM)
    return pl.pallas_call(
        kern,
        grid=(t // tm, 4),
        in_specs=[
            pl.BlockSpec((tm, D_MODEL), lambda i, j: (i, 0)),
            pl.BlockSpec((D_MODEL, tn), lambda i, j: (0, j)),
        ],
        out_specs=[
            pl.BlockSpec((tm, tn), lambda i, j: (i, j)),
            pl.BlockSpec(head_block(a_rows), lambda i, j: (i // a_period, 0, 0)),
            pl.BlockSpec(head_block(a_rows), lambda i, j: (i // a_period, 0, 0)),
            pl.BlockSpec(head_block(tm), lambda i, j: (i, 0, 0)),
            pl.BlockSpec(head_block(tm), lambda i, j: (i, 0, 0)),
        ],
        out_shape=[
            jax.ShapeDtypeStruct((t, 4 * SEG_COLS), BF16),
            jax.ShapeDtypeStruct(head_block(n_a), F32),
            jax.ShapeDtypeStruct(head_block(n_a), F32),
            jax.ShapeDtypeStruct(head_block(t), F32),
            jax.ShapeDtypeStruct(head_block(t), F32),
        ],
        compiler_params=pltpu.CompilerParams(
            dimension_semantics=("parallel", "arbitrary"),
            vmem_limit_bytes=_vmem_limit(blocks, 0, 6 * MIB)),
        name="in_proj_kv",
    )(n, w_kv)


def _cumsum_kernel(x_ref, o_ref):
    rows, length = x_ref.shape
    r_i = lax.broadcasted_iota(jnp.int32, (V7X_LANES, V7X_LANES), 0)
    c_i = lax.broadcasted_iota(jnp.int32, (V7X_LANES, V7X_LANES), 1)
    tri = (r_i <= c_i).astype(BF16)
    carry = jnp.zeros((rows, 1), F32)
    for c in range(length // V7X_LANES):
        x = x_ref[:, c * V7X_LANES:(c + 1) * V7X_LANES]
        hi = x.astype(BF16)
        r1 = x - hi.astype(F32)
        mid = r1.astype(BF16)
        lo = (r1 - mid.astype(F32)).astype(BF16)
        blk = (jnp.dot(hi, tri, preferred_element_type=F32)
               + jnp.dot(mid, tri, preferred_element_type=F32)
               + jnp.dot(lo, tri, preferred_element_type=F32)) + carry
        o_ref[:, c * V7X_LANES:(c + 1) * V7X_LANES] = blk
        carry = blk[:, V7X_LANES - 1:V7X_LANES]


def _cumsum_lanes(x):
    return pl.pallas_call(
        _cumsum_kernel,
        out_shape=jax.ShapeDtypeStruct(x.shape, F32),
        name="logf_cumsum",
    )(x)


def _rel_bias_kernel(g_ref, raw_ref, band_ref):
    row = lax.broadcasted_iota(jnp.int32, (BAND_TQ, BAND_W), 0)
    col = lax.broadcasted_iota(jnp.int32, (BAND_TQ, BAND_W), 1)
    q_chunk = row // CHUNK
    k_chunk = col // CHUNK - N_PAST_CHUNKS
    in_band = jnp.logical_and(k_chunk >= q_chunk - N_PAST_CHUNKS, k_chunk <= q_chunk)
    for h in range(N_HEADS):
        rows = jnp.broadcast_to(g_ref[h:h + 1, :], (BAND_TQ, BAND_G))
        toeplitz = pltpu.roll(rows, 0, 1, stride=1, stride_axis=0)[:, :BAND_W] * LOG2E
        raw_ref[h] = toeplitz
        band_ref[h] = jnp.where(in_band, toeplitz, NEG)


def _rel_bias_tiles(table):
    far_past = table[:, 2 * REL_CLIP:]
    far_future = table[:, :1]
    near = jnp.flip(table[:, :2 * REL_CLIP], axis=1)
    n_past = A_REACH - REL_CLIP + 1
    n_future = BAND_W - n_past - near.shape[1]
    g = jnp.concatenate([jnp.broadcast_to(far_past, (N_HEADS, n_past)), near,
                         jnp.broadcast_to(far_future, (N_HEADS, n_future)),
                         jnp.broadcast_to(far_past, (N_HEADS, BAND_G - BAND_W))], axis=1)
    shape = jax.ShapeDtypeStruct((N_HEADS, BAND_TQ, BAND_W), F32)
    return pl.pallas_call(_rel_bias_kernel, out_shape=[shape, shape], name="rel_bias")(g.astype(F32))


def _band_prompt_kernel(q_ref, k_ref, v_ref, bias_ref, o_ref, vp_ref):
    s_len = q_ref.shape[0]
    _widen_values(v_ref, vp_ref)
    n_tiles = s_len // BAND_TQ

    def logits(t):
        q0 = t * BAND_TQ
        k0 = max(q0 - A_REACH, 0)
        n_keys = q0 + BAND_TQ - k0
        s = lax.dot_general(q_ref[q0:q0 + BAND_TQ, :], k_ref[k0:k0 + n_keys, :], _NT,
                            preferred_element_type=F32)
        return [(s * SCALE2 + bias_ref[0, :, BAND_W - n_keys:], vp_ref[k0:k0 + n_keys, :])]

    def store(t, o):
        o_ref[t * BAND_TQ:(t + 1) * BAND_TQ, :] = o.astype(BF16)

    _attention_tiles(n_tiles, logits, store)


def _band_prompt(q, kv, bias, batch, s_len):
    t = batch * s_len
    blocks = 4 * s_len * HEAD_DIM * 2 + BAND_TQ * BAND_W * 4
    return pl.pallas_call(
        _band_prompt_kernel,
        grid=(batch, N_HEADS),
        in_specs=[
            pl.BlockSpec((s_len, HEAD_DIM), lambda b, h: (b, h)),
            pl.BlockSpec((s_len, HEAD_DIM), lambda b, h: (b, h)),
            pl.BlockSpec((s_len, HEAD_DIM), lambda b, h: (b, N_HEADS + h)),
            pl.BlockSpec((1, BAND_TQ, BAND_W), lambda b, h: (h, 0, 0)),
        ],
        out_specs=pl.BlockSpec((s_len, HEAD_DIM), lambda b, h: (b, h)),
        out_shape=jax.ShapeDtypeStruct((t, W_MIX), BF16),
        scratch_shapes=[pltpu.VMEM((s_len, PV_COLS), BF16)],
        compiler_params=pltpu.CompilerParams(
            dimension_semantics=("parallel", "parallel"),
            vmem_limit_bytes=_vmem_limit(blocks, s_len * PV_COLS * 2, 8 * MIB)),
        name="band_prompt",
    )(q, kv, kv, bias)


def _band_sample_kernel(q_ref, kn_ref, vn_ref, kn32_ref, vn32_ref, ck_ref, cv_ref, bias_ref,
                        o_ref, sk_ref, sv_ref):
    n_cache = ck_ref.shape[1]
    t_new = q_ref.shape[0]
    k_heads = _head_major(ck_ref[0])
    v_heads = _head_major(cv_ref[0])
    for h in range(N_HEADS):
        hs = slice(h * HEAD_DIM, (h + 1) * HEAD_DIM)
        q = q_ref[:, hs]
        kc = k_heads[h].astype(BF16)
        vc = v_heads[h].astype(BF16)
        s1 = (lax.dot_general(q, kc, _NT, preferred_element_type=F32) * SCALE2
              + bias_ref[h, :t_new, :n_cache])
        s2 = (lax.dot_general(q, kn_ref[:, hs], _NT, preferred_element_type=F32) * SCALE2
              + bias_ref[h, :t_new, n_cache:n_cache + t_new])
        o_ref[:, hs] = _softmax_pv([(s1, vc), (s2, vn_ref[:, hs])]).astype(BF16)
    for cache_ref, new_ref, state_ref in ((ck_ref, kn32_ref, sk_ref), (cv_ref, vn32_ref, sv_ref)):
        state_ref[0, :n_cache - t_new] = cache_ref[0, t_new:]
        state_ref[0, n_cache - t_new:] = new_ref[...]


def _band_sample(q, kv, ka32, va32, cache_k, cache_v, bias, batch, t_new):
    n_cache = cache_k.shape[1]
    cache_block = (1, n_cache, N_HEADS, HEAD_DIM)
    new_block = (t_new, N_HEADS, HEAD_DIM)
    blocks = 4 * t_new * W_MIX * 2 + 2 * t_new * W_MIX * 4 + 4 * n_cache * W_MIX * 4 + bias.size * 4
    return pl.pallas_call(
        _band_sample_kernel,
        grid=(batch,),
        in_specs=[
            pl.BlockSpec((t_new, W_MIX), lambda b: (b, 0)),
            pl.BlockSpec((t_new, W_MIX), lambda b: (b, 0)),
            pl.BlockSpec((t_new, W_MIX), lambda b: (b, 1)),
            pl.BlockSpec(new_block, lambda b: (b, 0, 0)),
            pl.BlockSpec(new_block, lambda b: (b, 0, 0)),
            pl.BlockSpec(cache_block, lambda b: (b, 0, 0, 0)),
            pl.BlockSpec(cache_block, lambda b: (b, 0, 0, 0)),
            pl.BlockSpec(bias.shape, lambda b: (0, 0, 0)),
        ],
        out_specs=[
            pl.BlockSpec((t_new, W_MIX), lambda b: (b, 0)),
            pl.BlockSpec(cache_block, lambda b: (b, 0, 0, 0)),
            pl.BlockSpec(cache_block, lambda b: (b, 0, 0, 0)),
        ],
        out_shape=[
            jax.ShapeDtypeStruct((batch * t_new, W_MIX), BF16),
            jax.ShapeDtypeStruct(cache_k.shape, F32),
            jax.ShapeDtypeStruct(cache_v.shape, F32),
        ],
        compiler_params=pltpu.CompilerParams(
            dimension_semantics=("parallel",),
            vmem_limit_bytes=_vmem_limit(blocks, 0, 4 * MIB)),
        name="band_sample",
    )(q, kv, kv, ka32, va32, cache_k, cache_v, bias)


def _fox_prompt_kernel(q_ref, k_ref, v_ref, f_ref, o_ref, vp_ref):
    s_len = q_ref.shape[0]
    _widen_values(v_ref, vp_ref)
    row = lax.broadcasted_iota(jnp.int32, (FOX_T, FOX_T), 0)
    col = lax.broadcasted_iota(jnp.int32, (FOX_T, FOX_T), 1)
    causal = row >= col
    f2 = f_ref[0] * LOG2E
    n_tiles = s_len // FOX_T

    def logits(qi):
        q0 = qi * FOX_T
        q = q_ref[q0:q0 + FOX_T, :]
        s_diag = (lax.dot_general(q, k_ref[q0:q0 + FOX_T, :], _NT, preferred_element_type=F32) * SCALE2
                  - f2[:, q0:q0 + FOX_T])
        parts = [(jnp.where(causal, s_diag, NEG), vp_ref[q0:q0 + FOX_T, :])]
        if qi:
            s_past = (lax.dot_general(q, k_ref[0:q0, :], _NT, preferred_element_type=F32) * SCALE2
                      - f2[:, 0:q0])
            parts.append((s_past, vp_ref[0:q0, :]))
        return parts

    def store(qi, o):
        o_ref[qi * FOX_T:(qi + 1) * FOX_T, :] = o.astype(BF16)

    _attention_tiles(n_tiles, logits, store)


def _fox_prompt(q, kv, f_rows, batch, s_len):
    t = batch * s_len
    blocks = 4 * s_len * HEAD_DIM * 2 + 8 * s_len * 4
    return pl.pallas_call(
        _fox_prompt_kernel,
        grid=(batch, N_HEADS),
        in_specs=[
            pl.BlockSpec((s_len, HEAD_DIM), lambda b, h: (b, N_HEADS + h)),
            pl.BlockSpec((s_len, HEAD_DIM), lambda b, h: (b, 2 * N_HEADS + h)),
            pl.BlockSpec((s_len, HEAD_DIM), lambda b, h: (b, 3 * N_HEADS + h)),
            pl.BlockSpec((1, 1, s_len), lambda b, h: (b * N_HEADS + h, 0, 0)),
        ],
        out_specs=pl.BlockSpec((s_len, HEAD_DIM), lambda b, h: (b, h)),
        out_shape=jax.ShapeDtypeStruct((t, W_MIX), BF16),
        scratch_shapes=[pltpu.VMEM((s_len, PV_COLS), BF16)],
        compiler_params=pltpu.CompilerParams(
            dimension_semantics=("parallel", "parallel"),
            vmem_limit_bytes=_vmem_limit(blocks, s_len * PV_COLS * 2, 6 * FOX_T * s_len * 4)),
        name="fox_prompt",
    )(q, kv, kv, f_rows)


def _fox_sample_kernel(q_ref, kn_ref, vn_ref, ck_ref, cv_ref, f_ref, o_ref):
    n_cache = ck_ref.shape[1]
    t_new = q_ref.shape[0]
    row = lax.broadcasted_iota(jnp.int32, (t_new, t_new), 0)
    col = lax.broadcasted_iota(jnp.int32, (t_new, t_new), 1)
    causal = row >= col
    k_heads = _head_major(ck_ref[0])
    v_heads = _head_major(cv_ref[0])
    for h in range(N_HEADS):
        hs = slice(h * HEAD_DIM, (h + 1) * HEAD_DIM)
        q = q_ref[:, hs]
        f2 = f_ref[0, h:h + 1, :] * LOG2E
        kc = k_heads[h].astype(BF16)
        vc = v_heads[h].astype(BF16)
        s1 = lax.dot_general(q, kc, _NT, preferred_element_type=F32) * SCALE2 - f2[:, :n_cache]
        s2 = (lax.dot_general(q, kn_ref[:, hs], _NT, preferred_element_type=F32) * SCALE2
              - f2[:, n_cache:n_cache + t_new])
        parts = [(s1, vc), (jnp.where(causal, s2, NEG), vn_ref[:, hs])]
        o_ref[:, hs] = _softmax_pv(parts).astype(BF16)


def _fox_sample(q, kv, cache_k, cache_v, f_rows, batch, t_new):
    n_cache = cache_k.shape[1]
    f_len = f_rows.shape[-1]
    cache_block = (1, n_cache, N_HEADS, HEAD_DIM)
    blocks = 4 * t_new * W_MIX * 2 + 2 * n_cache * W_MIX * 4 + N_HEADS * f_len * 4
    return pl.pallas_call(
        _fox_sample_kernel,
        grid=(batch,),
        in_specs=[
            pl.BlockSpec((t_new, W_MIX), lambda b: (b, 1)),
            pl.BlockSpec((t_new, W_MIX), lambda b: (b, 2)),
            pl.BlockSpec((t_new, W_MIX), lambda b: (b, 3)),
            pl.BlockSpec(cache_block, lambda b: (b, 0, 0, 0)),
            pl.BlockSpec(cache_block, lambda b: (b, 0, 0, 0)),
            pl.BlockSpec((1, N_HEADS, f_len), lambda b: (b, 0, 0)),
        ],
        out_specs=pl.BlockSpec((t_new, W_MIX), lambda b: (b, 0)),
        out_shape=jax.ShapeDtypeStruct((batch * t_new, W_MIX), BF16),
        compiler_params=pltpu.CompilerParams(
            dimension_semantics=("parallel",),
            vmem_limit_bytes=_vmem_limit(blocks, 0, 8 * MIB)),
        name="fox_sample",
    )(q, kv, kv, cache_k, cache_v, f_rows)


def _merge_kernel(oa_ref, ob_ref, gate_ref, x_ref, wa_ref, wb_ref, wo_ref, h_ref):
    ya = jnp.dot(oa_ref[...], wa_ref[...], preferred_element_type=F32)
    yb = jnp.dot(ob_ref[...], wb_ref[...], preferred_element_type=F32)
    m = (gate_ref[:, :D_MODEL].astype(F32) * ya + gate_ref[:, D_MODEL:].astype(F32) * yb)
    h_ref[...] = x_ref[...] + jnp.dot(m.astype(BF16), wo_ref[...], preferred_element_type=F32)


def _resident(shape):
    return pl.BlockSpec(shape, lambda *_: (0,) * len(shape), pipeline_mode=pl.Buffered(1))


def _merge(oa, ob, gates, x, w_a, w_b, w_o, *, tm):
    t = x.shape[0]
    blocks = 2 * tm * W_MIX * 2 + tm * 2 * D_MODEL * 2 + 2 * tm * D_MODEL * 4
    weights = (2 * W_MIX * D_MODEL + D_MODEL * D_MODEL) * 2
    return pl.pallas_call(
        _merge_kernel,
        grid=(t // tm,),
        in_specs=[
            pl.BlockSpec((tm, W_MIX), lambda i: (i, 0)),
            pl.BlockSpec((tm, W_MIX), lambda i: (i, 0)),
            pl.BlockSpec((tm, 2 * D_MODEL), lambda i: (i, 0)),
            pl.BlockSpec((tm, D_MODEL), lambda i: (i, 0)),
            _resident((W_MIX, D_MODEL)),
            _resident((W_MIX, D_MODEL)),
            _resident((D_MODEL, D_MODEL)),
        ],
        out_specs=pl.BlockSpec((tm, D_MODEL), lambda i: (i, 0)),
        out_shape=jax.ShapeDtypeStruct((t, D_MODEL), F32),
        compiler_params=pltpu.CompilerParams(
            dimension_semantics=("parallel",),
            vmem_limit_bytes=_vmem_limit(blocks, weights, 4 * tm * D_MODEL * 4)),
        name="merge",
    )(oa, ob, gates, x, w_a, w_b, w_o)


def _ffn_kernel(h_ref, g_ref, wu_ref, wd_ref, o_ref, n_ref):
    def hidden():
        a = jnp.dot(n_ref[...], wu_ref[...], preferred_element_type=F32)
        r = jnp.square(jnp.maximum(a, 0.0)).astype(BF16)
        return jnp.dot(r, wd_ref[...], preferred_element_type=F32)

    @pl.when(pl.program_id(1) == 0)
    def _():
        _norm_to_bf16(h_ref, g_ref, n_ref)
        o_ref[...] = h_ref[...] + hidden()

    @pl.when(pl.program_id(1) != 0)
    def _():
        o_ref[...] += hidden()


def _ffn(h, g, w_up, w_down, *, tm, tf):
    t = h.shape[0]
    blocks = 2 * tm * D_MODEL * 4 + 2 * D_MODEL * tf * 2
    scratch = tm * D_MODEL * 2
    return pl.pallas_call(
        _ffn_kernel,
        grid=(t // tm, D_FF // tf),
        in_specs=[
            pl.BlockSpec((tm, D_MODEL), lambda i, f: (i, 0)),
            pl.BlockSpec((1, D_MODEL), lambda i, f: (0, 0)),
            pl.BlockSpec((D_MODEL, tf), lambda i, f: (0, f)),
            pl.BlockSpec((tf, D_MODEL), lambda i, f: (f, 0)),
        ],
        out_specs=pl.BlockSpec((tm, D_MODEL), lambda i, f: (i, 0)),
        out_shape=jax.ShapeDtypeStruct((t, D_MODEL), F32),
        scratch_shapes=[pltpu.VMEM((tm, D_MODEL), BF16)],
        compiler_params=pltpu.CompilerParams(
            dimension_semantics=("parallel", "arbitrary"),
            vmem_limit_bytes=_vmem_limit(blocks, scratch, 2 * tm * tf * 4)),
        name="ffn",
    )(h, g, w_up, w_down)


def _ple_kernel(h_ref, p_ref, gp_ref, gf_ref, wg_ref, wp_ref, y_ref):
    h = h_ref[...]
    n = _rmsnorm_rows(h, gp_ref[...]).astype(BF16)
    gate = _sigmoid(jnp.dot(n, wg_ref[...], preferred_element_type=F32))
    proj = jnp.dot(p_ref[...].astype(BF16), wp_ref[...], preferred_element_type=F32)
    y_ref[...] = _rmsnorm_rows(h + proj * gate, gf_ref[...])


def _ple_final(h, p, g_ple, g_final, w_gate, w_proj, *, tm):
    t = h.shape[0]
    blocks = 2 * tm * D_MODEL * 4 + tm * D_PLE * 4
    weights = (D_MODEL * D_MODEL + D_PLE * D_MODEL) * 2
    return pl.pallas_call(
        _ple_kernel,
        grid=(t // tm,),
        in_specs=[
            pl.BlockSpec((tm, D_MODEL), lambda i: (i, 0)),
            pl.BlockSpec((tm, D_PLE), lambda i: (i, 0)),
            pl.BlockSpec((1, D_MODEL), lambda i: (0, 0)),
            pl.BlockSpec((1, D_MODEL), lambda i: (0, 0)),
            _resident((D_MODEL, D_MODEL)),
            _resident((D_PLE, D_MODEL)),
        ],
        out_specs=pl.BlockSpec((tm, D_MODEL), lambda i: (i, 0)),
        out_shape=jax.ShapeDtypeStruct((t, D_MODEL), F32),
        compiler_params=pltpu.CompilerParams(
            dimension_semantics=("parallel",),
            vmem_limit_bytes=_vmem_limit(blocks, weights, 4 * tm * D_MODEL * 4)),
        name="ple_final",
    )(h, p, g_ple, g_final, w_gate, w_proj)


def _pad_lanes(x, multiple):
    pad = (-x.shape[-1]) % multiple
    return jnp.pad(x, ((0, 0),) * (x.ndim - 1) + ((0, pad),)) if pad else x


def _token_tile(t, preferred):
    return preferred if t % preferred == 0 else t


def kernel(x_prompt, x_sample, p_prompt, p_sample, cache_a_k, cache_a_v, cache_b_k, cache_b_v,
           cache_b_logf, g_mix, w_in, b_f, rel_bias, w_a_proj, w_b_proj, w_o, g_mlp, w_up, w_down,
           g_ple, w_ple_gate, w_ple_proj, g_final):
    depth = w_in.shape[0]
    assert depth == 1, "single-layer step"
    batch, s_len, _ = x_prompt.shape
    dec_batch, t_new, _ = x_sample.shape
    n_cache_a = cache_a_k.shape[2]
    assert s_len % FOX_T == 0 and s_len >= A_REACH and t_new == CHUNK and n_cache_a == A_REACH

    w = w_in[0]
    seg = lambda k: w[:, k * SEG_COLS:(k + 1) * SEG_COLS]
    n_qkv = 6 * SEG_COLS
    w_qg = jnp.concatenate([seg(0), seg(3), w[:, n_qkv + N_HEADS:]], axis=1).astype(BF16)
    w_kv = jnp.concatenate([seg(1), seg(2), seg(4), seg(5)], axis=1).astype(BF16)
    w_f = _pad_lanes(w[:, n_qkv:n_qkv + N_HEADS], V7X_LANES).astype(BF16)
    b_f_row = _pad_lanes(b_f[0][None, :].astype(F32), V7X_LANES)
    g_mix_row = g_mix[0][None, :].astype(F32)
    g_mlp_row = g_mlp[0][None, :].astype(F32)
    g_ple_row = g_ple[0][None, :].astype(F32)
    g_final_row = g_final[None, :].astype(F32)
    w_a = w_a_proj[0].astype(BF16)
    w_b = w_b_proj[0].astype(BF16)
    w_o_b = w_o[0].astype(BF16)
    w_up_b = w_up[0].astype(BF16)
    w_down_b = w_down[0].astype(BF16)
    w_pg = w_ple_gate[0].astype(BF16)
    w_pp = w_ple_proj[0].astype(BF16)
    bias_raw, bias_band = _rel_bias_tiles(rel_bias[0])

    def project(x2d, *, a_rows, a_period, tm):
        q, gates, lf, n = _in_proj_q(x2d, g_mix_row, w_qg, w_f, b_f_row, tm=tm)
        return (q, gates, lf) + tuple(_in_proj_kv(n, w_kv, a_rows=a_rows, a_period=a_period, tm=tm))

    def finish(x2d, p2d, oa, ob, gates):
        t = x2d.shape[0]
        h1 = _merge(oa, ob, gates, x2d, w_a, w_b, w_o_b, tm=_token_tile(t, 256))
        h2 = _ffn(h1, g_mlp_row, w_up_b, w_down_b, tm=_token_tile(t, 1024), tf=512)
        return _ple_final(h2, p2d, g_ple_row, g_final_row, w_pg, w_pp, tm=_token_tile(t, 512))

    tp = batch * s_len
    xp = x_prompt.reshape(tp, D_MODEL)
    tm_p = _token_tile(s_len, 1024)
    q_p, gates_p, lf_p, kv_p, ka_p, va_p, kb_p, vb_p = project(
        xp, a_rows=min(A_REACH, tm_p), a_period=s_len // tm_p, tm=tm_p)
    lf_rows = lf_p.reshape(batch, s_len, N_HEADS).transpose(0, 2, 1).reshape(batch * N_HEADS, s_len)
    f_p = _cumsum_lanes(lf_rows).reshape(batch * N_HEADS, 1, s_len)
    oa_p = _band_prompt(q_p, kv_p, bias_band, batch, s_len)
    ob_p = _fox_prompt(q_p, kv_p, f_p, batch, s_len)
    y_prompt = finish(xp, p_prompt[0].reshape(tp, D_PLE), oa_p, ob_p, gates_p)

    ts = dec_batch * t_new
    xs = x_sample.reshape(ts, D_MODEL)
    q_s, gates_s, lf_s, kv_s, ka_s, va_s, kb_s, vb_s = project(xs, a_rows=ts, a_period=1, tm=ts)
    lf_new = lf_s.reshape(dec_batch, t_new, N_HEADS)
    lf_all = jnp.concatenate([cache_b_logf[0].astype(F32), lf_new], axis=1)
    lf_all = _pad_lanes(lf_all.transpose(0, 2, 1), V7X_LANES)
    f_s = _cumsum_lanes(lf_all.reshape(dec_batch * N_HEADS, -1)).reshape(dec_batch, N_HEADS, -1)
    oa_s, sk_s, sv_s = _band_sample(q_s, kv_s, ka_s, va_s, cache_a_k[0], cache_a_v[0], bias_raw,
                                    dec_batch, t_new)
    ob_s = _fox_sample(q_s, kv_s, cache_b_k[0], cache_b_v[0], f_s, dec_batch, t_new)
    y_sample = finish(xs, p_sample[0].reshape(ts, D_PLE), oa_s, ob_s, gates_s)

    def state(a, b):
        return a.reshape(1, b, -1, N_HEADS, HEAD_DIM)

    return (y_prompt.reshape(batch, s_len, D_MODEL),
            y_sample.reshape(dec_batch, t_new, D_MODEL),
            state(ka_p, batch), state(va_p, batch), state(kb_p, batch), state(vb_p, batch),
            lf_p.reshape(1, batch, s_len, N_HEADS),
            sk_s[None], sv_s[None],
            state(kb_s, dec_batch), state(vb_s, dec_batch),
            lf_new[None])
```

```python
import functools
import math

import jax
import jax.numpy as jnp
from jax import lax
from jax.experimental import pallas as pl
from jax.experimental.pallas import tpu as pltpu

F32 = jnp.float32
BF16 = jnp.bfloat16

D_MODEL = 2048
CHUNK = 64
N_PAST_CHUNKS = 8
A_REACH = N_PAST_CHUNKS * CHUNK
HEAD_DIM = 128
N_HEADS = 8
W_MIX = N_HEADS * HEAD_DIM
REL_CLIP = 128
D_FF = 4 * D_MODEL
D_PLE = 256
RMS_EPS = 1e-6
SCALE = HEAD_DIM ** -0.5
NEG = -1e30
LOG2E = math.log2(math.e)
SCALE2 = SCALE * LOG2E

V7X_LANES = 128
V7X_VMEM_LIMIT_CAP = 56 * 1024 * 1024
MIB = 1024 * 1024

NORM_ROWS = 128
BAND_TQ = 256
BAND_W = A_REACH + BAND_TQ
BAND_G = BAND_W + BAND_TQ
PV_COLS = 2 * HEAD_DIM
FOX_T = 256
SEG_COLS = 1024

_NT = (((1,), (1,)), ((), ()))


def _vmem_limit(block_bytes, scratch_bytes, temp_bytes):
    est = 2 * block_bytes + scratch_bytes + temp_bytes
    return int(min(max(est, 16 * MIB), V7X_VMEM_LIMIT_CAP))


def _sigmoid(x):
    return 0.5 * jnp.tanh(0.5 * x) + 0.5


def _rmsnorm_rows(x, g):
    ms = jnp.mean(x * x, axis=-1, keepdims=True)
    return (x * lax.rsqrt(ms + RMS_EPS)) * g


def _norm_to_bf16(x_ref, g_ref, n_ref):
    rows = min(NORM_ROWS, x_ref.shape[0])

    def body(c, carry):
        r = pl.ds(pl.multiple_of(c * rows, rows), rows)
        n_ref[r, :] = _rmsnorm_rows(x_ref[r, :], g_ref[...]).astype(BF16)
        return carry

    lax.fori_loop(0, x_ref.shape[0] // rows, body, 0)


def _softmax_pv(parts):
    m = functools.reduce(jnp.maximum, [jnp.max(s, axis=-1, keepdims=True) for s, _ in parts])
    ps = [jnp.exp2(s - m) for s, _ in parts]
    l = functools.reduce(jnp.add, [jnp.sum(p, axis=-1, keepdims=True) for p in ps])
    o = functools.reduce(jnp.add, [jnp.dot(p.astype(BF16), v, preferred_element_type=F32)
                                   for p, (_, v) in zip(ps, parts)])
    return o / l


def _widen_values(v_ref, vp_ref):
    lane = lax.broadcasted_iota(jnp.int32, v_ref.shape, 1)
    vp_ref[:, :HEAD_DIM] = v_ref[...]
    vp_ref[:, HEAD_DIM:] = jnp.where(lane == 0, 1.0, 0.0).astype(BF16)


def _softmax_weights(parts):
    m = functools.reduce(jnp.maximum, [jnp.max(s, axis=-1, keepdims=True) for s, _ in parts])
    return [(jnp.exp2(s - m).astype(BF16), v) for s, v in parts]


def _weighted_values_wide(weighted):
    o = functools.reduce(jnp.add, [jnp.dot(p, v, preferred_element_type=F32) for p, v in weighted])
    return o[:, :HEAD_DIM] / o[:, HEAD_DIM:HEAD_DIM + 1]


def _attention_tiles(n_tiles, logits, store):
    stage1 = {0: logits(0)}
    if n_tiles > 1:
        stage1[1] = logits(1)
    stage2 = {0: _softmax_weights(stage1.pop(0))}
    for t in range(n_tiles):
        if t + 2 < n_tiles:
            stage1[t + 2] = logits(t + 2)
        if t + 1 < n_tiles:
            stage2[t + 1] = _softmax_weights(stage1.pop(t + 1))
        store(t, _weighted_values_wide(stage2.pop(t)))


def _store_heads(ref, rows):
    stacked = jnp.stack([rows[:, h * HEAD_DIM:(h + 1) * HEAD_DIM] for h in range(N_HEADS)], axis=0)
    ref[...] = jnp.swapaxes(stacked, 0, 1)


def _head_major(rows):
    return jnp.swapaxes(rows, 0, 1)


def _inproj_g_kernel(x_ref, g_ref, w_ref, wf_ref, bf_ref, gate_ref, lf_ref, n_ref):
    @pl.when(pl.program_id(1) == 0)
    def _():
        _norm_to_bf16(x_ref, g_ref, n_ref)
        z = jnp.dot(n_ref[...], wf_ref[...], preferred_element_type=F32) + bf_ref[...]
        lf = jnp.minimum(z, 0.0) - jnp.log1p(jnp.exp(-jnp.abs(z)))
        lf_ref[...] = lf[:, :N_HEADS]

    a = lax.dot_general(n_ref[...], w_ref[...], _NT, preferred_element_type=F32)
    gate_ref[...] = _sigmoid(a).astype(BF16)


def _in_proj_g(x, g, w_g, w_f, b_f, *, tm):
    t = x.shape[0]
    tn = SEG_COLS
    blocks = (tm * D_MODEL * 4 + D_MODEL * tn * 2 + D_MODEL * V7X_LANES * 2
              + tm * tn * 2 + tm * V7X_LANES * 4 + tm * D_MODEL * 2)
    return pl.pallas_call(
        _inproj_g_kernel,
        grid=(t // tm, 4),
        in_specs=[
            pl.BlockSpec((tm, D_MODEL), lambda i, j: (i, 0)),
            pl.BlockSpec((1, D_MODEL), lambda i, j: (0, 0)),
            pl.BlockSpec((tn, D_MODEL), lambda i, j: (j, 0)),
            pl.BlockSpec((D_MODEL, V7X_LANES), lambda i, j: (0, 0)),
            pl.BlockSpec((1, V7X_LANES), lambda i, j: (0, 0)),
        ],
        out_specs=[
            pl.BlockSpec((tm, tn), lambda i, j: (i, j)),
            pl.BlockSpec((tm, N_HEADS), lambda i, j: (i, 0)),
            pl.BlockSpec((tm, D_MODEL), lambda i, j: (i, 0)),
        ],
        out_shape=[
            jax.ShapeDtypeStruct((t, 4 * SEG_COLS), BF16),
            jax.ShapeDtypeStruct((t, N_HEADS), F32),
            jax.ShapeDtypeStruct((t, D_MODEL), BF16),
        ],
        compiler_params=pltpu.CompilerParams(
            dimension_semantics=("parallel", "arbitrary"),
            vmem_limit_bytes=_vmem_limit(blocks, 0, 6 * MIB)),
        name="in_proj_g",
    )(x, g, w_g, w_f, b_f)


def _inproj_qkv_kernel(n_ref, w_ref, qkv_ref, ka_ref, va_ref, kb_ref, vb_ref, *, a_rows, a_period):
    i = pl.program_id(0)
    j = pl.program_id(1)
    tm = n_ref.shape[0]
    is_tail = (i % a_period) == (a_period - 1)
    is_a_kv = jnp.logical_or(j == 1, j == 2)

    def project():
        return lax.dot_general(n_ref[...], w_ref[...], _NT, preferred_element_type=F32)

    plain = jnp.logical_or(jnp.logical_or(j == 0, j == 3),
                           jnp.logical_and(is_a_kv, jnp.logical_not(is_tail)))

    @pl.when(plain)
    def _():
        qkv_ref[...] = project().astype(BF16)

    for col, tail_ref in ((1, ka_ref), (2, va_ref)):
        @pl.when(jnp.logical_and(j == col, is_tail))
        def _(tail_ref=tail_ref):
            r = project()
            qkv_ref[...] = r.astype(BF16)
            _store_heads(tail_ref, r[tm - a_rows:, :])

    for col, full_ref in ((4, kb_ref), (5, vb_ref)):
        @pl.when(j == col)
        def _(full_ref=full_ref):
            r = project()
            qkv_ref[...] = r.astype(BF16)
            _store_heads(full_ref, r)


def _in_proj_qkv(n, w_t, *, a_rows, a_period, tm):
    t = n.shape[0]
    tn = SEG_COLS
    n_a = (t // tm) // a_period * a_rows
    blocks = (tm * D_MODEL * 2 + D_MODEL * tn * 2 + tm * tn * 2
              + 2 * a_rows * tn * 4 + 2 * tm * tn * 4)
    kern = functools.partial(_inproj_qkv_kernel, a_rows=a_rows, a_period=a_period)
    head_block = lambda rows: (rows, N_HEADS, HEAD_DIM)
    return pl.pallas_call(
        kern,
        grid=(t // tm, 6),
        in_specs=[
            pl.BlockSpec((tm, D_MODEL), lambda i, j: (i, 0)),
            pl.BlockSpec((tn, D_MODEL), lambda i, j: (j, 0)),
        ],
        out_specs=[
            pl.BlockSpec((tm, tn), lambda i, j: (i, j)),
            pl.BlockSpec(head_block(a_rows), lambda i, j: (i // a_period, 0, 0)),
            pl.BlockSpec(head_block(a_rows), lambda i, j: (i // a_period, 0, 0)),
            pl.BlockSpec(head_block(tm), lambda i, j: (i, 0, 0)),
            pl.BlockSpec(head_block(tm), lambda i, j: (i, 0, 0)),
        ],
        out_shape=[
            jax.ShapeDtypeStruct((t, 6 * SEG_COLS), BF16),
            jax.ShapeDtypeStruct(head_block(n_a), F32),
            jax.ShapeDtypeStruct(head_block(n_a), F32),
            jax.ShapeDtypeStruct(head_block(t), F32),
            jax.ShapeDtypeStruct(head_block(t), F32),
        ],
        compiler_params=pltpu.CompilerParams(
            dimension_semantics=("parallel", "arbitrary"),
            vmem_limit_bytes=_vmem_limit(blocks, 0, 6 * MIB)),
        name="in_proj_qkv",
    )(n, w_t)


def _cumsum_kernel(x_ref, o_ref):
    rows, length = x_ref.shape
    r_i = lax.broadcasted_iota(jnp.int32, (V7X_LANES, V7X_LANES), 0)
    c_i = lax.broadcasted_iota(jnp.int32, (V7X_LANES, V7X_LANES), 1)
    tri = (r_i <= c_i).astype(BF16)
    carry = jnp.zeros((rows, 1), F32)
    for c in range(length // V7X_LANES):
        x = x_ref[:, c * V7X_LANES:(c + 1) * V7X_LANES]
        hi = x.astype(BF16)
        r1 = x - hi.astype(F32)
        mid = r1.astype(BF16)
        lo = (r1 - mid.astype(F32)).astype(BF16)
        blk = (jnp.dot(hi, tri, preferred_element_type=F32)
               + jnp.dot(mid, tri, preferred_element_type=F32)
               + jnp.dot(lo, tri, preferred_element_type=F32)) + carry
        o_ref[:, c * V7X_LANES:(c + 1) * V7X_LANES] = blk
        carry = blk[:, V7X_LANES - 1:V7X_LANES]


def _cumsum_lanes(x):
    return pl.pallas_call(
        _cumsum_kernel,
        out_shape=jax.ShapeDtypeStruct(x.shape, F32),
        name="logf_cumsum",
    )(x)


def _rel_bias_kernel(g_ref, raw_ref, band_ref):
    row = lax.broadcasted_iota(jnp.int32, (BAND_TQ, BAND_W), 0)
    col = lax.broadcasted_iota(jnp.int32, (BAND_TQ, BAND_W), 1)
    q_chunk = row // CHUNK
    k_chunk = col // CHUNK - N_PAST_CHUNKS
    in_band = jnp.logical_and(k_chunk >= q_chunk - N_PAST_CHUNKS, k_chunk <= q_chunk)
    for h in range(N_HEADS):
        rows = jnp.broadcast_to(g_ref[h:h + 1, :], (BAND_TQ, BAND_G))
        toeplitz = pltpu.roll(rows, 0, 1, stride=1, stride_axis=0)[:, :BAND_W] * LOG2E
        raw_ref[h] = toeplitz
        band_ref[h] = jnp.where(in_band, toeplitz, NEG)


def _rel_bias_tiles(table):
    far_past = table[:, 2 * REL_CLIP:]
    far_future = table[:, :1]
    near = jnp.flip(table[:, :2 * REL_CLIP], axis=1)
    n_past = A_REACH - REL_CLIP + 1
    n_future = BAND_W - n_past - near.shape[1]
    g = jnp.concatenate([jnp.broadcast_to(far_past, (N_HEADS, n_past)), near,
                         jnp.broadcast_to(far_future, (N_HEADS, n_future)),
                         jnp.broadcast_to(far_past, (N_HEADS, BAND_G - BAND_W))], axis=1)
    shape = jax.ShapeDtypeStruct((N_HEADS, BAND_TQ, BAND_W), F32)
    return pl.pallas_call(_rel_bias_kernel, out_shape=[shape, shape], name="rel_bias")(g.astype(F32))


def _mixers_prompt_kernel(qa_ref, ka_ref, va_ref, qb_ref, kb_ref, vb_ref, bias_ref, f_ref,
                          oa_ref, ob_ref, vpa_ref, vpb_ref):
    s_len = qa_ref.shape[0]
    _widen_values(va_ref, vpa_ref)
    _widen_values(vb_ref, vpb_ref)
    row = lax.broadcasted_iota(jnp.int32, (FOX_T, FOX_T), 0)
    col = lax.broadcasted_iota(jnp.int32, (FOX_T, FOX_T), 1)
    causal = row >= col
    f2 = f_ref[0] * LOG2E

    def band_logits(t):
        q0 = t * BAND_TQ
        k0 = max(q0 - A_REACH, 0)
        n_keys = q0 + BAND_TQ - k0
        s = lax.dot_general(qa_ref[q0:q0 + BAND_TQ, :], ka_ref[k0:k0 + n_keys, :], _NT,
                            preferred_element_type=F32)
        return [(s * SCALE2 + bias_ref[0, :, BAND_W - n_keys:], vpa_ref[k0:k0 + n_keys, :])]

    def fox_logits(qi):
        q0 = qi * FOX_T
        q = qb_ref[q0:q0 + FOX_T, :]
        s_diag = (lax.dot_general(q, kb_ref[q0:q0 + FOX_T, :], _NT, preferred_element_type=F32) * SCALE2
                  - f2[:, q0:q0 + FOX_T])
        parts = [(jnp.where(causal, s_diag, NEG), vpb_ref[q0:q0 + FOX_T, :])]
        if qi:
            s_past = (lax.dot_general(q, kb_ref[0:q0, :], _NT, preferred_element_type=F32) * SCALE2
                      - f2[:, 0:q0])
            parts.append((s_past, vpb_ref[0:q0, :]))
        return parts

    n_band = s_len // BAND_TQ
    n_fox = s_len // FOX_T
    order = []
    for t in range(max(n_band, n_fox)):
        if t < n_band:
            order.append(("band", t))
        if t < n_fox:
            order.append(("fox", t))

    def logits(i):
        kind, t = order[i]
        return band_logits(t) if kind == "band" else fox_logits(t)

    def store(i, o):
        kind, t = order[i]
        if kind == "band":
            oa_ref[t * BAND_TQ:(t + 1) * BAND_TQ, :] = o.astype(BF16)
        else:
            ob_ref[t * FOX_T:(t + 1) * FOX_T, :] = o.astype(BF16)

    _attention_tiles(len(order), logits, store)


def _mixers_prompt(qkv, bias, f_rows, batch, s_len):
    t = batch * s_len
    head = lambda seg: pl.BlockSpec((s_len, HEAD_DIM), lambda b, h: (b, seg * N_HEADS + h))
    blocks = 8 * s_len * HEAD_DIM * 2 + BAND_TQ * BAND_W * 4 + 8 * s_len * 4
    scratch = 2 * s_len * PV_COLS * 2
    out = jax.ShapeDtypeStruct((t, W_MIX), BF16)
    return pl.pallas_call(
        _mixers_prompt_kernel,
        grid=(batch, N_HEADS),
        in_specs=[head(s) for s in range(6)] + [
            pl.BlockSpec((1, BAND_TQ, BAND_W), lambda b, h: (h, 0, 0)),
            pl.BlockSpec((1, 1, s_len), lambda b, h: (b * N_HEADS + h, 0, 0))],
        out_specs=[head(0), head(0)],
        out_shape=[out, out],
        scratch_shapes=[pltpu.VMEM((s_len, PV_COLS), BF16), pltpu.VMEM((s_len, PV_COLS), BF16)],
        compiler_params=pltpu.CompilerParams(
            dimension_semantics=("parallel", "parallel"),
            vmem_limit_bytes=_vmem_limit(blocks, scratch, 8 * FOX_T * s_len * 4)),
        name="mixers_prompt",
    )(qkv, qkv, qkv, qkv, qkv, qkv, bias, f_rows)


def _band_sample_kernel(q_ref, kn_ref, vn_ref, kn32_ref, vn32_ref, ck_ref, cv_ref, bias_ref,
                        o_ref, sk_ref, sv_ref):
    n_cache = ck_ref.shape[1]
    t_new = q_ref.shape[0]
    k_heads = _head_major(ck_ref[0])
    v_heads = _head_major(cv_ref[0])
    for h in range(N_HEADS):
        hs = slice(h * HEAD_DIM, (h + 1) * HEAD_DIM)
        q = q_ref[:, hs]
        kc = k_heads[h].astype(BF16)
        vc = v_heads[h].astype(BF16)
        s1 = (lax.dot_general(q, kc, _NT, preferred_element_type=F32) * SCALE2
              + bias_ref[h, :t_new, :n_cache])
        s2 = (lax.dot_general(q, kn_ref[:, hs], _NT, preferred_element_type=F32) * SCALE2
              + bias_ref[h, :t_new, n_cache:n_cache + t_new])
        o_ref[:, hs] = _softmax_pv([(s1, vc), (s2, vn_ref[:, hs])]).astype(BF16)
    for cache_ref, new_ref, state_ref in ((ck_ref, kn32_ref, sk_ref), (cv_ref, vn32_ref, sv_ref)):
        state_ref[0, :n_cache - t_new] = cache_ref[0, t_new:]
        state_ref[0, n_cache - t_new:] = new_ref[...]


def _band_sample(qkv, ka32, va32, cache_k, cache_v, bias, batch, t_new):
    n_cache = cache_k.shape[1]
    cache_block = (1, n_cache, N_HEADS, HEAD_DIM)
    new_block = (t_new, N_HEADS, HEAD_DIM)
    blocks = 4 * t_new * W_MIX * 2 + 2 * t_new * W_MIX * 4 + 4 * n_cache * W_MIX * 4 + bias.size * 4
    return pl.pallas_call(
        _band_sample_kernel,
        grid=(batch,),
        in_specs=[
            pl.BlockSpec((t_new, W_MIX), lambda b: (b, 0)),
            pl.BlockSpec((t_new, W_MIX), lambda b: (b, 1)),
            pl.BlockSpec((t_new, W_MIX), lambda b: (b, 2)),
            pl.BlockSpec(new_block, lambda b: (b, 0, 0)),
            pl.BlockSpec(new_block, lambda b: (b, 0, 0)),
            pl.BlockSpec(cache_block, lambda b: (b, 0, 0, 0)),
            pl.BlockSpec(cache_block, lambda b: (b, 0, 0, 0)),
            pl.BlockSpec(bias.shape, lambda b: (0, 0, 0)),
        ],
        out_specs=[
            pl.BlockSpec((t_new, W_MIX), lambda b: (b, 0)),
            pl.BlockSpec(cache_block, lambda b: (b, 0, 0, 0)),
            pl.BlockSpec(cache_block, lambda b: (b, 0, 0, 0)),
        ],
        out_shape=[
            jax.ShapeDtypeStruct((batch * t_new, W_MIX), BF16),
            jax.ShapeDtypeStruct(cache_k.shape, F32),
            jax.ShapeDtypeStruct(cache_v.shape, F32),
        ],
        compiler_params=pltpu.CompilerParams(
            dimension_semantics=("parallel",),
            vmem_limit_bytes=_vmem_limit(blocks, 0, 4 * MIB)),
        name="band_sample",
    )(qkv, qkv, qkv, ka32, va32, cache_k, cache_v, bias)


def _fox_sample_kernel(q_ref, kn_ref, vn_ref, ck_ref, cv_ref, f_ref, o_ref):
    n_cache = ck_ref.shape[1]
    t_new = q_ref.shape[0]
    row = lax.broadcasted_iota(jnp.int32, (t_new, t_new), 0)
    col = lax.broadcasted_iota(jnp.int32, (t_new, t_new), 1)
    causal = row >= col
    k_heads = _head_major(ck_ref[0])
    v_heads = _head_major(cv_ref[0])
    for h in range(N_HEADS):
        hs = slice(h * HEAD_DIM, (h + 1) * HEAD_DIM)
        q = q_ref[:, hs]
        f2 = f_ref[0, h:h + 1, :] * LOG2E
        kc = k_heads[h].astype(BF16)
        vc = v_heads[h].astype(BF16)
        s1 = lax.dot_general(q, kc, _NT, preferred_element_type=F32) * SCALE2 - f2[:, :n_cache]
        s2 = (lax.dot_general(q, kn_ref[:, hs], _NT, preferred_element_type=F32) * SCALE2
              - f2[:, n_cache:n_cache + t_new])
        parts = [(s1, vc), (jnp.where(causal, s2, NEG), vn_ref[:, hs])]
        o_ref[:, hs] = _softmax_pv(parts).astype(BF16)


def _fox_sample(qkv, cache_k, cache_v, f_rows, batch, t_new):
    n_cache = cache_k.shape[1]
    f_len = f_rows.shape[-1]
    cache_block = (1, n_cache, N_HEADS, HEAD_DIM)
    blocks = 4 * t_new * W_MIX * 2 + 2 * n_cache * W_MIX * 4 + N_HEADS * f_len * 4
    return pl.pallas_call(
        _fox_sample_kernel,
        grid=(batch,),
        in_specs=[
            pl.BlockSpec((t_new, W_MIX), lambda b: (b, 3)),
            pl.BlockSpec((t_new, W_MIX), lambda b: (b, 4)),
            pl.BlockSpec((t_new, W_MIX), lambda b: (b, 5)),
            pl.BlockSpec(cache_block, lambda b: (b, 0, 0, 0)),
            pl.BlockSpec(cache_block, lambda b: (b, 0, 0, 0)),
            pl.BlockSpec((1, N_HEADS, f_len), lambda b: (b, 0, 0)),
        ],
        out_specs=pl.BlockSpec((t_new, W_MIX), lambda b: (b, 0)),
        out_shape=jax.ShapeDtypeStruct((batch * t_new, W_MIX), BF16),
        compiler_params=pltpu.CompilerParams(
            dimension_semantics=("parallel",),
            vmem_limit_bytes=_vmem_limit(blocks, 0, 8 * MIB)),
        name="fox_sample",
    )(qkv, qkv, qkv, cache_k, cache_v, f_rows)


def _merge_kernel(oa_ref, ob_ref, gate_ref, x_ref, wa_ref, wb_ref, wo_ref, h_ref):
    ya = jnp.dot(oa_ref[...], wa_ref[...], preferred_element_type=F32)
    yb = jnp.dot(ob_ref[...], wb_ref[...], preferred_element_type=F32)
    m = (gate_ref[:, :D_MODEL].astype(F32) * ya + gate_ref[:, D_MODEL:].astype(F32) * yb)
    h_ref[...] = x_ref[...] + jnp.dot(m.astype(BF16), wo_ref[...], preferred_element_type=F32)


def _resident(shape):
    return pl.BlockSpec(shape, lambda *_: (0,) * len(shape), pipeline_mode=pl.Buffered(1))


def _merge(oa, ob, gates, x, w_a, w_b, w_o, *, tm):
    t = x.shape[0]
    blocks = 2 * tm * W_MIX * 2 + tm * 2 * D_MODEL * 2 + 2 * tm * D_MODEL * 4
    weights = (2 * W_MIX * D_MODEL + D_MODEL * D_MODEL) * 2
    return pl.pallas_call(
        _merge_kernel,
        grid=(t // tm,),
        in_specs=[
            pl.BlockSpec((tm, W_MIX), lambda i: (i, 0)),
            pl.BlockSpec((tm, W_MIX), lambda i: (i, 0)),
            pl.BlockSpec((tm, 2 * D_MODEL), lambda i: (i, 0)),
            pl.BlockSpec((tm, D_MODEL), lambda i: (i, 0)),
            _resident((W_MIX, D_MODEL)),
            _resident((W_MIX, D_MODEL)),
            _resident((D_MODEL, D_MODEL)),
        ],
        out_specs=pl.BlockSpec((tm, D_MODEL), lambda i: (i, 0)),
        out_shape=jax.ShapeDtypeStruct((t, D_MODEL), F32),
        compiler_params=pltpu.CompilerParams(
            dimension_semantics=("parallel",),
            vmem_limit_bytes=_vmem_limit(blocks, weights, 4 * tm * D_MODEL * 4)),
        name="merge",
    )(oa, ob, gates, x, w_a, w_b, w_o)


def _ffn_kernel(h_ref, g_ref, wu_ref, wd_ref, o_ref, n_ref):
    def hidden():
        a = jnp.dot(n_ref[...], wu_ref[...], preferred_element_type=F32)
        r = jnp.square(jnp.maximum(a, 0.0)).astype(BF16)
        return jnp.dot(r, wd_ref[...], preferred_element_type=F32)

    @pl.when(pl.program_id(1) == 0)
    def _():
        _norm_to_bf16(h_ref, g_ref, n_ref)
        o_ref[...] = h_ref[...] + hidden()

    @pl.when(pl.program_id(1) != 0)
    def _():
        o_ref[...] += hidden()


def _ffn(h, g, w_up, w_down, *, tm, tf):
    t = h.shape[0]
    blocks = 2 * tm * D_MODEL * 4 + 2 * D_MODEL * tf * 2
    scratch = tm * D_MODEL * 2
    return pl.pallas_call(
        _ffn_kernel,
        grid=(t // tm, D_FF // tf),
        in_specs=[
            pl.BlockSpec((tm, D_MODEL), lambda i, f: (i, 0)),
            pl.BlockSpec((1, D_MODEL), lambda i, f: (0, 0)),
            pl.BlockSpec((D_MODEL, tf), lambda i, f: (0, f)),
            pl.BlockSpec((tf, D_MODEL), lambda i, f: (f, 0)),
        ],
        out_specs=pl.BlockSpec((tm, D_MODEL), lambda i, f: (i, 0)),
        out_shape=jax.ShapeDtypeStruct((t, D_MODEL), F32),
        scratch_shapes=[pltpu.VMEM((tm, D_MODEL), BF16)],
        compiler_params=pltpu.CompilerParams(
            dimension_semantics=("parallel", "arbitrary"),
            vmem_limit_bytes=_vmem_limit(blocks, scratch, 2 * tm * tf * 4)),
        name="ffn",
    )(h, g, w_up, w_down)


def _ple_kernel(h_ref, p_ref, gp_ref, gf_ref, wg_ref, wp_ref, y_ref):
    h = h_ref[...]
    n = _rmsnorm_rows(h, gp_ref[...]).astype(BF16)
    gate = _sigmoid(jnp.dot(n, wg_ref[...], preferred_element_type=F32))
    proj = jnp.dot(p_ref[...].astype(BF16), wp_ref[...], preferred_element_type=F32)
    y_ref[...] = _rmsnorm_rows(h + proj * gate, gf_ref[...])


def _ple_final(h, p, g_ple, g_final, w_gate, w_proj, *, tm):
    t = h.shape[0]
    blocks = 2 * tm * D_MODEL * 4 + tm * D_PLE * 4
    weights = (D_MODEL * D_MODEL + D_PLE * D_MODEL) * 2
    return pl.pallas_call(
        _ple_kernel,
        grid=(t // tm,),
        in_specs=[
            pl.BlockSpec((tm, D_MODEL), lambda i: (i, 0)),
            pl.BlockSpec((tm, D_PLE), lambda i: (i, 0)),
            pl.BlockSpec((1, D_MODEL), lambda i: (0, 0)),
            pl.BlockSpec((1, D_MODEL), lambda i: (0, 0)),
            _resident((D_MODEL, D_MODEL)),
            _resident((D_PLE, D_MODEL)),
        ],
        out_specs=pl.BlockSpec((tm, D_MODEL), lambda i: (i, 0)),
        out_shape=jax.ShapeDtypeStruct((t, D_MODEL), F32),
        compiler_params=pltpu.CompilerParams(
            dimension_semantics=("parallel",),
            vmem_limit_bytes=_vmem_limit(blocks, weights, 4 * tm * D_MODEL * 4)),
        name="ple_final",
    )(h, p, g_ple, g_final, w_gate, w_proj)


def _pad_lanes(x, multiple):
    pad = (-x.shape[-1]) % multiple
    return jnp.pad(x, ((0, 0),) * (x.ndim - 1) + ((0, pad),)) if pad else x


def _token_tile(t, preferred):
    return preferred if t % preferred == 0 else t


def kernel(x_prompt, x_sample, p_prompt, p_sample, cache_a_k, cache_a_v, cache_b_k, cache_b_v,
           cache_b_logf, g_mix, w_in, b_f, rel_bias, w_a_proj, w_b_proj, w_o, g_mlp, w_up, w_down,
           g_ple, w_ple_gate, w_ple_proj, g_final):
    depth = w_in.shape[0]
    assert depth == 1, "single-layer step"
    batch, s_len, _ = x_prompt.shape
    dec_batch, t_new, _ = x_sample.shape
    n_cache_a = cache_a_k.shape[2]
    assert s_len % FOX_T == 0 and s_len >= A_REACH and t_new == CHUNK and n_cache_a == A_REACH

    w = w_in[0]
    n_qkv = 6 * SEG_COLS
    w_t = jnp.swapaxes(w, 0, 1).astype(BF16)
    w_g = w_t[n_qkv + N_HEADS:]
    w_f = _pad_lanes(w[:, n_qkv:n_qkv + N_HEADS], V7X_LANES).astype(BF16)
    b_f_row = _pad_lanes(b_f[0][None, :].astype(F32), V7X_LANES)
    g_mix_row = g_mix[0][None, :].astype(F32)
    g_mlp_row = g_mlp[0][None, :].astype(F32)
    g_ple_row = g_ple[0][None, :].astype(F32)
    g_final_row = g_final[None, :].astype(F32)
    w_a = w_a_proj[0].astype(BF16)
    w_b = w_b_proj[0].astype(BF16)
    w_o_b = w_o[0].astype(BF16)
    w_up_b = w_up[0].astype(BF16)
    w_down_b = w_down[0].astype(BF16)
    w_pg = w_ple_gate[0].astype(BF16)
    w_pp = w_ple_proj[0].astype(BF16)
    bias_raw, bias_band = _rel_bias_tiles(rel_bias[0])

    def project(x2d, *, a_rows, a_period, tm):
        gates, lf, n = _in_proj_g(x2d, g_mix_row, w_g, w_f, b_f_row, tm=tm)
        return (gates, lf) + tuple(_in_proj_qkv(n, w_t, a_rows=a_rows, a_period=a_period, tm=tm))

    def finish(x2d, p2d, oa, ob, gates):
        t = x2d.shape[0]
        h1 = _merge(oa, ob, gates, x2d, w_a, w_b, w_o_b, tm=_token_tile(t, 256))
        h2 = _ffn(h1, g_mlp_row, w_up_b, w_down_b, tm=_token_tile(t, 1024), tf=512)
        return _ple_final(h2, p2d, g_ple_row, g_final_row, w_pg, w_pp, tm=_token_tile(t, 512))

    tp = batch * s_len
    xp = x_prompt.reshape(tp, D_MODEL)
    tm_p = _token_tile(s_len, 1024)
    gates_p, lf_p, qkv_p, ka_p, va_p, kb_p, vb_p = project(
        xp, a_rows=min(A_REACH, tm_p), a_period=s_len // tm_p, tm=tm_p)
    lf_rows = lf_p.reshape(batch, s_len, N_HEADS).transpose(0, 2, 1).reshape(batch * N_HEADS, s_len)
    f_p = _cumsum_lanes(lf_rows).reshape(batch * N_HEADS, 1, s_len)
    oa_p, ob_p = _mixers_prompt(qkv_p, bias_band, f_p, batch, s_len)
    y_prompt = finish(xp, p_prompt[0].reshape(tp, D_PLE), oa_p, ob_p, gates_p)

    ts = dec_batch * t_new
    xs = x_sample.reshape(ts, D_MODEL)
    gates_s, lf_s, qkv_s, ka_s, va_s, kb_s, vb_s = project(xs, a_rows=ts, a_period=1, tm=ts)
    lf_new = lf_s.reshape(dec_batch, t_new, N_HEADS)
    lf_all = jnp.concatenate([cache_b_logf[0].astype(F32), lf_new], axis=1)
    lf_all = _pad_lanes(lf_all.transpose(0, 2, 1), V7X_LANES)
    f_s = _cumsum_lanes(lf_all.reshape(dec_batch * N_HEADS, -1)).reshape(dec_batch, N_HEADS, -1)
    oa_s, sk_s, sv_s = _band_sample(qkv_s, ka_s, va_s, cache_a_k[0], cache_a_v[0], bias_raw,
                                    dec_batch, t_new)
    ob_s = _fox_sample(qkv_s, cache_b_k[0], cache_b_v[0], f_s, dec_batch, t_new)
    y_sample = finish(xs, p_sample[0].reshape(ts, D_PLE), oa_s, ob_s, gates_s)

    def state(a, b):
        return a.reshape(1, b, -1, N_HEADS, HEAD_DIM)

    return (y_prompt.reshape(batch, s_len, D_MODEL),
            y_sample.reshape(dec_batch, t_new, D_MODEL),
            state(ka_p, batch), state(va_p, batch), state(kb_p, batch), state(vb_p, batch),
            lf_p.reshape(1, batch, s_len, N_HEADS),
            sk_s[None], sv_s[None],
            state(kb_s, dec_batch), state(vb_s, dec_batch),
            lf_new[None])
```

```python
import functools
import math

import jax
import jax.numpy as jnp
from jax import lax
from jax.experimental import pallas as pl
from jax.experimental.pallas import tpu as pltpu

F32 = jnp.float32
BF16 = jnp.bfloat16

D_MODEL = 2048
CHUNK = 64
N_PAST_CHUNKS = 8
A_REACH = N_PAST_CHUNKS * CHUNK
HEAD_DIM = 128
N_HEADS = 8
W_MIX = N_HEADS * HEAD_DIM
REL_CLIP = 128
D_FF = 4 * D_MODEL
D_PLE = 256
RMS_EPS = 1e-6
SCALE = HEAD_DIM ** -0.5
NEG = -1e30
LOG2E = math.log2(math.e)
SCALE2 = SCALE * LOG2E

V7X_LANES = 128
V7X_VMEM_LIMIT_CAP = 56 * 1024 * 1024
MIB = 1024 * 1024

NORM_ROWS = 128
BAND_TQ = 256
BAND_W = A_REACH + BAND_TQ
BAND_G = BAND_W + BAND_TQ
PV_COLS = 2 * HEAD_DIM
FOX_T = 256
MIX_HEADS = 2
GATE_COLS = 2048
SEG_COLS = 1024
FFN_CHUNK = 512

_NT = (((1,), (1,)), ((), ()))


def _vmem_limit(block_bytes, scratch_bytes, temp_bytes):
    est = 2 * block_bytes + scratch_bytes + temp_bytes
    return int(min(max(est, 16 * MIB), V7X_VMEM_LIMIT_CAP))


def _sigmoid(x):
    return 0.5 * jnp.tanh(0.5 * x) + 0.5


def _rmsnorm_rows(x, g):
    ms = jnp.mean(x * x, axis=-1, keepdims=True)
    return (x * lax.rsqrt(ms + RMS_EPS)) * g


def _norm_to_bf16(x_ref, g_ref, n_ref):
    rows = min(NORM_ROWS, x_ref.shape[0])

    def body(c, carry):
        r = pl.ds(pl.multiple_of(c * rows, rows), rows)
        n_ref[r, :] = _rmsnorm_rows(x_ref[r, :], g_ref[...]).astype(BF16)
        return carry

    lax.fori_loop(0, x_ref.shape[0] // rows, body, 0)


def _softmax_pv(parts):
    m = functools.reduce(jnp.maximum, [jnp.max(s, axis=-1, keepdims=True) for s, _ in parts])
    ps = [jnp.exp2(s - m) for s, _ in parts]
    l = functools.reduce(jnp.add, [jnp.sum(p, axis=-1, keepdims=True) for p in ps])
    o = functools.reduce(jnp.add, [jnp.dot(p.astype(BF16), v, preferred_element_type=F32)
                                   for p, (_, v) in zip(ps, parts)])
    return o / l


def _widen_values(v_ref, vp_ref):
    lane = lax.broadcasted_iota(jnp.int32, v_ref.shape, 1)
    vp_ref[:, :HEAD_DIM] = v_ref[...]
    vp_ref[:, HEAD_DIM:] = jnp.where(lane == 0, 1.0, 0.0).astype(BF16)


def _softmax_weights(parts):
    m = functools.reduce(jnp.maximum, [jnp.max(s, axis=-1, keepdims=True) for s, _ in parts])
    return [(jnp.exp2(s - m).astype(BF16), v) for s, v in parts]


def _weighted_values_wide(weighted):
    o = functools.reduce(jnp.add, [jnp.dot(p, v, preferred_element_type=F32) for p, v in weighted])
    return o[:, :HEAD_DIM] / o[:, HEAD_DIM:HEAD_DIM + 1]


def _attention_tiles(n_tiles, logits, store):
    stage1 = {0: logits(0)}
    if n_tiles > 1:
        stage1[1] = logits(1)
    stage2 = {0: _softmax_weights(stage1.pop(0))}
    for t in range(n_tiles):
        if t + 2 < n_tiles:
            stage1[t + 2] = logits(t + 2)
        if t + 1 < n_tiles:
            stage2[t + 1] = _softmax_weights(stage1.pop(t + 1))
        store(t, _weighted_values_wide(stage2.pop(t)))


def _store_heads(ref, rows):
    stacked = jnp.stack([rows[:, h * HEAD_DIM:(h + 1) * HEAD_DIM] for h in range(N_HEADS)], axis=0)
    ref[...] = jnp.swapaxes(stacked, 0, 1)


def _head_major(rows):
    return jnp.swapaxes(rows, 0, 1)


def _inproj_g_kernel(x_ref, g_ref, w_ref, wf_ref, bf_ref, gate_ref, lf_ref, n_ref):
    @pl.when(pl.program_id(1) == 0)
    def _():
        _norm_to_bf16(x_ref, g_ref, n_ref)
        z = jnp.dot(n_ref[...], wf_ref[...], preferred_element_type=F32) + bf_ref[...]
        lf = jnp.minimum(z, 0.0) - jnp.log1p(jnp.exp(-jnp.abs(z)))
        lf_ref[...] = lf[:, :N_HEADS]

    a = lax.dot_general(n_ref[...], w_ref[...], _NT, preferred_element_type=F32)
    gate_ref[...] = _sigmoid(a).astype(BF16)


def _in_proj_g(x, g, w_g, w_f, b_f, *, tm):
    t = x.shape[0]
    tn = GATE_COLS
    blocks = (tm * D_MODEL * 4 + D_MODEL * tn * 2 + D_MODEL * V7X_LANES * 2
              + tm * tn * 2 + tm * V7X_LANES * 4 + tm * D_MODEL * 2)
    return pl.pallas_call(
        _inproj_g_kernel,
        grid=(t // tm, 4 * SEG_COLS // GATE_COLS),
        in_specs=[
            pl.BlockSpec((tm, D_MODEL), lambda i, j: (i, 0)),
            pl.BlockSpec((1, D_MODEL), lambda i, j: (0, 0)),
            pl.BlockSpec((tn, D_MODEL), lambda i, j: (j, 0)),
            pl.BlockSpec((D_MODEL, V7X_LANES), lambda i, j: (0, 0)),
            pl.BlockSpec((1, V7X_LANES), lambda i, j: (0, 0)),
        ],
        out_specs=[
            pl.BlockSpec((tm, tn), lambda i, j: (i, j)),
            pl.BlockSpec((tm, N_HEADS), lambda i, j: (i, 0)),
            pl.BlockSpec((tm, D_MODEL), lambda i, j: (i, 0)),
        ],
        out_shape=[
            jax.ShapeDtypeStruct((t, 4 * SEG_COLS), BF16),
            jax.ShapeDtypeStruct((t, N_HEADS), F32),
            jax.ShapeDtypeStruct((t, D_MODEL), BF16),
        ],
        compiler_params=pltpu.CompilerParams(
            dimension_semantics=("parallel", "arbitrary"),
            vmem_limit_bytes=_vmem_limit(blocks, 0, 10 * MIB)),
        name="in_proj_g",
    )(x, g, w_g, w_f, b_f)


def _inproj_qkv_kernel(n_ref, w_ref, qkv_ref, ka_ref, va_ref, kb_ref, vb_ref, *, a_rows, a_period):
    i = pl.program_id(0)
    j = pl.program_id(1)
    tm = n_ref.shape[0]
    is_tail = (i % a_period) == (a_period - 1)
    is_a_kv = jnp.logical_or(j == 1, j == 2)

    def project():
        return lax.dot_general(n_ref[...], w_ref[...], _NT, preferred_element_type=F32)

    plain = jnp.logical_or(jnp.logical_or(j == 0, j == 3),
                           jnp.logical_and(is_a_kv, jnp.logical_not(is_tail)))

    @pl.when(plain)
    def _():
        qkv_ref[...] = project().astype(BF16)

    for col, tail_ref in ((1, ka_ref), (2, va_ref)):
        @pl.when(jnp.logical_and(j == col, is_tail))
        def _(tail_ref=tail_ref):
            r = project()
            qkv_ref[...] = r.astype(BF16)
            _store_heads(tail_ref, r[tm - a_rows:, :])

    for col, full_ref in ((4, kb_ref), (5, vb_ref)):
        @pl.when(j == col)
        def _(full_ref=full_ref):
            r = project()
            qkv_ref[...] = r.astype(BF16)
            _store_heads(full_ref, r)


def _in_proj_qkv(n, w_t, *, a_rows, a_period, tm):
    t = n.shape[0]
    tn = SEG_COLS
    n_a = (t // tm) // a_period * a_rows
    blocks = (tm * D_MODEL * 2 + D_MODEL * tn * 2 + tm * tn * 2
              + 2 * a_rows * tn * 4 + 2 * tm * tn * 4)
    kern = functools.partial(_inproj_qkv_kernel, a_rows=a_rows, a_period=a_period)
    head_block = lambda rows: (rows, N_HEADS, HEAD_DIM)
    return pl.pallas_call(
        kern,
        grid=(t // tm, 6),
        in_specs=[
            pl.BlockSpec((tm, D_MODEL), lambda i, j: (i, 0)),
            pl.BlockSpec((tn, D_MODEL), lambda i, j: (j, 0)),
        ],
        out_specs=[
            pl.BlockSpec((tm, tn), lambda i, j: (i, j)),
            pl.BlockSpec(head_block(a_rows), lambda i, j: (i // a_period, 0, 0)),
            pl.BlockSpec(head_block(a_rows), lambda i, j: (i // a_period, 0, 0)),
            pl.BlockSpec(head_block(tm), lambda i, j: (i, 0, 0)),
            pl.BlockSpec(head_block(tm), lambda i, j: (i, 0, 0)),
        ],
        out_shape=[
            jax.ShapeDtypeStruct((t, 6 * SEG_COLS), BF16),
            jax.ShapeDtypeStruct(head_block(n_a), F32),
            jax.ShapeDtypeStruct(head_block(n_a), F32),
            jax.ShapeDtypeStruct(head_block(t), F32),
            jax.ShapeDtypeStruct(head_block(t), F32),
        ],
        compiler_params=pltpu.CompilerParams(
            dimension_semantics=("parallel", "arbitrary"),
            vmem_limit_bytes=_vmem_limit(blocks, 0, 6 * MIB)),
        name="in_proj_qkv",
    )(n, w_t)


def _cumsum_kernel(x_ref, o_ref):
    rows, length = x_ref.shape
    r_i = lax.broadcasted_iota(jnp.int32, (V7X_LANES, V7X_LANES), 0)
    c_i = lax.broadcasted_iota(jnp.int32, (V7X_LANES, V7X_LANES), 1)
    tri = (r_i <= c_i).astype(BF16)
    carry = jnp.zeros((rows, 1), F32)
    for c in range(length // V7X_LANES):
        x = x_ref[:, c * V7X_LANES:(c + 1) * V7X_LANES]
        hi = x.astype(BF16)
        r1 = x - hi.astype(F32)
        mid = r1.astype(BF16)
        lo = (r1 - mid.astype(F32)).astype(BF16)
        blk = (jnp.dot(hi, tri, preferred_element_type=F32)
               + jnp.dot(mid, tri, preferred_element_type=F32)
               + jnp.dot(lo, tri, preferred_element_type=F32)) + carry
        o_ref[:, c * V7X_LANES:(c + 1) * V7X_LANES] = blk
        carry = blk[:, V7X_LANES - 1:V7X_LANES]


def _cumsum_lanes(x):
    return pl.pallas_call(
        _cumsum_kernel,
        out_shape=jax.ShapeDtypeStruct(x.shape, F32),
        name="logf_cumsum",
    )(x)


def _rel_bias_kernel(g_ref, raw_ref, band_ref):
    row = lax.broadcasted_iota(jnp.int32, (BAND_TQ, BAND_W), 0)
    col = lax.broadcasted_iota(jnp.int32, (BAND_TQ, BAND_W), 1)
    q_chunk = row // CHUNK
    k_chunk = col // CHUNK - N_PAST_CHUNKS
    in_band = jnp.logical_and(k_chunk >= q_chunk - N_PAST_CHUNKS, k_chunk <= q_chunk)
    for h in range(N_HEADS):
        rows = jnp.broadcast_to(g_ref[h:h + 1, :], (BAND_TQ, BAND_G))
        toeplitz = pltpu.roll(rows, 0, 1, stride=1, stride_axis=0)[:, :BAND_W] * LOG2E
        raw_ref[h] = toeplitz
        band_ref[h] = jnp.where(in_band, toeplitz, NEG)


def _rel_bias_tiles(table):
    far_past = table[:, 2 * REL_CLIP:]
    far_future = table[:, :1]
    near = jnp.flip(table[:, :2 * REL_CLIP], axis=1)
    n_past = A_REACH - REL_CLIP + 1
    n_future = BAND_W - n_past - near.shape[1]
    g = jnp.concatenate([jnp.broadcast_to(far_past, (N_HEADS, n_past)), near,
                         jnp.broadcast_to(far_future, (N_HEADS, n_future)),
                         jnp.broadcast_to(far_past, (N_HEADS, BAND_G - BAND_W))], axis=1)
    shape = jax.ShapeDtypeStruct((N_HEADS, BAND_TQ, BAND_W), F32)
    return pl.pallas_call(_rel_bias_kernel, out_shape=[shape, shape], name="rel_bias")(g.astype(F32))


def _mixers_prompt_kernel(qa_ref, ka_ref, va_ref, qb_ref, kb_ref, vb_ref, bias_ref, f_ref,
                          oa_ref, ob_ref, vpa_ref, vpb_ref):
    s_len = qa_ref.shape[0]
    row = lax.broadcasted_iota(jnp.int32, (FOX_T, FOX_T), 0)
    col = lax.broadcasted_iota(jnp.int32, (FOX_T, FOX_T), 1)
    causal = row >= col
    lanes = [slice(h * HEAD_DIM, (h + 1) * HEAD_DIM) for h in range(MIX_HEADS)]
    f2 = []
    for h, ls in enumerate(lanes):
        _widen_values(va_ref.at[:, ls], vpa_ref.at[h])
        _widen_values(vb_ref.at[:, ls], vpb_ref.at[h])
        f2.append(f_ref[h] * LOG2E)

    def band_logits(h, t):
        q0 = t * BAND_TQ
        k0 = max(q0 - A_REACH, 0)
        n_keys = q0 + BAND_TQ - k0
        s = lax.dot_general(qa_ref[q0:q0 + BAND_TQ, lanes[h]], ka_ref[k0:k0 + n_keys, lanes[h]], _NT,
                            preferred_element_type=F32)
        return [(s * SCALE2 + bias_ref[h, :, BAND_W - n_keys:], vpa_ref[h, k0:k0 + n_keys, :])]

    def fox_logits(h, qi):
        q0 = qi * FOX_T
        q = qb_ref[q0:q0 + FOX_T, lanes[h]]
        s_diag = (lax.dot_general(q, kb_ref[q0:q0 + FOX_T, lanes[h]], _NT, preferred_element_type=F32)
                  * SCALE2 - f2[h][:, q0:q0 + FOX_T])
        parts = [(jnp.where(causal, s_diag, NEG), vpb_ref[h, q0:q0 + FOX_T, :])]
        if qi:
            s_past = (lax.dot_general(q, kb_ref[0:q0, lanes[h]], _NT, preferred_element_type=F32) * SCALE2
                      - f2[h][:, 0:q0])
            parts.append((s_past, vpb_ref[h, 0:q0, :]))
        return parts

    n_band = s_len // BAND_TQ
    n_fox = s_len // FOX_T
    order = []
    for t in range(max(n_band, n_fox)):
        for h in range(MIX_HEADS):
            if t < n_band:
                order.append(("band", h, t))
            if t < n_fox:
                order.append(("fox", h, t))

    def logits(i):
        kind, h, t = order[i]
        return band_logits(h, t) if kind == "band" else fox_logits(h, t)

    def store(i, o):
        kind, h, t = order[i]
        if kind == "band":
            oa_ref[t * BAND_TQ:(t + 1) * BAND_TQ, lanes[h]] = o.astype(BF16)
        else:
            ob_ref[t * FOX_T:(t + 1) * FOX_T, lanes[h]] = o.astype(BF16)

    _attention_tiles(len(order), logits, store)


def _mixers_prompt(qkv, bias, f_rows, batch, s_len):
    t = batch * s_len
    groups = N_HEADS // MIX_HEADS
    width = MIX_HEADS * HEAD_DIM
    head = lambda seg: pl.BlockSpec((s_len, width), lambda b, g: (b, seg * groups + g))
    blocks = 8 * s_len * width * 2 + MIX_HEADS * (BAND_TQ * BAND_W * 4 + 8 * s_len * 4)
    scratch = 2 * MIX_HEADS * s_len * PV_COLS * 2
    out = jax.ShapeDtypeStruct((t, W_MIX), BF16)
    vp = pltpu.VMEM((MIX_HEADS, s_len, PV_COLS), BF16)
    return pl.pallas_call(
        _mixers_prompt_kernel,
        grid=(batch, groups),
        in_specs=[head(s) for s in range(6)] + [
            pl.BlockSpec((MIX_HEADS, BAND_TQ, BAND_W), lambda b, g: (g, 0, 0)),
            pl.BlockSpec((MIX_HEADS, 1, s_len), lambda b, g: (b * groups + g, 0, 0))],
        out_specs=[head(0), head(0)],
        out_shape=[out, out],
        scratch_shapes=[vp, vp],
        compiler_params=pltpu.CompilerParams(
            dimension_semantics=("parallel", "parallel"),
            vmem_limit_bytes=_vmem_limit(blocks, scratch, 8 * FOX_T * s_len * 4)),
        name="mixers_prompt",
    )(qkv, qkv, qkv, qkv, qkv, qkv, bias, f_rows)


def _band_sample_kernel(q_ref, kn_ref, vn_ref, kn32_ref, vn32_ref, ck_ref, cv_ref, bias_ref,
                        o_ref, sk_ref, sv_ref):
    n_cache = ck_ref.shape[1]
    t_new = q_ref.shape[0]
    k_heads = _head_major(ck_ref[0])
    v_heads = _head_major(cv_ref[0])
    for h in range(N_HEADS):
        hs = slice(h * HEAD_DIM, (h + 1) * HEAD_DIM)
        q = q_ref[:, hs]
        kc = k_heads[h].astype(BF16)
        vc = v_heads[h].astype(BF16)
        s1 = (lax.dot_general(q, kc, _NT, preferred_element_type=F32) * SCALE2
              + bias_ref[h, :t_new, :n_cache])
        s2 = (lax.dot_general(q, kn_ref[:, hs], _NT, preferred_element_type=F32) * SCALE2
              + bias_ref[h, :t_new, n_cache:n_cache + t_new])
        o_ref[:, hs] = _softmax_pv([(s1, vc), (s2, vn_ref[:, hs])]).astype(BF16)
    for cache_ref, new_ref, state_ref in ((ck_ref, kn32_ref, sk_ref), (cv_ref, vn32_ref, sv_ref)):
        state_ref[0, :n_cache - t_new] = cache_ref[0, t_new:]
        state_ref[0, n_cache - t_new:] = new_ref[...]


def _band_sample(qkv, ka32, va32, cache_k, cache_v, bias, batch, t_new):
    n_cache = cache_k.shape[1]
    cache_block = (1, n_cache, N_HEADS, HEAD_DIM)
    new_block = (t_new, N_HEADS, HEAD_DIM)
    blocks = 4 * t_new * W_MIX * 2 + 2 * t_new * W_MIX * 4 + 4 * n_cache * W_MIX * 4 + bias.size * 4
    return pl.pallas_call(
        _band_sample_kernel,
        grid=(batch,),
        in_specs=[
            pl.BlockSpec((t_new, W_MIX), lambda b: (b, 0)),
            pl.BlockSpec((t_new, W_MIX), lambda b: (b, 1)),
            pl.BlockSpec((t_new, W_MIX), lambda b: (b, 2)),
            pl.BlockSpec(new_block, lambda b: (b, 0, 0)),
            pl.BlockSpec(new_block, lambda b: (b, 0, 0)),
            pl.BlockSpec(cache_block, lambda b: (b, 0, 0, 0)),
            pl.BlockSpec(cache_block, lambda b: (b, 0, 0, 0)),
            pl.BlockSpec(bias.shape, lambda b: (0, 0, 0)),
        ],
        out_specs=[
            pl.BlockSpec((t_new, W_MIX), lambda b: (b, 0)),
            pl.BlockSpec(cache_block, lambda b: (b, 0, 0, 0)),
            pl.BlockSpec(cache_block, lambda b: (b, 0, 0, 0)),
        ],
        out_shape=[
            jax.ShapeDtypeStruct((batch * t_new, W_MIX), BF16),
            jax.ShapeDtypeStruct(cache_k.shape, F32),
            jax.ShapeDtypeStruct(cache_v.shape, F32),
        ],
        compiler_params=pltpu.CompilerParams(
            dimension_semantics=("parallel",),
            vmem_limit_bytes=_vmem_limit(blocks, 0, 4 * MIB)),
        name="band_sample",
    )(qkv, qkv, qkv, ka32, va32, cache_k, cache_v, bias)


def _fox_sample_kernel(q_ref, kn_ref, vn_ref, ck_ref, cv_ref, f_ref, o_ref):
    n_cache = ck_ref.shape[1]
    t_new = q_ref.shape[0]
    row = lax.broadcasted_iota(jnp.int32, (t_new, t_new), 0)
    col = lax.broadcasted_iota(jnp.int32, (t_new, t_new), 1)
    causal = row >= col
    k_heads = _head_major(ck_ref[0])
    v_heads = _head_major(cv_ref[0])
    for h in range(N_HEADS):
        hs = slice(h * HEAD_DIM, (h + 1) * HEAD_DIM)
        q = q_ref[:, hs]
        f2 = f_ref[0, h:h + 1, :] * LOG2E
        kc = k_heads[h].astype(BF16)
        vc = v_heads[h].astype(BF16)
        s1 = lax.dot_general(q, kc, _NT, preferred_element_type=F32) * SCALE2 - f2[:, :n_cache]
        s2 = (lax.dot_general(q, kn_ref[:, hs], _NT, preferred_element_type=F32) * SCALE2
              - f2[:, n_cache:n_cache + t_new])
        parts = [(s1, vc), (jnp.where(causal, s2, NEG), vn_ref[:, hs])]
        o_ref[:, hs] = _softmax_pv(parts).astype(BF16)


def _fox_sample(qkv, cache_k, cache_v, f_rows, batch, t_new):
    n_cache = cache_k.shape[1]
    f_len = f_rows.shape[-1]
    cache_block = (1, n_cache, N_HEADS, HEAD_DIM)
    blocks = 4 * t_new * W_MIX * 2 + 2 * n_cache * W_MIX * 4 + N_HEADS * f_len * 4
    return pl.pallas_call(
        _fox_sample_kernel,
        grid=(batch,),
        in_specs=[
            pl.BlockSpec((t_new, W_MIX), lambda b: (b, 3)),
            pl.BlockSpec((t_new, W_MIX), lambda b: (b, 4)),
            pl.BlockSpec((t_new, W_MIX), lambda b: (b, 5)),
            pl.BlockSpec(cache_block, lambda b: (b, 0, 0, 0)),
            pl.BlockSpec(cache_block, lambda b: (b, 0, 0, 0)),
            pl.BlockSpec((1, N_HEADS, f_len), lambda b: (b, 0, 0)),
        ],
        out_specs=pl.BlockSpec((t_new, W_MIX), lambda b: (b, 0)),
        out_shape=jax.ShapeDtypeStruct((batch * t_new, W_MIX), BF16),
        compiler_params=pltpu.CompilerParams(
            dimension_semantics=("parallel",),
            vmem_limit_bytes=_vmem_limit(blocks, 0, 8 * MIB)),
        name="fox_sample",
    )(qkv, qkv, qkv, cache_k, cache_v, f_rows)


def _merge_kernel(oa_ref, ob_ref, gate_ref, x_ref, wa_ref, wb_ref, wo_ref, h_ref):
    ya = jnp.dot(oa_ref[...], wa_ref[...], preferred_element_type=F32)
    yb = jnp.dot(ob_ref[...], wb_ref[...], preferred_element_type=F32)
    m = (gate_ref[:, :D_MODEL].astype(F32) * ya + gate_ref[:, D_MODEL:].astype(F32) * yb)
    h_ref[...] = x_ref[...] + jnp.dot(m.astype(BF16), wo_ref[...], preferred_element_type=F32)


def _resident(shape):
    return pl.BlockSpec(shape, lambda *_: (0,) * len(shape), pipeline_mode=pl.Buffered(1))


def _merge(oa, ob, gates, x, w_a, w_b, w_o, *, tm):
    t = x.shape[0]
    blocks = 2 * tm * W_MIX * 2 + tm * 2 * D_MODEL * 2 + 2 * tm * D_MODEL * 4
    weights = (2 * W_MIX * D_MODEL + D_MODEL * D_MODEL) * 2
    return pl.pallas_call(
        _merge_kernel,
        grid=(t // tm,),
        in_specs=[
            pl.BlockSpec((tm, W_MIX), lambda i: (i, 0)),
            pl.BlockSpec((tm, W_MIX), lambda i: (i, 0)),
            pl.BlockSpec((tm, 2 * D_MODEL), lambda i: (i, 0)),
            pl.BlockSpec((tm, D_MODEL), lambda i: (i, 0)),
            _resident((W_MIX, D_MODEL)),
            _resident((W_MIX, D_MODEL)),
            _resident((D_MODEL, D_MODEL)),
        ],
        out_specs=pl.BlockSpec((tm, D_MODEL), lambda i: (i, 0)),
        out_shape=jax.ShapeDtypeStruct((t, D_MODEL), F32),
        compiler_params=pltpu.CompilerParams(
            dimension_semantics=("parallel",),
            vmem_limit_bytes=_vmem_limit(blocks, weights, 4 * tm * D_MODEL * 4)),
        name="merge",
    )(oa, ob, gates, x, w_a, w_b, w_o)


def _ffn_kernel(h_ref, g_ref, wu_ref, wd_ref, o_ref, n_ref):
    tf = wu_ref.shape[1]
    chunks = [slice(c * FFN_CHUNK, (c + 1) * FFN_CHUNK) for c in range(tf // FFN_CHUNK)]

    def hidden(c):
        a = jnp.dot(n_ref[...], wu_ref[:, c], preferred_element_type=F32)
        r = jnp.square(jnp.maximum(a, 0.0)).astype(BF16)
        return jnp.dot(r, wd_ref[c, :], preferred_element_type=F32)

    @pl.when(pl.program_id(1) == 0)
    def _():
        _norm_to_bf16(h_ref, g_ref, n_ref)
        o_ref[...] = h_ref[...] + hidden(chunks[0])

    @pl.when(pl.program_id(1) != 0)
    def _():
        o_ref[...] += hidden(chunks[0])

    for c in chunks[1:]:
        o_ref[...] += hidden(c)


def _ffn(h, g, w_up, w_down, *, tm, tf):
    t = h.shape[0]
    blocks = 2 * tm * D_MODEL * 4 + 2 * D_MODEL * tf * 2
    scratch = tm * D_MODEL * 2
    return pl.pallas_call(
        _ffn_kernel,
        grid=(t // tm, D_FF // tf),
        in_specs=[
            pl.BlockSpec((tm, D_MODEL), lambda i, f: (i, 0)),
            pl.BlockSpec((1, D_MODEL), lambda i, f: (0, 0)),
            pl.BlockSpec((D_MODEL, tf), lambda i, f: (0, f)),
            pl.BlockSpec((tf, D_MODEL), lambda i, f: (f, 0)),
        ],
        out_specs=pl.BlockSpec((tm, D_MODEL), lambda i, f: (i, 0)),
        out_shape=jax.ShapeDtypeStruct((t, D_MODEL), F32),
        scratch_shapes=[pltpu.VMEM((tm, D_MODEL), BF16)],
        compiler_params=pltpu.CompilerParams(
            dimension_semantics=("parallel", "arbitrary"),
            vmem_limit_bytes=_vmem_limit(blocks, scratch, 2 * tm * tf * 4)),
        name="ffn",
    )(h, g, w_up, w_down)


def _ple_kernel(h_ref, p_ref, gp_ref, gf_ref, wg_ref, wp_ref, y_ref):
    h = h_ref[...]
    n = _rmsnorm_rows(h, gp_ref[...]).astype(BF16)
    gate = _sigmoid(jnp.dot(n, wg_ref[...], preferred_element_type=F32))
    proj = jnp.dot(p_ref[...].astype(BF16), wp_ref[...], preferred_element_type=F32)
    y_ref[...] = _rmsnorm_rows(h + proj * gate, gf_ref[...])


def _ple_final(h, p, g_ple, g_final, w_gate, w_proj, *, tm):
    t = h.shape[0]
    blocks = 2 * tm * D_MODEL * 4 + tm * D_PLE * 4
    weights = (D_MODEL * D_MODEL + D_PLE * D_MODEL) * 2
    return pl.pallas_call(
        _ple_kernel,
        grid=(t // tm,),
        in_specs=[
            pl.BlockSpec((tm, D_MODEL), lambda i: (i, 0)),
            pl.BlockSpec((tm, D_PLE), lambda i: (i, 0)),
            pl.BlockSpec((1, D_MODEL), lambda i: (0, 0)),
            pl.BlockSpec((1, D_MODEL), lambda i: (0, 0)),
            _resident((D_MODEL, D_MODEL)),
            _resident((D_PLE, D_MODEL)),
        ],
        out_specs=pl.BlockSpec((tm, D_MODEL), lambda i: (i, 0)),
        out_shape=jax.ShapeDtypeStruct((t, D_MODEL), F32),
        compiler_params=pltpu.CompilerParams(
            dimension_semantics=("parallel",),
            vmem_limit_bytes=_vmem_limit(blocks, weights, 4 * tm * D_MODEL * 4)),
        name="ple_final",
    )(h, p, g_ple, g_final, w_gate, w_proj)


def _pad_lanes(x, multiple):
    pad = (-x.shape[-1]) % multiple
    return jnp.pad(x, ((0, 0),) * (x.ndim - 1) + ((0, pad),)) if pad else x


def _token_tile(t, preferred):
    return preferred if t % preferred == 0 else t


def kernel(x_prompt, x_sample, p_prompt, p_sample, cache_a_k, cache_a_v, cache_b_k, cache_b_v,
           cache_b_logf, g_mix, w_in, b_f, rel_bias, w_a_proj, w_b_proj, w_o, g_mlp, w_up, w_down,
           g_ple, w_ple_gate, w_ple_proj, g_final):
    depth = w_in.shape[0]
    assert depth == 1, "single-layer step"
    batch, s_len, _ = x_prompt.shape
    dec_batch, t_new, _ = x_sample.shape
    n_cache_a = cache_a_k.shape[2]
    assert s_len % FOX_T == 0 and s_len >= A_REACH and t_new == CHUNK and n_cache_a == A_REACH

    w = w_in[0]
    n_qkv = 6 * SEG_COLS
    w_t = jnp.swapaxes(w, 0, 1).astype(BF16)
    w_g = w_t[n_qkv + N_HEADS:]
    w_f = _pad_lanes(w[:, n_qkv:n_qkv + N_HEADS], V7X_LANES).astype(BF16)
    b_f_row = _pad_lanes(b_f[0][None, :].astype(F32), V7X_LANES)
    g_mix_row = g_mix[0][None, :].astype(F32)
    g_mlp_row = g_mlp[0][None, :].astype(F32)
    g_ple_row = g_ple[0][None, :].astype(F32)
    g_final_row = g_final[None, :].astype(F32)
    w_a = w_a_proj[0].astype(BF16)
    w_b = w_b_proj[0].astype(BF16)
    w_o_b = w_o[0].astype(BF16)
    w_up_b = w_up[0].astype(BF16)
    w_down_b = w_down[0].astype(BF16)
    w_pg = w_ple_gate[0].astype(BF16)
    w_pp = w_ple_proj[0].astype(BF16)
    bias_raw, bias_band = _rel_bias_tiles(rel_bias[0])

    def project(x2d, *, a_rows, a_period, tm):
        gates, lf, n = _in_proj_g(x2d, g_mix_row, w_g, w_f, b_f_row, tm=tm)
        return (gates, lf) + tuple(_in_proj_qkv(n, w_t, a_rows=a_rows, a_period=a_period, tm=tm))

    def finish(x2d, p2d, oa, ob, gates):
        t = x2d.shape[0]
        h1 = _merge(oa, ob, gates, x2d, w_a, w_b, w_o_b, tm=_token_tile(t, 256))
        h2 = _ffn(h1, g_mlp_row, w_up_b, w_down_b, tm=_token_tile(t, 1024), tf=1024)
        return _ple_final(h2, p2d, g_ple_row, g_final_row, w_pg, w_pp, tm=_token_tile(t, 512))

    tp = batch * s_len
    xp = x_prompt.reshape(tp, D_MODEL)
    tm_p = _token_tile(s_len, 1024)
    gates_p, lf_p, qkv_p, ka_p, va_p, kb_p, vb_p = project(
        xp, a_rows=min(A_REACH, tm_p), a_period=s_len // tm_p, tm=tm_p)
    lf_rows = lf_p.reshape(batch, s_len, N_HEADS).transpose(0, 2, 1).reshape(batch * N_HEADS, s_len)
    f_p = _cumsum_lanes(lf_rows).reshape(batch * N_HEADS, 1, s_len)
    oa_p, ob_p = _mixers_prompt(qkv_p, bias_band, f_p, batch, s_len)
    y_prompt = finish(xp, p_prompt[0].reshape(tp, D_PLE), oa_p, ob_p, gates_p)

    ts = dec_batch * t_new
    xs = x_sample.reshape(ts, D_MODEL)
    gates_s, lf_s, qkv_s, ka_s, va_s, kb_s, vb_s = project(xs, a_rows=ts, a_period=1, tm=ts)
    lf_new = lf_s.reshape(dec_batch, t_new, N_HEADS)
    lf_all = jnp.concatenate([cache_b_logf[0].astype(F32), lf_new], axis=1)
    lf_all = _pad_lanes(lf_all.transpose(0, 2, 1), V7X_LANES)
    f_s = _cumsum_lanes(lf_all.reshape(dec_batch * N_HEADS, -1)).reshape(dec_batch, N_HEADS, -1)
    oa_s, sk_s, sv_s = _band_sample(qkv_s, ka_s, va_s, cache_a_k[0], cache_a_v[0], bias_raw,
                                    dec_batch, t_new)
    ob_s = _fox_sample(qkv_s, cache_b_k[0], cache_b_v[0], f_s, dec_batch, t_new)
    y_sample = finish(xs, p_sample[0].reshape(ts, D_PLE), oa_s, ob_s, gates_s)

    def state(a, b):
        return a.reshape(1, b, -1, N_HEADS, HEAD_DIM)

    return (y_prompt.reshape(batch, s_len, D_MODEL),
            y_sample.reshape(dec_batch, t_new, D_MODEL),
            state(ka_p, batch), state(va_p, batch), state(kb_p, batch), state(vb_p, batch),
            lf_p.reshape(1, batch, s_len, N_HEADS),
            sk_s[None], sv_s[None],
            state(kb_s, dec_batch), state(vb_s, dec_batch),
            lf_new[None])
```

```python
import functools
import math

import jax
import jax.numpy as jnp
from jax import lax
from jax.experimental import pallas as pl
from jax.experimental.pallas import tpu as pltpu

F32 = jnp.float32
BF16 = jnp.bfloat16

D_MODEL = 2048
CHUNK = 64
N_PAST_CHUNKS = 8
A_REACH = N_PAST_CHUNKS * CHUNK
HEAD_DIM = 128
N_HEADS = 8
W_MIX = N_HEADS * HEAD_DIM
REL_CLIP = 128
D_FF = 4 * D_MODEL
D_PLE = 256
RMS_EPS = 1e-6
SCALE = HEAD_DIM ** -0.5
NEG = -1e30
LOG2E = math.log2(math.e)
SCALE2 = SCALE * LOG2E

V7X_LANES = 128
V7X_VMEM_LIMIT_CAP = 56 * 1024 * 1024
MIB = 1024 * 1024

NORM_ROWS = 128
BAND_TQ = 256
BAND_W = A_REACH + BAND_TQ
BAND_G = BAND_W + BAND_TQ
PV_COLS = 2 * HEAD_DIM
FOX_T = 256
MIX_HEADS = 2
GATE_COLS = 2048
SEG_COLS = 1024
FFN_CHUNK = 512

_NT = (((1,), (1,)), ((), ()))


def _vmem_limit(block_bytes, scratch_bytes, temp_bytes):
    est = 2 * block_bytes + scratch_bytes + temp_bytes
    return int(min(max(est, 16 * MIB), V7X_VMEM_LIMIT_CAP))


def _sigmoid(x):
    return 0.5 * jnp.tanh(0.5 * x) + 0.5


def _rmsnorm_rows(x, g):
    ms = jnp.mean(x * x, axis=-1, keepdims=True)
    return (x * lax.rsqrt(ms + RMS_EPS)) * g


def _norm_to_bf16(x_ref, g_ref, n_ref):
    rows = min(NORM_ROWS, x_ref.shape[0])

    def body(c, carry):
        r = pl.ds(pl.multiple_of(c * rows, rows), rows)
        n_ref[r, :] = _rmsnorm_rows(x_ref[r, :], g_ref[...]).astype(BF16)
        return carry

    lax.fori_loop(0, x_ref.shape[0] // rows, body, 0)


def _softmax_pv(parts):
    m = functools.reduce(jnp.maximum, [jnp.max(s, axis=-1, keepdims=True) for s, _ in parts])
    ps = [jnp.exp2(s - m) for s, _ in parts]
    l = functools.reduce(jnp.add, [jnp.sum(p, axis=-1, keepdims=True) for p in ps])
    o = functools.reduce(jnp.add, [jnp.dot(p.astype(BF16), v, preferred_element_type=F32)
                                   for p, (_, v) in zip(ps, parts)])
    return o / l


def _widen_values(v_ref, vp_ref):
    lane = lax.broadcasted_iota(jnp.int32, v_ref.shape, 1)
    vp_ref[:, :HEAD_DIM] = v_ref[...]
    vp_ref[:, HEAD_DIM:] = jnp.where(lane == 0, 1.0, 0.0).astype(BF16)


def _softmax_weights(parts):
    m = functools.reduce(jnp.maximum, [jnp.max(s, axis=-1, keepdims=True) for s, _ in parts])
    return [(jnp.exp2(s - m).astype(BF16), v) for s, v in parts]


def _weighted_values_wide(weighted):
    o = functools.reduce(jnp.add, [jnp.dot(p, v, preferred_element_type=F32) for p, v in weighted])
    return o[:, :HEAD_DIM] / o[:, HEAD_DIM:HEAD_DIM + 1]


def _attention_tiles(n_tiles, logits, store):
    stage1 = {0: logits(0)}
    if n_tiles > 1:
        stage1[1] = logits(1)
    stage2 = {0: _softmax_weights(stage1.pop(0))}
    for t in range(n_tiles):
        if t + 2 < n_tiles:
            stage1[t + 2] = logits(t + 2)
        if t + 1 < n_tiles:
            stage2[t + 1] = _softmax_weights(stage1.pop(t + 1))
        store(t, _weighted_values_wide(stage2.pop(t)))


def _store_heads(ref, rows):
    stacked = jnp.stack([rows[:, h * HEAD_DIM:(h + 1) * HEAD_DIM] for h in range(N_HEADS)], axis=0)
    ref[...] = jnp.swapaxes(stacked, 0, 1)


def _head_major(rows):
    return jnp.swapaxes(rows, 0, 1)


def _inproj_g_kernel(x_ref, g_ref, w_ref, wf_ref, bf_ref, gate_ref, lf_ref, n_ref):
    @pl.when(pl.program_id(1) == 0)
    def _():
        _norm_to_bf16(x_ref, g_ref, n_ref)
        z = jnp.dot(n_ref[...], wf_ref[...], preferred_element_type=F32) + bf_ref[...]
        lf = jnp.minimum(z, 0.0) - jnp.log1p(jnp.exp(-jnp.abs(z)))
        lf_ref[...] = lf[:, :N_HEADS]

    a = lax.dot_general(n_ref[...], w_ref[...], _NT, preferred_element_type=F32)
    gate_ref[...] = _sigmoid(a).astype(BF16)


def _in_proj_g(x, g, w_g, w_f, b_f, *, tm):
    t = x.shape[0]
    tn = GATE_COLS
    blocks = (tm * D_MODEL * 4 + D_MODEL * tn * 2 + D_MODEL * V7X_LANES * 2
              + tm * tn * 2 + tm * V7X_LANES * 4 + tm * D_MODEL * 2)
    return pl.pallas_call(
        _inproj_g_kernel,
        grid=(t // tm, 4 * SEG_COLS // GATE_COLS),
        in_specs=[
            pl.BlockSpec((tm, D_MODEL), lambda i, j: (i, 0)),
            pl.BlockSpec((1, D_MODEL), lambda i, j: (0, 0)),
            pl.BlockSpec((tn, D_MODEL), lambda i, j: (j, 0)),
            pl.BlockSpec((D_MODEL, V7X_LANES), lambda i, j: (0, 0)),
            pl.BlockSpec((1, V7X_LANES), lambda i, j: (0, 0)),
        ],
        out_specs=[
            pl.BlockSpec((tm, tn), lambda i, j: (i, j)),
            pl.BlockSpec((tm, N_HEADS), lambda i, j: (i, 0)),
            pl.BlockSpec((tm, D_MODEL), lambda i, j: (i, 0)),
        ],
        out_shape=[
            jax.ShapeDtypeStruct((t, 4 * SEG_COLS), BF16),
            jax.ShapeDtypeStruct((t, N_HEADS), F32),
            jax.ShapeDtypeStruct((t, D_MODEL), BF16),
        ],
        compiler_params=pltpu.CompilerParams(
            dimension_semantics=("parallel", "arbitrary"),
            vmem_limit_bytes=_vmem_limit(blocks, 0, 10 * MIB)),
        name="in_proj_g",
    )(x, g, w_g, w_f, b_f)


def _inproj_qkv_kernel(n_ref, w_ref, qkv_ref, ka_ref, va_ref, kb_ref, vb_ref, *, a_rows, a_period):
    i = pl.program_id(0)
    j = pl.program_id(1)
    tm = n_ref.shape[0]
    is_tail = (i % a_period) == (a_period - 1)
    first = slice(0, SEG_COLS)
    second = slice(SEG_COLS, 2 * SEG_COLS)

    def project():
        return lax.dot_general(n_ref[...], w_ref[...], _NT, preferred_element_type=F32)

    @pl.when(jnp.logical_and(j < 2, jnp.logical_not(is_tail)))
    def _():
        qkv_ref[...] = project().astype(BF16)

    for step, cols, tail_ref in ((0, second, ka_ref), (1, first, va_ref)):
        @pl.when(jnp.logical_and(j == step, is_tail))
        def _(cols=cols, tail_ref=tail_ref):
            r = project()
            qkv_ref[...] = r.astype(BF16)
            _store_heads(tail_ref, r[tm - a_rows:, cols])

    @pl.when(j == 2)
    def _():
        r = project()
        qkv_ref[...] = r.astype(BF16)
        _store_heads(kb_ref, r[:, first])
        _store_heads(vb_ref, r[:, second])


def _in_proj_qkv(n, w_t, *, a_rows, a_period, tm):
    t = n.shape[0]
    tn = 2 * SEG_COLS
    n_a = (t // tm) // a_period * a_rows
    blocks = tm * D_MODEL * 2 + D_MODEL * tn * 2 + tm * tn * 2
    once = 2 * a_rows * SEG_COLS * 4 + 2 * tm * SEG_COLS * 4
    kern = functools.partial(_inproj_qkv_kernel, a_rows=a_rows, a_period=a_period)
    head_block = lambda rows: (rows, N_HEADS, HEAD_DIM)
    state = lambda rows, index: pl.BlockSpec(head_block(rows), index, pipeline_mode=pl.Buffered(1))
    return pl.pallas_call(
        kern,
        grid=(t // tm, 3),
        in_specs=[
            pl.BlockSpec((tm, D_MODEL), lambda i, j: (i, 0)),
            pl.BlockSpec((tn, D_MODEL), lambda i, j: (j, 0)),
        ],
        out_specs=[
            pl.BlockSpec((tm, tn), lambda i, j: (i, j)),
            state(a_rows, lambda i, j: (i // a_period, 0, 0)),
            state(a_rows, lambda i, j: (i // a_period, 0, 0)),
            state(tm, lambda i, j: (i, 0, 0)),
            state(tm, lambda i, j: (i, 0, 0)),
        ],
        out_shape=[
            jax.ShapeDtypeStruct((t, 6 * SEG_COLS), BF16),
            jax.ShapeDtypeStruct(head_block(n_a), F32),
            jax.ShapeDtypeStruct(head_block(n_a), F32),
            jax.ShapeDtypeStruct(head_block(t), F32),
            jax.ShapeDtypeStruct(head_block(t), F32),
        ],
        compiler_params=pltpu.CompilerParams(
            dimension_semantics=("parallel", "arbitrary"),
            vmem_limit_bytes=_vmem_limit(blocks, once, 10 * MIB)),
        name="in_proj_qkv",
    )(n, w_t)


def _cumsum_kernel(x_ref, o_ref):
    rows, length = x_ref.shape
    r_i = lax.broadcasted_iota(jnp.int32, (V7X_LANES, V7X_LANES), 0)
    c_i = lax.broadcasted_iota(jnp.int32, (V7X_LANES, V7X_LANES), 1)
    tri = (r_i <= c_i).astype(BF16)
    carry = jnp.zeros((rows, 1), F32)
    for c in range(length // V7X_LANES):
        x = x_ref[:, c * V7X_LANES:(c + 1) * V7X_LANES]
        hi = x.astype(BF16)
        r1 = x - hi.astype(F32)
        mid = r1.astype(BF16)
        lo = (r1 - mid.astype(F32)).astype(BF16)
        blk = (jnp.dot(hi, tri, preferred_element_type=F32)
               + jnp.dot(mid, tri, preferred_element_type=F32)
               + jnp.dot(lo, tri, preferred_element_type=F32)) + carry
        o_ref[:, c * V7X_LANES:(c + 1) * V7X_LANES] = blk
        carry = blk[:, V7X_LANES - 1:V7X_LANES]


def _cumsum_lanes(x):
    return pl.pallas_call(
        _cumsum_kernel,
        out_shape=jax.ShapeDtypeStruct(x.shape, F32),
        name="logf_cumsum",
    )(x)


def _rel_bias_kernel(g_ref, raw_ref, band_ref):
    row = lax.broadcasted_iota(jnp.int32, (BAND_TQ, BAND_W), 0)
    col = lax.broadcasted_iota(jnp.int32, (BAND_TQ, BAND_W), 1)
    q_chunk = row // CHUNK
    k_chunk = col // CHUNK - N_PAST_CHUNKS
    in_band = jnp.logical_and(k_chunk >= q_chunk - N_PAST_CHUNKS, k_chunk <= q_chunk)
    for h in range(N_HEADS):
        rows = jnp.broadcast_to(g_ref[h:h + 1, :], (BAND_TQ, BAND_G))
        toeplitz = pltpu.roll(rows, 0, 1, stride=1, stride_axis=0)[:, :BAND_W] * LOG2E
        raw_ref[h] = toeplitz
        band_ref[h] = jnp.where(in_band, toeplitz, NEG)


def _rel_bias_tiles(table):
    far_past = table[:, 2 * REL_CLIP:]
    far_future = table[:, :1]
    near = jnp.flip(table[:, :2 * REL_CLIP], axis=1)
    n_past = A_REACH - REL_CLIP + 1
    n_future = BAND_W - n_past - near.shape[1]
    g = jnp.concatenate([jnp.broadcast_to(far_past, (N_HEADS, n_past)), near,
                         jnp.broadcast_to(far_future, (N_HEADS, n_future)),
                         jnp.broadcast_to(far_past, (N_HEADS, BAND_G - BAND_W))], axis=1)
    shape = jax.ShapeDtypeStruct((N_HEADS, BAND_TQ, BAND_W), F32)
    return pl.pallas_call(_rel_bias_kernel, out_shape=[shape, shape], name="rel_bias")(g.astype(F32))


def _mixers_prompt_kernel(qa_ref, ka_ref, va_ref, qb_ref, kb_ref, vb_ref, bias_ref, f_ref,
                          oa_ref, ob_ref, vpa_ref, vpb_ref):
    s_len = qa_ref.shape[0]
    row = lax.broadcasted_iota(jnp.int32, (FOX_T, FOX_T), 0)
    col = lax.broadcasted_iota(jnp.int32, (FOX_T, FOX_T), 1)
    causal = row >= col
    lanes = [slice(h * HEAD_DIM, (h + 1) * HEAD_DIM) for h in range(MIX_HEADS)]
    f2 = []
    for h, ls in enumerate(lanes):
        _widen_values(va_ref.at[:, ls], vpa_ref.at[h])
        _widen_values(vb_ref.at[:, ls], vpb_ref.at[h])
        f2.append(f_ref[h] * LOG2E)

    def band_logits(h, t):
        q0 = t * BAND_TQ
        k0 = max(q0 - A_REACH, 0)
        n_keys = q0 + BAND_TQ - k0
        s = lax.dot_general(qa_ref[q0:q0 + BAND_TQ, lanes[h]], ka_ref[k0:k0 + n_keys, lanes[h]], _NT,
                            preferred_element_type=F32)
        return [(s * SCALE2 + bias_ref[h, :, BAND_W - n_keys:], vpa_ref[h, k0:k0 + n_keys, :])]

    def fox_logits(h, qi):
        q0 = qi * FOX_T
        q = qb_ref[q0:q0 + FOX_T, lanes[h]]
        s_diag = (lax.dot_general(q, kb_ref[q0:q0 + FOX_T, lanes[h]], _NT, preferred_element_type=F32)
                  * SCALE2 - f2[h][:, q0:q0 + FOX_T])
        parts = [(jnp.where(causal, s_diag, NEG), vpb_ref[h, q0:q0 + FOX_T, :])]
        if qi:
            s_past = (lax.dot_general(q, kb_ref[0:q0, lanes[h]], _NT, preferred_element_type=F32) * SCALE2
                      - f2[h][:, 0:q0])
            parts.append((s_past, vpb_ref[h, 0:q0, :]))
        return parts

    n_band = s_len // BAND_TQ
    n_fox = s_len // FOX_T
    order = []
    for t in range(max(n_band, n_fox)):
        for h in range(MIX_HEADS):
            if t < n_band:
                order.append(("band", h, t))
            if t < n_fox:
                order.append(("fox", h, t))

    def logits(i):
        kind, h, t = order[i]
        return band_logits(h, t) if kind == "band" else fox_logits(h, t)

    def store(i, o):
        kind, h, t = order[i]
        if kind == "band":
            oa_ref[t * BAND_TQ:(t + 1) * BAND_TQ, lanes[h]] = o.astype(BF16)
        else:
            ob_ref[t * FOX_T:(t + 1) * FOX_T, lanes[h]] = o.astype(BF16)

    _attention_tiles(len(order), logits, store)


def _mixers_prompt(qkv, bias, f_rows, batch, s_len):
    t = batch * s_len
    groups = N_HEADS // MIX_HEADS
    width = MIX_HEADS * HEAD_DIM
    head = lambda seg: pl.BlockSpec((s_len, width), lambda b, g: (b, seg * groups + g))
    blocks = 8 * s_len * width * 2 + MIX_HEADS * (BAND_TQ * BAND_W * 4 + 8 * s_len * 4)
    scratch = 2 * MIX_HEADS * s_len * PV_COLS * 2
    out = jax.ShapeDtypeStruct((t, W_MIX), BF16)
    vp = pltpu.VMEM((MIX_HEADS, s_len, PV_COLS), BF16)
    return pl.pallas_call(
        _mixers_prompt_kernel,
        grid=(batch, groups),
        in_specs=[head(s) for s in range(6)] + [
            pl.BlockSpec((MIX_HEADS, BAND_TQ, BAND_W), lambda b, g: (g, 0, 0)),
            pl.BlockSpec((MIX_HEADS, 1, s_len), lambda b, g: (b * groups + g, 0, 0))],
        out_specs=[head(0), head(0)],
        out_shape=[out, out],
        scratch_shapes=[vp, vp],
        compiler_params=pltpu.CompilerParams(
            dimension_semantics=("parallel", "parallel"),
            vmem_limit_bytes=_vmem_limit(blocks, scratch, 8 * FOX_T * s_len * 4)),
        name="mixers_prompt",
    )(qkv, qkv, qkv, qkv, qkv, qkv, bias, f_rows)


def _band_sample_kernel(q_ref, kn_ref, vn_ref, kn32_ref, vn32_ref, ck_ref, cv_ref, bias_ref,
                        o_ref, sk_ref, sv_ref):
    n_cache = ck_ref.shape[1]
    t_new = q_ref.shape[0]
    k_heads = _head_major(ck_ref[0])
    v_heads = _head_major(cv_ref[0])
    for h in range(N_HEADS):
        hs = slice(h * HEAD_DIM, (h + 1) * HEAD_DIM)
        q = q_ref[:, hs]
        kc = k_heads[h].astype(BF16)
        vc = v_heads[h].astype(BF16)
        s1 = (lax.dot_general(q, kc, _NT, preferred_element_type=F32) * SCALE2
              + bias_ref[h, :t_new, :n_cache])
        s2 = (lax.dot_general(q, kn_ref[:, hs], _NT, preferred_element_type=F32) * SCALE2
              + bias_ref[h, :t_new, n_cache:n_cache + t_new])
        o_ref[:, hs] = _softmax_pv([(s1, vc), (s2, vn_ref[:, hs])]).astype(BF16)
    for cache_ref, new_ref, state_ref in ((ck_ref, kn32_ref, sk_ref), (cv_ref, vn32_ref, sv_ref)):
        state_ref[0, :n_cache - t_new] = cache_ref[0, t_new:]
        state_ref[0, n_cache - t_new:] = new_ref[...]


def _band_sample(qkv, ka32, va32, cache_k, cache_v, bias, batch, t_new):
    n_cache = cache_k.shape[1]
    cache_block = (1, n_cache, N_HEADS, HEAD_DIM)
    new_block = (t_new, N_HEADS, HEAD_DIM)
    blocks = 4 * t_new * W_MIX * 2 + 2 * t_new * W_MIX * 4 + 4 * n_cache * W_MIX * 4 + bias.size * 4
    return pl.pallas_call(
        _band_sample_kernel,
        grid=(batch,),
        in_specs=[
            pl.BlockSpec((t_new, W_MIX), lambda b: (b, 0)),
            pl.BlockSpec((t_new, W_MIX), lambda b: (b, 1)),
            pl.BlockSpec((t_new, W_MIX), lambda b: (b, 2)),
            pl.BlockSpec(new_block, lambda b: (b, 0, 0)),
            pl.BlockSpec(new_block, lambda b: (b, 0, 0)),
            pl.BlockSpec(cache_block, lambda b: (b, 0, 0, 0)),
            pl.BlockSpec(cache_block, lambda b: (b, 0, 0, 0)),
            pl.BlockSpec(bias.shape, lambda b: (0, 0, 0)),
        ],
        out_specs=[
            pl.BlockSpec((t_new, W_MIX), lambda b: (b, 0)),
            pl.BlockSpec(cache_block, lambda b: (b, 0, 0, 0)),
            pl.BlockSpec(cache_block, lambda b: (b, 0, 0, 0)),
        ],
        out_shape=[
            jax.ShapeDtypeStruct((batch * t_new, W_MIX), BF16),
            jax.ShapeDtypeStruct(cache_k.shape, F32),
            jax.ShapeDtypeStruct(cache_v.shape, F32),
        ],
        compiler_params=pltpu.CompilerParams(
            dimension_semantics=("parallel",),
            vmem_limit_bytes=_vmem_limit(blocks, 0, 4 * MIB)),
        name="band_sample",
    )(qkv, qkv, qkv, ka32, va32, cache_k, cache_v, bias)


def _fox_sample_kernel(q_ref, kn_ref, vn_ref, ck_ref, cv_ref, f_ref, o_ref):
    n_cache = ck_ref.shape[1]
    t_new = q_ref.shape[0]
    row = lax.broadcasted_iota(jnp.int32, (t_new, t_new), 0)
    col = lax.broadcasted_iota(jnp.int32, (t_new, t_new), 1)
    causal = row >= col
    k_heads = _head_major(ck_ref[0])
    v_heads = _head_major(cv_ref[0])
    for h in range(N_HEADS):
        hs = slice(h * HEAD_DIM, (h + 1) * HEAD_DIM)
        q = q_ref[:, hs]
        f2 = f_ref[0, h:h + 1, :] * LOG2E
        kc = k_heads[h].astype(BF16)
        vc = v_heads[h].astype(BF16)
        s1 = lax.dot_general(q, kc, _NT, preferred_element_type=F32) * SCALE2 - f2[:, :n_cache]
        s2 = (lax.dot_general(q, kn_ref[:, hs], _NT, preferred_element_type=F32) * SCALE2
              - f2[:, n_cache:n_cache + t_new])
        parts = [(s1, vc), (jnp.where(causal, s2, NEG), vn_ref[:, hs])]
        o_ref[:, hs] = _softmax_pv(parts).astype(BF16)


def _fox_sample(qkv, cache_k, cache_v, f_rows, batch, t_new):
    n_cache = cache_k.shape[1]
    f_len = f_rows.shape[-1]
    cache_block = (1, n_cache, N_HEADS, HEAD_DIM)
    blocks = 4 * t_new * W_MIX * 2 + 2 * n_cache * W_MIX * 4 + N_HEADS * f_len * 4
    return pl.pallas_call(
        _fox_sample_kernel,
        grid=(batch,),
        in_specs=[
            pl.BlockSpec((t_new, W_MIX), lambda b: (b, 3)),
            pl.BlockSpec((t_new, W_MIX), lambda b: (b, 4)),
            pl.BlockSpec((t_new, W_MIX), lambda b: (b, 5)),
            pl.BlockSpec(cache_block, lambda b: (b, 0, 0, 0)),
            pl.BlockSpec(cache_block, lambda b: (b, 0, 0, 0)),
            pl.BlockSpec((1, N_HEADS, f_len), lambda b: (b, 0, 0)),
        ],
        out_specs=pl.BlockSpec((t_new, W_MIX), lambda b: (b, 0)),
        out_shape=jax.ShapeDtypeStruct((batch * t_new, W_MIX), BF16),
        compiler_params=pltpu.CompilerParams(
            dimension_semantics=("parallel",),
            vmem_limit_bytes=_vmem_limit(blocks, 0, 8 * MIB)),
        name="fox_sample",
    )(qkv, qkv, qkv, cache_k, cache_v, f_rows)


def _merge_kernel(oa_ref, ob_ref, gate_ref, x_ref, wa_ref, wb_ref, wo_ref, h_ref):
    ya = jnp.dot(oa_ref[...], wa_ref[...], preferred_element_type=F32)
    yb = jnp.dot(ob_ref[...], wb_ref[...], preferred_element_type=F32)
    m = (gate_ref[:, :D_MODEL].astype(F32) * ya + gate_ref[:, D_MODEL:].astype(F32) * yb)
    h_ref[...] = x_ref[...] + jnp.dot(m.astype(BF16), wo_ref[...], preferred_element_type=F32)


def _resident(shape):
    return pl.BlockSpec(shape, lambda *_: (0,) * len(shape), pipeline_mode=pl.Buffered(1))


def _merge(oa, ob, gates, x, w_a, w_b, w_o, *, tm):
    t = x.shape[0]
    blocks = 2 * tm * W_MIX * 2 + tm * 2 * D_MODEL * 2 + 2 * tm * D_MODEL * 4
    weights = (2 * W_MIX * D_MODEL + D_MODEL * D_MODEL) * 2
    return pl.pallas_call(
        _merge_kernel,
        grid=(t // tm,),
        in_specs=[
            pl.BlockSpec((tm, W_MIX), lambda i: (i, 0)),
            pl.BlockSpec((tm, W_MIX), lambda i: (i, 0)),
            pl.BlockSpec((tm, 2 * D_MODEL), lambda i: (i, 0)),
            pl.BlockSpec((tm, D_MODEL), lambda i: (i, 0)),
            _resident((W_MIX, D_MODEL)),
            _resident((W_MIX, D_MODEL)),
            _resident((D_MODEL, D_MODEL)),
        ],
        out_specs=pl.BlockSpec((tm, D_MODEL), lambda i: (i, 0)),
        out_shape=jax.ShapeDtypeStruct((t, D_MODEL), F32),
        compiler_params=pltpu.CompilerParams(
            dimension_semantics=("parallel",),
            vmem_limit_bytes=_vmem_limit(blocks, weights, 4 * tm * D_MODEL * 4)),
        name="merge",
    )(oa, ob, gates, x, w_a, w_b, w_o)


def _ffn_kernel(h_ref, g_ref, wu_ref, wd_ref, o_ref, n_ref):
    tf = wu_ref.shape[1]
    chunks = [slice(c * FFN_CHUNK, (c + 1) * FFN_CHUNK) for c in range(tf // FFN_CHUNK)]

    def hidden(c):
        a = jnp.dot(n_ref[...], wu_ref[:, c], preferred_element_type=F32)
        r = jnp.square(jnp.maximum(a, 0.0)).astype(BF16)
        return jnp.dot(r, wd_ref[c, :], preferred_element_type=F32)

    @pl.when(pl.program_id(1) == 0)
    def _():
        _norm_to_bf16(h_ref, g_ref, n_ref)
        o_ref[...] = h_ref[...] + hidden(chunks[0])

    @pl.when(pl.program_id(1) != 0)
    def _():
        o_ref[...] += hidden(chunks[0])

    for c in chunks[1:]:
        o_ref[...] += hidden(c)


def _ffn(h, g, w_up, w_down, *, tm, tf):
    t = h.shape[0]
    blocks = 2 * tm * D_MODEL * 4 + 2 * D_MODEL * tf * 2
    scratch = tm * D_MODEL * 2
    return pl.pallas_call(
        _ffn_kernel,
        grid=(t // tm, D_FF // tf),
        in_specs=[
            pl.BlockSpec((tm, D_MODEL), lambda i, f: (i, 0)),
            pl.BlockSpec((1, D_MODEL), lambda i, f: (0, 0)),
            pl.BlockSpec((D_MODEL, tf), lambda i, f: (0, f)),
            pl.BlockSpec((tf, D_MODEL), lambda i, f: (f, 0)),
        ],
        out_specs=pl.BlockSpec((tm, D_MODEL), lambda i, f: (i, 0)),
        out_shape=jax.ShapeDtypeStruct((t, D_MODEL), F32),
        scratch_shapes=[pltpu.VMEM((tm, D_MODEL), BF16)],
        compiler_params=pltpu.CompilerParams(
            dimension_semantics=("parallel", "arbitrary"),
            vmem_limit_bytes=_vmem_limit(blocks, scratch, 2 * tm * tf * 4)),
        name="ffn",
    )(h, g, w_up, w_down)


def _ple_kernel(h_ref, p_ref, gp_ref, gf_ref, wg_ref, wp_ref, y_ref):
    h = h_ref[...]
    n = _rmsnorm_rows(h, gp_ref[...]).astype(BF16)
    gate = _sigmoid(jnp.dot(n, wg_ref[...], preferred_element_type=F32))
    proj = jnp.dot(p_ref[...].astype(BF16), wp_ref[...], preferred_element_type=F32)
    y_ref[...] = _rmsnorm_rows(h + proj * gate, gf_ref[...])


def _ple_final(h, p, g_ple, g_final, w_gate, w_proj, *, tm):
    t = h.shape[0]
    blocks = 2 * tm * D_MODEL * 4 + tm * D_PLE * 4
    weights = (D_MODEL * D_MODEL + D_PLE * D_MODEL) * 2
    return pl.pallas_call(
        _ple_kernel,
        grid=(t // tm,),
        in_specs=[
            pl.BlockSpec((tm, D_MODEL), lambda i: (i, 0)),
            pl.BlockSpec((tm, D_PLE), lambda i: (i, 0)),
            pl.BlockSpec((1, D_MODEL), lambda i: (0, 0)),
            pl.BlockSpec((1, D_MODEL), lambda i: (0, 0)),
            _resident((D_MODEL, D_MODEL)),
            _resident((D_PLE, D_MODEL)),
        ],
        out_specs=pl.BlockSpec((tm, D_MODEL), lambda i: (i, 0)),
        out_shape=jax.ShapeDtypeStruct((t, D_MODEL), F32),
        compiler_params=pltpu.CompilerParams(
            dimension_semantics=("parallel",),
            vmem_limit_bytes=_vmem_limit(blocks, weights, 4 * tm * D_MODEL * 4)),
        name="ple_final",
    )(h, p, g_ple, g_final, w_gate, w_proj)


def _pad_lanes(x, multiple):
    pad = (-x.shape[-1]) % multiple
    return jnp.pad(x, ((0, 0),) * (x.ndim - 1) + ((0, pad),)) if pad else x


def _token_tile(t, preferred):
    return preferred if t % preferred == 0 else t


def kernel(x_prompt, x_sample, p_prompt, p_sample, cache_a_k, cache_a_v, cache_b_k, cache_b_v,
           cache_b_logf, g_mix, w_in, b_f, rel_bias, w_a_proj, w_b_proj, w_o, g_mlp, w_up, w_down,
           g_ple, w_ple_gate, w_ple_proj, g_final):
    depth = w_in.shape[0]
    assert depth == 1, "single-layer step"
    batch, s_len, _ = x_prompt.shape
    dec_batch, t_new, _ = x_sample.shape
    n_cache_a = cache_a_k.shape[2]
    assert s_len % FOX_T == 0 and s_len >= A_REACH and t_new == CHUNK and n_cache_a == A_REACH

    w = w_in[0]
    n_qkv = 6 * SEG_COLS
    w_t = jnp.swapaxes(w, 0, 1).astype(BF16)
    w_g = w_t[n_qkv + N_HEADS:]
    w_f = _pad_lanes(w[:, n_qkv:n_qkv + N_HEADS], V7X_LANES).astype(BF16)
    b_f_row = _pad_lanes(b_f[0][None, :].astype(F32), V7X_LANES)
    g_mix_row = g_mix[0][None, :].astype(F32)
    g_mlp_row = g_mlp[0][None, :].astype(F32)
    g_ple_row = g_ple[0][None, :].astype(F32)
    g_final_row = g_final[None, :].astype(F32)
    w_a = w_a_proj[0].astype(BF16)
    w_b = w_b_proj[0].astype(BF16)
    w_o_b = w_o[0].astype(BF16)
    w_up_b = w_up[0].astype(BF16)
    w_down_b = w_down[0].astype(BF16)
    w_pg = w_ple_gate[0].astype(BF16)
    w_pp = w_ple_proj[0].astype(BF16)
    bias_raw, bias_band = _rel_bias_tiles(rel_bias[0])

    def project(x2d, *, a_rows, a_period, tm):
        gates, lf, n = _in_proj_g(x2d, g_mix_row, w_g, w_f, b_f_row, tm=tm)
        return (gates, lf) + tuple(_in_proj_qkv(n, w_t, a_rows=a_rows, a_period=a_period, tm=tm))

    def finish(x2d, p2d, oa, ob, gates):
        t = x2d.shape[0]
        h1 = _merge(oa, ob, gates, x2d, w_a, w_b, w_o_b, tm=_token_tile(t, 256))
        h2 = _ffn(h1, g_mlp_row, w_up_b, w_down_b, tm=_token_tile(t, 1024), tf=1024)
        return _ple_final(h2, p2d, g_ple_row, g_final_row, w_pg, w_pp, tm=_token_tile(t, 512))

    tp = batch * s_len
    xp = x_prompt.reshape(tp, D_MODEL)
    tm_p = _token_tile(s_len, 1024)
    gates_p, lf_p, qkv_p, ka_p, va_p, kb_p, vb_p = project(
        xp, a_rows=min(A_REACH, tm_p), a_period=s_len // tm_p, tm=tm_p)
    lf_rows = lf_p.reshape(batch, s_len, N_HEADS).transpose(0, 2, 1).reshape(batch * N_HEADS, s_len)
    f_p = _cumsum_lanes(lf_rows).reshape(batch * N_HEADS, 1, s_len)
    oa_p, ob_p = _mixers_prompt(qkv_p, bias_band, f_p, batch, s_len)
    y_prompt = finish(xp, p_prompt[0].reshape(tp, D_PLE), oa_p, ob_p, gates_p)

    ts = dec_batch * t_new
    xs = x_sample.reshape(ts, D_MODEL)
    gates_s, lf_s, qkv_s, ka_s, va_s, kb_s, vb_s = project(xs, a_rows=ts, a_period=1, tm=ts)
    lf_new = lf_s.reshape(dec_batch, t_new, N_HEADS)
    lf_all = jnp.concatenate([cache_b_logf[0].astype(F32), lf_new], axis=1)
    lf_all = _pad_lanes(lf_all.transpose(0, 2, 1), V7X_LANES)
    f_s = _cumsum_lanes(lf_all.reshape(dec_batch * N_HEADS, -1)).reshape(dec_batch, N_HEADS, -1)
    oa_s, sk_s, sv_s = _band_sample(qkv_s, ka_s, va_s, cache_a_k[0], cache_a_v[0], bias_raw,
                                    dec_batch, t_new)
    ob_s = _fox_sample(qkv_s, cache_b_k[0], cache_b_v[0], f_s, dec_batch, t_new)
    y_sample = finish(xs, p_sample[0].reshape(ts, D_PLE), oa_s, ob_s, gates_s)

    def state(a, b):
        return a.reshape(1, b, -1, N_HEADS, HEAD_DIM)

    return (y_prompt.reshape(batch, s_len, D_MODEL),
            y_sample.reshape(dec_batch, t_new, D_MODEL),
            state(ka_p, batch), state(va_p, batch), state(kb_p, batch), state(vb_p, batch),
            lf_p.reshape(1, batch, s_len, N_HEADS),
            sk_s[None], sv_s[None],
            state(kb_s, dec_batch), state(vb_s, dec_batch),
            lf_new[None])
```

```python
import functools
import math

import jax
import jax.numpy as jnp
from jax import lax
from jax.experimental import pallas as pl
from jax.experimental.pallas import tpu as pltpu

F32 = jnp.float32
BF16 = jnp.bfloat16

D_MODEL = 2048
CHUNK = 64
N_PAST_CHUNKS = 8
A_REACH = N_PAST_CHUNKS * CHUNK
HEAD_DIM = 128
N_HEADS = 8
W_MIX = N_HEADS * HEAD_DIM
REL_CLIP = 128
D_FF = 4 * D_MODEL
D_PLE = 256
RMS_EPS = 1e-6
SCALE = HEAD_DIM ** -0.5
NEG = -1e30
LOG2E = math.log2(math.e)
SCALE2 = SCALE * LOG2E

V7X_LANES = 128
V7X_VMEM_LIMIT_CAP = 60 * 1024 * 1024
MIB = 1024 * 1024

NORM_ROWS = 128
BAND_TQ = 256
BAND_W = A_REACH + BAND_TQ
BAND_G = BAND_W + BAND_TQ
PV_COLS = 2 * HEAD_DIM
FOX_T = 256
MIX_HEADS = 2
GATE_COLS = 2048
SEG_COLS = 1024
FFN_CHUNK = 1024

_NT = (((1,), (1,)), ((), ()))


def _vmem_limit(block_bytes, scratch_bytes, temp_bytes):
    est = 2 * block_bytes + scratch_bytes + temp_bytes
    return int(min(max(est, 16 * MIB), V7X_VMEM_LIMIT_CAP))


def _sigmoid(x):
    return 0.5 * jnp.tanh(0.5 * x) + 0.5


def _rmsnorm_rows(x, g):
    ms = jnp.mean(x * x, axis=-1, keepdims=True)
    return (x * lax.rsqrt(ms + RMS_EPS)) * g


def _norm_to_bf16(x_ref, g_ref, n_ref):
    rows = min(NORM_ROWS, x_ref.shape[0])

    def body(c, carry):
        r = pl.ds(pl.multiple_of(c * rows, rows), rows)
        n_ref[r, :] = _rmsnorm_rows(x_ref[r, :], g_ref[...]).astype(BF16)
        return carry

    lax.fori_loop(0, x_ref.shape[0] // rows, body, 0)


def _softmax_pv(parts):
    m = functools.reduce(jnp.maximum, [jnp.max(s, axis=-1, keepdims=True) for s, _ in parts])
    ps = [jnp.exp2(s - m) for s, _ in parts]
    l = functools.reduce(jnp.add, [jnp.sum(p, axis=-1, keepdims=True) for p in ps])
    o = functools.reduce(jnp.add, [jnp.dot(p.astype(BF16), v, preferred_element_type=F32)
                                   for p, (_, v) in zip(ps, parts)])
    return o / l


def _widen_values(v_ref, vp_ref):
    lane = lax.broadcasted_iota(jnp.int32, v_ref.shape, 1)
    vp_ref[:, :HEAD_DIM] = v_ref[...]
    vp_ref[:, HEAD_DIM:] = jnp.where(lane == 0, 1.0, 0.0).astype(BF16)


def _softmax_weights(parts):
    m = functools.reduce(jnp.maximum, [jnp.max(s, axis=-1, keepdims=True) for s, _ in parts])
    return [(jnp.exp2(s - m).astype(BF16), v) for s, v in parts]


def _weighted_values_wide(weighted):
    o = functools.reduce(jnp.add, [jnp.dot(p, v, preferred_element_type=F32) for p, v in weighted])
    return o[:, :HEAD_DIM] / o[:, HEAD_DIM:HEAD_DIM + 1]


def _attention_tiles(n_tiles, logits, store):
    stage1 = {0: logits(0)}
    if n_tiles > 1:
        stage1[1] = logits(1)
    stage2 = {0: _softmax_weights(stage1.pop(0))}
    for t in range(n_tiles):
        if t + 2 < n_tiles:
            stage1[t + 2] = logits(t + 2)
        if t + 1 < n_tiles:
            stage2[t + 1] = _softmax_weights(stage1.pop(t + 1))
        store(t, _weighted_values_wide(stage2.pop(t)))


def _store_heads(ref, rows):
    stacked = jnp.stack([rows[:, h * HEAD_DIM:(h + 1) * HEAD_DIM] for h in range(N_HEADS)], axis=0)
    ref[...] = jnp.swapaxes(stacked, 0, 1)


def _head_major(rows):
    return jnp.swapaxes(rows, 0, 1)


def _inproj_g_kernel(x_ref, g_ref, w_ref, wf_ref, bf_ref, gate_ref, lf_ref, n_ref):
    @pl.when(pl.program_id(1) == 0)
    def _():
        _norm_to_bf16(x_ref, g_ref, n_ref)
        z = jnp.dot(n_ref[...], wf_ref[...], preferred_element_type=F32) + bf_ref[...]
        lf = jnp.minimum(z, 0.0) - jnp.log1p(jnp.exp(-jnp.abs(z)))
        lf_ref[...] = lf[:, :N_HEADS]

    a = lax.dot_general(n_ref[...], w_ref[...], _NT, preferred_element_type=F32)
    gate_ref[...] = _sigmoid(a).astype(BF16)


def _in_proj_g(x, g, w_g, w_f, b_f, *, tm):
    t = x.shape[0]
    tn = GATE_COLS
    blocks = (tm * D_MODEL * 4 + D_MODEL * tn * 2 + D_MODEL * V7X_LANES * 2
              + tm * tn * 2 + tm * V7X_LANES * 4 + tm * D_MODEL * 2)
    return pl.pallas_call(
        _inproj_g_kernel,
        grid=(t // tm, 4 * SEG_COLS // GATE_COLS),
        in_specs=[
            pl.BlockSpec((tm, D_MODEL), lambda i, j: (i, 0)),
            pl.BlockSpec((1, D_MODEL), lambda i, j: (0, 0)),
            pl.BlockSpec((tn, D_MODEL), lambda i, j: (j, 0)),
            pl.BlockSpec((D_MODEL, V7X_LANES), lambda i, j: (0, 0)),
            pl.BlockSpec((1, V7X_LANES), lambda i, j: (0, 0)),
        ],
        out_specs=[
            pl.BlockSpec((tm, tn), lambda i, j: (i, j)),
            pl.BlockSpec((tm, N_HEADS), lambda i, j: (i, 0)),
            pl.BlockSpec((tm, D_MODEL), lambda i, j: (i, 0)),
        ],
        out_shape=[
            jax.ShapeDtypeStruct((t, 4 * SEG_COLS), BF16),
            jax.ShapeDtypeStruct((t, N_HEADS), F32),
            jax.ShapeDtypeStruct((t, D_MODEL), BF16),
        ],
        compiler_params=pltpu.CompilerParams(
            dimension_semantics=("parallel", "arbitrary"),
            vmem_limit_bytes=_vmem_limit(blocks, 0, 10 * MIB)),
        name="in_proj_g",
    )(x, g, w_g, w_f, b_f)


def _inproj_qkv_kernel(n_ref, w_ref, qkv_ref, ka_ref, va_ref, kb_ref, vb_ref, *, a_rows, a_period):
    i = pl.program_id(0)
    j = pl.program_id(1)
    tm = n_ref.shape[0]
    is_tail = (i % a_period) == (a_period - 1)
    is_a_kv = jnp.logical_or(j == 1, j == 2)

    def project():
        return lax.dot_general(n_ref[...], w_ref[...], _NT, preferred_element_type=F32)

    plain = jnp.logical_or(jnp.logical_or(j == 0, j == 3),
                           jnp.logical_and(is_a_kv, jnp.logical_not(is_tail)))

    @pl.when(plain)
    def _():
        qkv_ref[...] = project().astype(BF16)

    for col, tail_ref in ((1, ka_ref), (2, va_ref)):
        @pl.when(jnp.logical_and(j == col, is_tail))
        def _(tail_ref=tail_ref):
            r = project()
            qkv_ref[...] = r.astype(BF16)
            _store_heads(tail_ref, r[tm - a_rows:, :])

    for col, full_ref in ((4, kb_ref), (5, vb_ref)):
        @pl.when(j == col)
        def _(full_ref=full_ref):
            r = project()
            qkv_ref[...] = r.astype(BF16)
            _store_heads(full_ref, r)


def _in_proj_qkv(n, w_t, *, a_rows, a_period, tm):
    t = n.shape[0]
    tn = SEG_COLS
    n_a = (t // tm) // a_period * a_rows
    blocks = (tm * D_MODEL * 2 + D_MODEL * tn * 2 + tm * tn * 2
              + 2 * a_rows * tn * 4 + 2 * tm * tn * 4)
    kern = functools.partial(_inproj_qkv_kernel, a_rows=a_rows, a_period=a_period)
    head_block = lambda rows: (rows, N_HEADS, HEAD_DIM)
    return pl.pallas_call(
        kern,
        grid=(t // tm, 6),
        in_specs=[
            pl.BlockSpec((tm, D_MODEL), lambda i, j: (i, 0)),
            pl.BlockSpec((tn, D_MODEL), lambda i, j: (j, 0)),
        ],
        out_specs=[
            pl.BlockSpec((tm, tn), lambda i, j: (i, j)),
            pl.BlockSpec(head_block(a_rows), lambda i, j: (i // a_period, 0, 0)),
            pl.BlockSpec(head_block(a_rows), lambda i, j: (i // a_period, 0, 0)),
            pl.BlockSpec(head_block(tm), lambda i, j: (i, 0, 0)),
            pl.BlockSpec(head_block(tm), lambda i, j: (i, 0, 0)),
        ],
        out_shape=[
            jax.ShapeDtypeStruct((t, 6 * SEG_COLS), BF16),
            jax.ShapeDtypeStruct(head_block(n_a), F32),
            jax.ShapeDtypeStruct(head_block(n_a), F32),
            jax.ShapeDtypeStruct(head_block(t), F32),
            jax.ShapeDtypeStruct(head_block(t), F32),
        ],
        compiler_params=pltpu.CompilerParams(
            dimension_semantics=("parallel", "arbitrary"),
            vmem_limit_bytes=_vmem_limit(blocks, 0, 6 * MIB)),
        name="in_proj_qkv",
    )(n, w_t)


def _cumsum_kernel(x_ref, o_ref):
    rows, length = x_ref.shape
    r_i = lax.broadcasted_iota(jnp.int32, (V7X_LANES, V7X_LANES), 0)
    c_i = lax.broadcasted_iota(jnp.int32, (V7X_LANES, V7X_LANES), 1)
    tri = (r_i <= c_i).astype(BF16)
    carry = jnp.zeros((rows, 1), F32)
    for c in range(length // V7X_LANES):
        x = x_ref[:, c * V7X_LANES:(c + 1) * V7X_LANES]
        hi = x.astype(BF16)
        r1 = x - hi.astype(F32)
        mid = r1.astype(BF16)
        lo = (r1 - mid.astype(F32)).astype(BF16)
        blk = (jnp.dot(hi, tri, preferred_element_type=F32)
               + jnp.dot(mid, tri, preferred_element_type=F32)
               + jnp.dot(lo, tri, preferred_element_type=F32)) + carry
        o_ref[:, c * V7X_LANES:(c + 1) * V7X_LANES] = blk
        carry = blk[:, V7X_LANES - 1:V7X_LANES]


def _cumsum_lanes(x):
    return pl.pallas_call(
        _cumsum_kernel,
        out_shape=jax.ShapeDtypeStruct(x.shape, F32),
        name="logf_cumsum",
    )(x)


def _rel_bias_kernel(g_ref, raw_ref, band_ref):
    row = lax.broadcasted_iota(jnp.int32, (BAND_TQ, BAND_W), 0)
    col = lax.broadcasted_iota(jnp.int32, (BAND_TQ, BAND_W), 1)
    q_chunk = row // CHUNK
    k_chunk = col // CHUNK - N_PAST_CHUNKS
    in_band = jnp.logical_and(k_chunk >= q_chunk - N_PAST_CHUNKS, k_chunk <= q_chunk)
    for h in range(N_HEADS):
        rows = jnp.broadcast_to(g_ref[h:h + 1, :], (BAND_TQ, BAND_G))
        toeplitz = pltpu.roll(rows, 0, 1, stride=1, stride_axis=0)[:, :BAND_W] * LOG2E
        raw_ref[h] = toeplitz
        band_ref[h] = jnp.where(in_band, toeplitz, NEG)


def _rel_bias_tiles(table):
    far_past = table[:, 2 * REL_CLIP:]
    far_future = table[:, :1]
    near = jnp.flip(table[:, :2 * REL_CLIP], axis=1)
    n_past = A_REACH - REL_CLIP + 1
    n_future = BAND_W - n_past - near.shape[1]
    g = jnp.concatenate([jnp.broadcast_to(far_past, (N_HEADS, n_past)), near,
                         jnp.broadcast_to(far_future, (N_HEADS, n_future)),
                         jnp.broadcast_to(far_past, (N_HEADS, BAND_G - BAND_W))], axis=1)
    shape = jax.ShapeDtypeStruct((N_HEADS, BAND_TQ, BAND_W), F32)
    return pl.pallas_call(_rel_bias_kernel, out_shape=[shape, shape], name="rel_bias")(g.astype(F32))


def _mixers_prompt_kernel(qa_ref, ka_ref, va_ref, qb_ref, kb_ref, vb_ref, bias_ref, f_ref,
                          oa_ref, ob_ref, vpa_ref, vpb_ref):
    s_len = qa_ref.shape[0]
    row = lax.broadcasted_iota(jnp.int32, (FOX_T, FOX_T), 0)
    col = lax.broadcasted_iota(jnp.int32, (FOX_T, FOX_T), 1)
    causal = row >= col
    lanes = [slice(h * HEAD_DIM, (h + 1) * HEAD_DIM) for h in range(MIX_HEADS)]
    f2 = []
    for h, ls in enumerate(lanes):
        _widen_values(va_ref.at[:, ls], vpa_ref.at[h])
        _widen_values(vb_ref.at[:, ls], vpb_ref.at[h])
        f2.append(f_ref[h] * LOG2E)

    def band_logits(h, t):
        q0 = t * BAND_TQ
        k0 = max(q0 - A_REACH, 0)
        n_keys = q0 + BAND_TQ - k0
        s = lax.dot_general(qa_ref[q0:q0 + BAND_TQ, lanes[h]], ka_ref[k0:k0 + n_keys, lanes[h]], _NT,
                            preferred_element_type=F32)
        return [(s * SCALE2 + bias_ref[h, :, BAND_W - n_keys:], vpa_ref[h, k0:k0 + n_keys, :])]

    def fox_logits(h, qi):
        q0 = qi * FOX_T
        q = qb_ref[q0:q0 + FOX_T, lanes[h]]
        s_diag = (lax.dot_general(q, kb_ref[q0:q0 + FOX_T, lanes[h]], _NT, preferred_element_type=F32)
                  * SCALE2 - f2[h][:, q0:q0 + FOX_T])
        parts = [(jnp.where(causal, s_diag, NEG), vpb_ref[h, q0:q0 + FOX_T, :])]
        if qi:
            s_past = (lax.dot_general(q, kb_ref[0:q0, lanes[h]], _NT, preferred_element_type=F32) * SCALE2
                      - f2[h][:, 0:q0])
            parts.append((s_past, vpb_ref[h, 0:q0, :]))
        return parts

    n_band = s_len // BAND_TQ
    n_fox = s_len // FOX_T
    order = []
    for t in range(max(n_band, n_fox)):
        for h in range(MIX_HEADS):
            if t < n_band:
                order.append(("band", h, t))
            if t < n_fox:
                order.append(("fox", h, t))

    def logits(i):
        kind, h, t = order[i]
        return band_logits(h, t) if kind == "band" else fox_logits(h, t)

    def store(i, o):
        kind, h, t = order[i]
        if kind == "band":
            oa_ref[t * BAND_TQ:(t + 1) * BAND_TQ, lanes[h]] = o.astype(BF16)
        else:
            ob_ref[t * FOX_T:(t + 1) * FOX_T, lanes[h]] = o.astype(BF16)

    _attention_tiles(len(order), logits, store)


def _mixers_prompt(qkv, bias, f_rows, batch, s_len):
    t = batch * s_len
    groups = N_HEADS // MIX_HEADS
    width = MIX_HEADS * HEAD_DIM
    head = lambda seg: pl.BlockSpec((s_len, width), lambda b, g: (b, seg * groups + g))
    blocks = 8 * s_len * width * 2 + MIX_HEADS * (BAND_TQ * BAND_W * 4 + 8 * s_len * 4)
    scratch = 2 * MIX_HEADS * s_len * PV_COLS * 2
    out = jax.ShapeDtypeStruct((t, W_MIX), BF16)
    vp = pltpu.VMEM((MIX_HEADS, s_len, PV_COLS), BF16)
    return pl.pallas_call(
        _mixers_prompt_kernel,
        grid=(batch, groups),
        in_specs=[head(s) for s in range(6)] + [
            pl.BlockSpec((MIX_HEADS, BAND_TQ, BAND_W), lambda b, g: (g, 0, 0)),
            pl.BlockSpec((MIX_HEADS, 1, s_len), lambda b, g: (b * groups + g, 0, 0))],
        out_specs=[head(0), head(0)],
        out_shape=[out, out],
        scratch_shapes=[vp, vp],
        compiler_params=pltpu.CompilerParams(
            dimension_semantics=("parallel", "parallel"),
            vmem_limit_bytes=_vmem_limit(blocks, scratch, 8 * FOX_T * s_len * 4)),
        name="mixers_prompt",
    )(qkv, qkv, qkv, qkv, qkv, qkv, bias, f_rows)


def _band_sample_kernel(q_ref, kn_ref, vn_ref, kn32_ref, vn32_ref, ck_ref, cv_ref, bias_ref,
                        o_ref, sk_ref, sv_ref):
    n_cache = ck_ref.shape[1]
    t_new = q_ref.shape[0]
    k_heads = _head_major(ck_ref[0])
    v_heads = _head_major(cv_ref[0])
    for h in range(N_HEADS):
        hs = slice(h * HEAD_DIM, (h + 1) * HEAD_DIM)
        q = q_ref[:, hs]
        kc = k_heads[h].astype(BF16)
        vc = v_heads[h].astype(BF16)
        s1 = (lax.dot_general(q, kc, _NT, preferred_element_type=F32) * SCALE2
              + bias_ref[h, :t_new, :n_cache])
        s2 = (lax.dot_general(q, kn_ref[:, hs], _NT, preferred_element_type=F32) * SCALE2
              + bias_ref[h, :t_new, n_cache:n_cache + t_new])
        o_ref[:, hs] = _softmax_pv([(s1, vc), (s2, vn_ref[:, hs])]).astype(BF16)
    for cache_ref, new_ref, state_ref in ((ck_ref, kn32_ref, sk_ref), (cv_ref, vn32_ref, sv_ref)):
        state_ref[0, :n_cache - t_new] = cache_ref[0, t_new:]
        state_ref[0, n_cache - t_new:] = new_ref[...]


def _band_sample(qkv, ka32, va32, cache_k, cache_v, bias, batch, t_new):
    n_cache = cache_k.shape[1]
    cache_block = (1, n_cache, N_HEADS, HEAD_DIM)
    new_block = (t_new, N_HEADS, HEAD_DIM)
    blocks = 4 * t_new * W_MIX * 2 + 2 * t_new * W_MIX * 4 + 4 * n_cache * W_MIX * 4 + bias.size * 4
    return pl.pallas_call(
        _band_sample_kernel,
        grid=(batch,),
        in_specs=[
            pl.BlockSpec((t_new, W_MIX), lambda b: (b, 0)),
            pl.BlockSpec((t_new, W_MIX), lambda b: (b, 1)),
            pl.BlockSpec((t_new, W_MIX), lambda b: (b, 2)),
            pl.BlockSpec(new_block, lambda b: (b, 0, 0)),
            pl.BlockSpec(new_block, lambda b: (b, 0, 0)),
            pl.BlockSpec(cache_block, lambda b: (b, 0, 0, 0)),
            pl.BlockSpec(cache_block, lambda b: (b, 0, 0, 0)),
            pl.BlockSpec(bias.shape, lambda b: (0, 0, 0)),
        ],
        out_specs=[
            pl.BlockSpec((t_new, W_MIX), lambda b: (b, 0)),
            pl.BlockSpec(cache_block, lambda b: (b, 0, 0, 0)),
            pl.BlockSpec(cache_block, lambda b: (b, 0, 0, 0)),
        ],
        out_shape=[
            jax.ShapeDtypeStruct((batch * t_new, W_MIX), BF16),
            jax.ShapeDtypeStruct(cache_k.shape, F32),
            jax.ShapeDtypeStruct(cache_v.shape, F32),
        ],
        compiler_params=pltpu.CompilerParams(
            dimension_semantics=("parallel",),
            vmem_limit_bytes=_vmem_limit(blocks, 0, 4 * MIB)),
        name="band_sample",
    )(qkv, qkv, qkv, ka32, va32, cache_k, cache_v, bias)


def _fox_sample_kernel(q_ref, kn_ref, vn_ref, ck_ref, cv_ref, f_ref, o_ref):
    n_cache = ck_ref.shape[1]
    t_new = q_ref.shape[0]
    row = lax.broadcasted_iota(jnp.int32, (t_new, t_new), 0)
    col = lax.broadcasted_iota(jnp.int32, (t_new, t_new), 1)
    causal = row >= col
    k_heads = _head_major(ck_ref[0])
    v_heads = _head_major(cv_ref[0])
    for h in range(N_HEADS):
        hs = slice(h * HEAD_DIM, (h + 1) * HEAD_DIM)
        q = q_ref[:, hs]
        f2 = f_ref[0, h:h + 1, :] * LOG2E
        kc = k_heads[h].astype(BF16)
        vc = v_heads[h].astype(BF16)
        s1 = lax.dot_general(q, kc, _NT, preferred_element_type=F32) * SCALE2 - f2[:, :n_cache]
        s2 = (lax.dot_general(q, kn_ref[:, hs], _NT, preferred_element_type=F32) * SCALE2
              - f2[:, n_cache:n_cache + t_new])
        parts = [(s1, vc), (jnp.where(causal, s2, NEG), vn_ref[:, hs])]
        o_ref[:, hs] = _softmax_pv(parts).astype(BF16)


def _fox_sample(qkv, cache_k, cache_v, f_rows, batch, t_new):
    n_cache = cache_k.shape[1]
    f_len = f_rows.shape[-1]
    cache_block = (1, n_cache, N_HEADS, HEAD_DIM)
    blocks = 4 * t_new * W_MIX * 2 + 2 * n_cache * W_MIX * 4 + N_HEADS * f_len * 4
    return pl.pallas_call(
        _fox_sample_kernel,
        grid=(batch,),
        in_specs=[
            pl.BlockSpec((t_new, W_MIX), lambda b: (b, 3)),
            pl.BlockSpec((t_new, W_MIX), lambda b: (b, 4)),
            pl.BlockSpec((t_new, W_MIX), lambda b: (b, 5)),
            pl.BlockSpec(cache_block, lambda b: (b, 0, 0, 0)),
            pl.BlockSpec(cache_block, lambda b: (b, 0, 0, 0)),
            pl.BlockSpec((1, N_HEADS, f_len), lambda b: (b, 0, 0)),
        ],
        out_specs=pl.BlockSpec((t_new, W_MIX), lambda b: (b, 0)),
        out_shape=jax.ShapeDtypeStruct((batch * t_new, W_MIX), BF16),
        compiler_params=pltpu.CompilerParams(
            dimension_semantics=("parallel",),
            vmem_limit_bytes=_vmem_limit(blocks, 0, 8 * MIB)),
        name="fox_sample",
    )(qkv, qkv, qkv, cache_k, cache_v, f_rows)


def _merge_kernel(oa_ref, ob_ref, gate_ref, x_ref, wa_ref, wb_ref, wo_ref, h_ref):
    ya = jnp.dot(oa_ref[...], wa_ref[...], preferred_element_type=F32)
    yb = jnp.dot(ob_ref[...], wb_ref[...], preferred_element_type=F32)
    m = (gate_ref[:, :D_MODEL].astype(F32) * ya + gate_ref[:, D_MODEL:].astype(F32) * yb)
    h_ref[...] = x_ref[...] + jnp.dot(m.astype(BF16), wo_ref[...], preferred_element_type=F32)


def _resident(shape):
    return pl.BlockSpec(shape, lambda *_: (0,) * len(shape), pipeline_mode=pl.Buffered(1))


def _merge(oa, ob, gates, x, w_a, w_b, w_o, *, tm):
    t = x.shape[0]
    blocks = 2 * tm * W_MIX * 2 + tm * 2 * D_MODEL * 2 + 2 * tm * D_MODEL * 4
    weights = (2 * W_MIX * D_MODEL + D_MODEL * D_MODEL) * 2
    return pl.pallas_call(
        _merge_kernel,
        grid=(t // tm,),
        in_specs=[
            pl.BlockSpec((tm, W_MIX), lambda i: (i, 0)),
            pl.BlockSpec((tm, W_MIX), lambda i: (i, 0)),
            pl.BlockSpec((tm, 2 * D_MODEL), lambda i: (i, 0)),
            pl.BlockSpec((tm, D_MODEL), lambda i: (i, 0)),
            _resident((W_MIX, D_MODEL)),
            _resident((W_MIX, D_MODEL)),
            _resident((D_MODEL, D_MODEL)),
        ],
        out_specs=pl.BlockSpec((tm, D_MODEL), lambda i: (i, 0)),
        out_shape=jax.ShapeDtypeStruct((t, D_MODEL), F32),
        compiler_params=pltpu.CompilerParams(
            dimension_semantics=("parallel",),
            vmem_limit_bytes=_vmem_limit(blocks, weights, 4 * tm * D_MODEL * 4)),
        name="merge",
    )(oa, ob, gates, x, w_a, w_b, w_o)


def _ffn_kernel(h_ref, g_ref, wu_ref, wd_ref, o_ref, n_ref):
    tf = wu_ref.shape[1]
    chunks = [slice(c * FFN_CHUNK, (c + 1) * FFN_CHUNK) for c in range(tf // FFN_CHUNK)]

    def hidden(c):
        a = jnp.dot(n_ref[...], wu_ref[:, c], preferred_element_type=F32)
        r = jnp.square(jnp.maximum(a, 0.0)).astype(BF16)
        return jnp.dot(r, wd_ref[c, :], preferred_element_type=F32)

    @pl.when(pl.program_id(1) == 0)
    def _():
        _norm_to_bf16(h_ref, g_ref, n_ref)
        o_ref[...] = h_ref[...] + hidden(chunks[0])

    @pl.when(pl.program_id(1) != 0)
    def _():
        o_ref[...] += hidden(chunks[0])

    for c in chunks[1:]:
        o_ref[...] += hidden(c)


def _ffn(h, g, w_up, w_down, *, tm, tf):
    t = h.shape[0]
    blocks = 2 * tm * D_MODEL * 4 + 2 * D_MODEL * tf * 2
    scratch = tm * D_MODEL * 2
    return pl.pallas_call(
        _ffn_kernel,
        grid=(t // tm, D_FF // tf),
        in_specs=[
            pl.BlockSpec((tm, D_MODEL), lambda i, f: (i, 0)),
            pl.BlockSpec((1, D_MODEL), lambda i, f: (0, 0)),
            pl.BlockSpec((D_MODEL, tf), lambda i, f: (0, f)),
            pl.BlockSpec((tf, D_MODEL), lambda i, f: (f, 0)),
        ],
        out_specs=pl.BlockSpec((tm, D_MODEL), lambda i, f: (i, 0)),
        out_shape=jax.ShapeDtypeStruct((t, D_MODEL), F32),
        scratch_shapes=[pltpu.VMEM((tm, D_MODEL), BF16)],
        compiler_params=pltpu.CompilerParams(
            dimension_semantics=("parallel", "arbitrary"),
            vmem_limit_bytes=_vmem_limit(blocks, scratch, 2 * tm * tf * 4)),
        name="ffn",
    )(h, g, w_up, w_down)


def _ple_kernel(h_ref, p_ref, gp_ref, gf_ref, wg_ref, wp_ref, y_ref):
    h = h_ref[...]
    n = _rmsnorm_rows(h, gp_ref[...]).astype(BF16)
    gate = _sigmoid(jnp.dot(n, wg_ref[...], preferred_element_type=F32))
    proj = jnp.dot(p_ref[...].astype(BF16), wp_ref[...], preferred_element_type=F32)
    y_ref[...] = _rmsnorm_rows(h + proj * gate, gf_ref[...])


def _ple_final(h, p, g_ple, g_final, w_gate, w_proj, *, tm):
    t = h.shape[0]
    blocks = 2 * tm * D_MODEL * 4 + tm * D_PLE * 4
    weights = (D_MODEL * D_MODEL + D_PLE * D_MODEL) * 2
    return pl.pallas_call(
        _ple_kernel,
        grid=(t // tm,),
        in_specs=[
            pl.BlockSpec((tm, D_MODEL), lambda i: (i, 0)),
            pl.BlockSpec((tm, D_PLE), lambda i: (i, 0)),
            pl.BlockSpec((1, D_MODEL), lambda i: (0, 0)),
            pl.BlockSpec((1, D_MODEL), lambda i: (0, 0)),
            _resident((D_MODEL, D_MODEL)),
            _resident((D_PLE, D_MODEL)),
        ],
        out_specs=pl.BlockSpec((tm, D_MODEL), lambda i: (i, 0)),
        out_shape=jax.ShapeDtypeStruct((t, D_MODEL), F32),
        compiler_params=pltpu.CompilerParams(
            dimension_semantics=("parallel",),
            vmem_limit_bytes=_vmem_limit(blocks, weights, 4 * tm * D_MODEL * 4)),
        name="ple_final",
    )(h, p, g_ple, g_final, w_gate, w_proj)


def _pad_lanes(x, multiple):
    pad = (-x.shape[-1]) % multiple
    return jnp.pad(x, ((0, 0),) * (x.ndim - 1) + ((0, pad),)) if pad else x


def _token_tile(t, preferred):
    return preferred if t % preferred == 0 else t


def kernel(x_prompt, x_sample, p_prompt, p_sample, cache_a_k, cache_a_v, cache_b_k, cache_b_v,
           cache_b_logf, g_mix, w_in, b_f, rel_bias, w_a_proj, w_b_proj, w_o, g_mlp, w_up, w_down,
           g_ple, w_ple_gate, w_ple_proj, g_final):
    depth = w_in.shape[0]
    assert depth == 1, "single-layer step"
    batch, s_len, _ = x_prompt.shape
    dec_batch, t_new, _ = x_sample.shape
    n_cache_a = cache_a_k.shape[2]
    assert s_len % FOX_T == 0 and s_len >= A_REACH and t_new == CHUNK and n_cache_a == A_REACH

    w = w_in[0]
    n_qkv = 6 * SEG_COLS
    w_t = jnp.swapaxes(w, 0, 1).astype(BF16)
    w_g = w_t[n_qkv + N_HEADS:]
    w_f = _pad_lanes(w[:, n_qkv:n_qkv + N_HEADS], V7X_LANES).astype(BF16)
    b_f_row = _pad_lanes(b_f[0][None, :].astype(F32), V7X_LANES)
    g_mix_row = g_mix[0][None, :].astype(F32)
    g_mlp_row = g_mlp[0][None, :].astype(F32)
    g_ple_row = g_ple[0][None, :].astype(F32)
    g_final_row = g_final[None, :].astype(F32)
    w_a = w_a_proj[0].astype(BF16)
    w_b = w_b_proj[0].astype(BF16)
    w_o_b = w_o[0].astype(BF16)
    w_up_b = w_up[0].astype(BF16)
    w_down_b = w_down[0].astype(BF16)
    w_pg = w_ple_gate[0].astype(BF16)
    w_pp = w_ple_proj[0].astype(BF16)
    bias_raw, bias_band = _rel_bias_tiles(rel_bias[0])

    def project(x2d, *, a_rows, a_period, tm):
        gates, lf, n = _in_proj_g(x2d, g_mix_row, w_g, w_f, b_f_row, tm=tm)
        return (gates, lf) + tuple(_in_proj_qkv(n, w_t, a_rows=a_rows, a_period=a_period, tm=tm))

    def finish(x2d, p2d, oa, ob, gates):
        t = x2d.shape[0]
        h1 = _merge(oa, ob, gates, x2d, w_a, w_b, w_o_b, tm=_token_tile(t, 256))
        h2 = _ffn(h1, g_mlp_row, w_up_b, w_down_b, tm=_token_tile(t, 1024), tf=1024)
        return _ple_final(h2, p2d, g_ple_row, g_final_row, w_pg, w_pp, tm=_token_tile(t, 512))

    tp = batch * s_len
    xp = x_prompt.reshape(tp, D_MODEL)
    tm_p = _token_tile(s_len, 1024)
    gates_p, lf_p, qkv_p, ka_p, va_p, kb_p, vb_p = project(
        xp, a_rows=min(A_REACH, tm_p), a_period=s_len // tm_p, tm=tm_p)
    lf_rows = lf_p.reshape(batch, s_len, N_HEADS).transpose(0, 2, 1).reshape(batch * N_HEADS, s_len)
    f_p = _cumsum_lanes(lf_rows).reshape(batch * N_HEADS, 1, s_len)
    oa_p, ob_p = _mixers_prompt(qkv_p, bias_band, f_p, batch, s_len)
    y_prompt = finish(xp, p_prompt[0].reshape(tp, D_PLE), oa_p, ob_p, gates_p)

    ts = dec_batch * t_new
    xs = x_sample.reshape(ts, D_MODEL)
    gates_s, lf_s, qkv_s, ka_s, va_s, kb_s, vb_s = project(xs, a_rows=ts, a_period=1, tm=ts)
    lf_new = lf_s.reshape(dec_batch, t_new, N_HEADS)
    lf_all = jnp.concatenate([cache_b_logf[0].astype(F32), lf_new], axis=1)
    lf_all = _pad_lanes(lf_all.transpose(0, 2, 1), V7X_LANES)
    f_s = _cumsum_lanes(lf_all.reshape(dec_batch * N_HEADS, -1)).reshape(dec_batch, N_HEADS, -1)
    oa_s, sk_s, sv_s = _band_sample(qkv_s, ka_s, va_s, cache_a_k[0], cache_a_v[0], bias_raw,
                                    dec_batch, t_new)
    ob_s = _fox_sample(qkv_s, cache_b_k[0], cache_b_v[0], f_s, dec_batch, t_new)
    y_sample = finish(xs, p_sample[0].reshape(ts, D_PLE), oa_s, ob_s, gates_s)

    def state(a, b):
        return a.reshape(1, b, -1, N_HEADS, HEAD_DIM)

    return (y_prompt.reshape(batch, s_len, D_MODEL),
            y_sample.reshape(dec_batch, t_new, D_MODEL),
            state(ka_p, batch), state(va_p, batch), state(kb_p, batch), state(vb_p, batch),
            lf_p.reshape(1, batch, s_len, N_HEADS),
            sk_s[None], sv_s[None],
            state(kb_s, dec_batch), state(vb_s, dec_batch),
            lf_new[None])
```

```python
import functools
import math

import jax
import jax.numpy as jnp
from jax import lax
from jax.experimental import pallas as pl
from jax.experimental.pallas import tpu as pltpu

F32 = jnp.float32
BF16 = jnp.bfloat16

D_MODEL = 2048
CHUNK = 64
N_PAST_CHUNKS = 8
A_REACH = N_PAST_CHUNKS * CHUNK
HEAD_DIM = 128
N_HEADS = 8
W_MIX = N_HEADS * HEAD_DIM
REL_CLIP = 128
D_FF = 4 * D_MODEL
D_PLE = 256
RMS_EPS = 1e-6
SCALE = HEAD_DIM ** -0.5
NEG = -1e30
LOG2E = math.log2(math.e)
SCALE2 = SCALE * LOG2E

V7X_LANES = 128
V7X_VMEM_LIMIT_CAP = 62 * 1024 * 1024
MIB = 1024 * 1024

NORM_ROWS = 128
BAND_TQ = 256
BAND_W = A_REACH + BAND_TQ
BAND_G = BAND_W + BAND_TQ
PV_COLS = 2 * HEAD_DIM
FOX_T = 256
MIX_HEADS = 4
GATE_COLS = 2048
SEG_COLS = 1024
FFN_CHUNK = 1024

_NT = (((1,), (1,)), ((), ()))


def _vmem_limit(block_bytes, scratch_bytes, temp_bytes):
    est = 2 * block_bytes + scratch_bytes + temp_bytes
    return int(min(max(est, 16 * MIB), V7X_VMEM_LIMIT_CAP))


def _sigmoid(x):
    return 0.5 * jnp.tanh(0.5 * x) + 0.5


def _rmsnorm_rows(x, g):
    ms = jnp.mean(x * x, axis=-1, keepdims=True)
    return (x * lax.rsqrt(ms + RMS_EPS)) * g


def _norm_to_bf16(x_ref, g_ref, n_ref):
    rows = min(NORM_ROWS, x_ref.shape[0])

    def body(c, carry):
        r = pl.ds(pl.multiple_of(c * rows, rows), rows)
        n_ref[r, :] = _rmsnorm_rows(x_ref[r, :], g_ref[...]).astype(BF16)
        return carry

    lax.fori_loop(0, x_ref.shape[0] // rows, body, 0)


def _softmax_pv(parts):
    m = functools.reduce(jnp.maximum, [jnp.max(s, axis=-1, keepdims=True) for s, _ in parts])
    ps = [jnp.exp2(s - m) for s, _ in parts]
    l = functools.reduce(jnp.add, [jnp.sum(p, axis=-1, keepdims=True) for p in ps])
    o = functools.reduce(jnp.add, [jnp.dot(p.astype(BF16), v, preferred_element_type=F32)
                                   for p, (_, v) in zip(ps, parts)])
    return o / l


def _widen_values(v_ref, vp_ref):
    lane = lax.broadcasted_iota(jnp.int32, v_ref.shape, 1)
    vp_ref[:, :HEAD_DIM] = v_ref[...]
    vp_ref[:, HEAD_DIM:] = jnp.where(lane == 0, 1.0, 0.0).astype(BF16)


def _softmax_weights(parts):
    m = functools.reduce(jnp.maximum, [jnp.max(s, axis=-1, keepdims=True) for s, _ in parts])
    return [(jnp.exp2(s - m).astype(BF16), v) for s, v in parts]


def _weighted_values_wide(weighted):
    o = functools.reduce(jnp.add, [jnp.dot(p, v, preferred_element_type=F32) for p, v in weighted])
    return o[:, :HEAD_DIM] / o[:, HEAD_DIM:HEAD_DIM + 1]


def _attention_tiles(n_tiles, logits, store):
    stage1 = {0: logits(0)}
    if n_tiles > 1:
        stage1[1] = logits(1)
    stage2 = {0: _softmax_weights(stage1.pop(0))}
    for t in range(n_tiles):
        if t + 2 < n_tiles:
            stage1[t + 2] = logits(t + 2)
        if t + 1 < n_tiles:
            stage2[t + 1] = _softmax_weights(stage1.pop(t + 1))
        store(t, _weighted_values_wide(stage2.pop(t)))


def _store_heads(ref, rows):
    stacked = jnp.stack([rows[:, h * HEAD_DIM:(h + 1) * HEAD_DIM] for h in range(N_HEADS)], axis=0)
    ref[...] = jnp.swapaxes(stacked, 0, 1)


def _head_major(rows):
    return jnp.swapaxes(rows, 0, 1)


def _inproj_g_kernel(x_ref, g_ref, w_ref, wf_ref, bf_ref, gate_ref, lf_ref, n_ref):
    @pl.when(pl.program_id(1) == 0)
    def _():
        _norm_to_bf16(x_ref, g_ref, n_ref)
        z = jnp.dot(n_ref[...], wf_ref[...], preferred_element_type=F32) + bf_ref[...]
        lf = jnp.minimum(z, 0.0) - jnp.log1p(jnp.exp(-jnp.abs(z)))
        lf_ref[...] = lf[:, :N_HEADS]

    a = lax.dot_general(n_ref[...], w_ref[...], _NT, preferred_element_type=F32)
    gate_ref[...] = _sigmoid(a).astype(BF16)


def _in_proj_g(x, g, w_g, w_f, b_f, *, tm):
    t = x.shape[0]
    tn = GATE_COLS
    blocks = (tm * D_MODEL * 4 + D_MODEL * tn * 2 + D_MODEL * V7X_LANES * 2
              + tm * tn * 2 + tm * V7X_LANES * 4 + tm * D_MODEL * 2)
    return pl.pallas_call(
        _inproj_g_kernel,
        grid=(t // tm, 4 * SEG_COLS // GATE_COLS),
        in_specs=[
            pl.BlockSpec((tm, D_MODEL), lambda i, j: (i, 0)),
            pl.BlockSpec((1, D_MODEL), lambda i, j: (0, 0)),
            pl.BlockSpec((tn, D_MODEL), lambda i, j: (j, 0)),
            pl.BlockSpec((D_MODEL, V7X_LANES), lambda i, j: (0, 0)),
            pl.BlockSpec((1, V7X_LANES), lambda i, j: (0, 0)),
        ],
        out_specs=[
            pl.BlockSpec((tm, tn), lambda i, j: (i, j)),
            pl.BlockSpec((tm, N_HEADS), lambda i, j: (i, 0)),
            pl.BlockSpec((tm, D_MODEL), lambda i, j: (i, 0)),
        ],
        out_shape=[
            jax.ShapeDtypeStruct((t, 4 * SEG_COLS), BF16),
            jax.ShapeDtypeStruct((t, N_HEADS), F32),
            jax.ShapeDtypeStruct((t, D_MODEL), BF16),
        ],
        compiler_params=pltpu.CompilerParams(
            dimension_semantics=("parallel", "arbitrary"),
            vmem_limit_bytes=_vmem_limit(blocks, 0, 10 * MIB)),
        name="in_proj_g",
    )(x, g, w_g, w_f, b_f)


def _inproj_qkv_kernel(n_ref, w_ref, qkv_ref, ka_ref, va_ref, kb_ref, vb_ref, *, a_rows, a_period):
    i = pl.program_id(0)
    j = pl.program_id(1)
    tm = n_ref.shape[0]
    is_tail = (i % a_period) == (a_period - 1)
    is_a_kv = jnp.logical_or(j == 1, j == 2)

    def project():
        return lax.dot_general(n_ref[...], w_ref[...], _NT, preferred_element_type=F32)

    plain = jnp.logical_or(jnp.logical_or(j == 0, j == 3),
                           jnp.logical_and(is_a_kv, jnp.logical_not(is_tail)))

    @pl.when(plain)
    def _():
        qkv_ref[...] = project().astype(BF16)

    for col, tail_ref in ((1, ka_ref), (2, va_ref)):
        @pl.when(jnp.logical_and(j == col, is_tail))
        def _(tail_ref=tail_ref):
            r = project()
            qkv_ref[...] = r.astype(BF16)
            _store_heads(tail_ref, r[tm - a_rows:, :])

    for col, full_ref in ((4, kb_ref), (5, vb_ref)):
        @pl.when(j == col)
        def _(full_ref=full_ref):
            r = project()
            qkv_ref[...] = r.astype(BF16)
            _store_heads(full_ref, r)


def _in_proj_qkv(n, w_t, *, a_rows, a_period, tm):
    t = n.shape[0]
    tn = SEG_COLS
    n_a = (t // tm) // a_period * a_rows
    blocks = (tm * D_MODEL * 2 + D_MODEL * tn * 2 + tm * tn * 2
              + 2 * a_rows * tn * 4 + 2 * tm * tn * 4)
    kern = functools.partial(_inproj_qkv_kernel, a_rows=a_rows, a_period=a_period)
    head_block = lambda rows: (rows, N_HEADS, HEAD_DIM)
    return pl.pallas_call(
        kern,
        grid=(t // tm, 6),
        in_specs=[
            pl.BlockSpec((tm, D_MODEL), lambda i, j: (i, 0)),
            pl.BlockSpec((tn, D_MODEL), lambda i, j: (j, 0)),
        ],
        out_specs=[
            pl.BlockSpec((tm, tn), lambda i, j: (i, j)),
            pl.BlockSpec(head_block(a_rows), lambda i, j: (i // a_period, 0, 0)),
            pl.BlockSpec(head_block(a_rows), lambda i, j: (i // a_period, 0, 0)),
            pl.BlockSpec(head_block(tm), lambda i, j: (i, 0, 0)),
            pl.BlockSpec(head_block(tm), lambda i, j: (i, 0, 0)),
        ],
        out_shape=[
            jax.ShapeDtypeStruct((t, 6 * SEG_COLS), BF16),
            jax.ShapeDtypeStruct(head_block(n_a), F32),
            jax.ShapeDtypeStruct(head_block(n_a), F32),
            jax.ShapeDtypeStruct(head_block(t), F32),
            jax.ShapeDtypeStruct(head_block(t), F32),
        ],
        compiler_params=pltpu.CompilerParams(
            dimension_semantics=("parallel", "arbitrary"),
            vmem_limit_bytes=_vmem_limit(blocks, 0, 6 * MIB)),
        name="in_proj_qkv",
    )(n, w_t)


def _cumsum_kernel(x_ref, o_ref):
    rows, length = x_ref.shape
    r_i = lax.broadcasted_iota(jnp.int32, (V7X_LANES, V7X_LANES), 0)
    c_i = lax.broadcasted_iota(jnp.int32, (V7X_LANES, V7X_LANES), 1)
    tri = (r_i <= c_i).astype(BF16)
    carry = jnp.zeros((rows, 1), F32)
    for c in range(length // V7X_LANES):
        x = x_ref[:, c * V7X_LANES:(c + 1) * V7X_LANES]
        hi = x.astype(BF16)
        r1 = x - hi.astype(F32)
        mid = r1.astype(BF16)
        lo = (r1 - mid.astype(F32)).astype(BF16)
        blk = (jnp.dot(hi, tri, preferred_element_type=F32)
               + jnp.dot(mid, tri, preferred_element_type=F32)
               + jnp.dot(lo, tri, preferred_element_type=F32)) + carry
        o_ref[:, c * V7X_LANES:(c + 1) * V7X_LANES] = blk
        carry = blk[:, V7X_LANES - 1:V7X_LANES]


def _cumsum_lanes(x):
    return pl.pallas_call(
        _cumsum_kernel,
        out_shape=jax.ShapeDtypeStruct(x.shape, F32),
        name="logf_cumsum",
    )(x)


def _rel_bias_kernel(g_ref, raw_ref, band_ref):
    row = lax.broadcasted_iota(jnp.int32, (BAND_TQ, BAND_W), 0)
    col = lax.broadcasted_iota(jnp.int32, (BAND_TQ, BAND_W), 1)
    q_chunk = row // CHUNK
    k_chunk = col // CHUNK - N_PAST_CHUNKS
    in_band = jnp.logical_and(k_chunk >= q_chunk - N_PAST_CHUNKS, k_chunk <= q_chunk)
    for h in range(N_HEADS):
        rows = jnp.broadcast_to(g_ref[h:h + 1, :], (BAND_TQ, BAND_G))
        toeplitz = pltpu.roll(rows, 0, 1, stride=1, stride_axis=0)[:, :BAND_W] * LOG2E
        raw_ref[h] = toeplitz
        band_ref[h] = jnp.where(in_band, toeplitz, NEG)


def _rel_bias_tiles(table):
    far_past = table[:, 2 * REL_CLIP:]
    far_future = table[:, :1]
    near = jnp.flip(table[:, :2 * REL_CLIP], axis=1)
    n_past = A_REACH - REL_CLIP + 1
    n_future = BAND_W - n_past - near.shape[1]
    g = jnp.concatenate([jnp.broadcast_to(far_past, (N_HEADS, n_past)), near,
                         jnp.broadcast_to(far_future, (N_HEADS, n_future)),
                         jnp.broadcast_to(far_past, (N_HEADS, BAND_G - BAND_W))], axis=1)
    shape = jax.ShapeDtypeStruct((N_HEADS, BAND_TQ, BAND_W), F32)
    return pl.pallas_call(_rel_bias_kernel, out_shape=[shape, shape], name="rel_bias")(g.astype(F32))


def _mixers_prompt_kernel(qa_ref, ka_ref, va_ref, qb_ref, kb_ref, vb_ref, bias_ref, f_ref,
                          oa_ref, ob_ref, vpa_ref, vpb_ref):
    s_len = qa_ref.shape[0]
    row = lax.broadcasted_iota(jnp.int32, (FOX_T, FOX_T), 0)
    col = lax.broadcasted_iota(jnp.int32, (FOX_T, FOX_T), 1)
    causal = row >= col
    lanes = [slice(h * HEAD_DIM, (h + 1) * HEAD_DIM) for h in range(MIX_HEADS)]
    f2 = []
    for h, ls in enumerate(lanes):
        _widen_values(va_ref.at[:, ls], vpa_ref.at[h])
        _widen_values(vb_ref.at[:, ls], vpb_ref.at[h])
        f2.append(f_ref[h] * LOG2E)

    def band_logits(h, t):
        q0 = t * BAND_TQ
        k0 = max(q0 - A_REACH, 0)
        n_keys = q0 + BAND_TQ - k0
        s = lax.dot_general(qa_ref[q0:q0 + BAND_TQ, lanes[h]], ka_ref[k0:k0 + n_keys, lanes[h]], _NT,
                            preferred_element_type=F32)
        return [(s * SCALE2 + bias_ref[h, :, BAND_W - n_keys:], vpa_ref[h, k0:k0 + n_keys, :])]

    def fox_logits(h, qi):
        q0 = qi * FOX_T
        q = qb_ref[q0:q0 + FOX_T, lanes[h]]
        s_diag = (lax.dot_general(q, kb_ref[q0:q0 + FOX_T, lanes[h]], _NT, preferred_element_type=F32)
                  * SCALE2 - f2[h][:, q0:q0 + FOX_T])
        parts = [(jnp.where(causal, s_diag, NEG), vpb_ref[h, q0:q0 + FOX_T, :])]
        if qi:
            s_past = (lax.dot_general(q, kb_ref[0:q0, lanes[h]], _NT, preferred_element_type=F32) * SCALE2
                      - f2[h][:, 0:q0])
            parts.append((s_past, vpb_ref[h, 0:q0, :]))
        return parts

    n_band = s_len // BAND_TQ
    n_fox = s_len // FOX_T
    order = []
    for t in range(max(n_band, n_fox)):
        for h in range(MIX_HEADS):
            if t < n_band:
                order.append(("band", h, t))
            if t < n_fox:
                order.append(("fox", h, t))

    def logits(i):
        kind, h, t = order[i]
        return band_logits(h, t) if kind == "band" else fox_logits(h, t)

    def store(i, o):
        kind, h, t = order[i]
        if kind == "band":
            oa_ref[t * BAND_TQ:(t + 1) * BAND_TQ, lanes[h]] = o.astype(BF16)
        else:
            ob_ref[t * FOX_T:(t + 1) * FOX_T, lanes[h]] = o.astype(BF16)

    _attention_tiles(len(order), logits, store)


def _mixers_prompt(qkv, bias, f_rows, batch, s_len):
    t = batch * s_len
    groups = N_HEADS // MIX_HEADS
    width = MIX_HEADS * HEAD_DIM
    head = lambda seg: pl.BlockSpec((s_len, width), lambda b, g: (b, seg * groups + g))
    blocks = 8 * s_len * width * 2 + MIX_HEADS * (BAND_TQ * BAND_W * 4 + 8 * s_len * 4)
    scratch = 2 * MIX_HEADS * s_len * PV_COLS * 2
    out = jax.ShapeDtypeStruct((t, W_MIX), BF16)
    vp = pltpu.VMEM((MIX_HEADS, s_len, PV_COLS), BF16)
    return pl.pallas_call(
        _mixers_prompt_kernel,
        grid=(batch, groups),
        in_specs=[head(s) for s in range(6)] + [
            pl.BlockSpec((MIX_HEADS, BAND_TQ, BAND_W), lambda b, g: (g, 0, 0)),
            pl.BlockSpec((MIX_HEADS, 1, s_len), lambda b, g: (b * groups + g, 0, 0))],
        out_specs=[head(0), head(0)],
        out_shape=[out, out],
        scratch_shapes=[vp, vp],
        compiler_params=pltpu.CompilerParams(
            dimension_semantics=("parallel", "parallel"),
            vmem_limit_bytes=_vmem_limit(blocks, scratch, 8 * FOX_T * s_len * 4)),
        name="mixers_prompt",
    )(qkv, qkv, qkv, qkv, qkv, qkv, bias, f_rows)


def _band_sample_kernel(q_ref, kn_ref, vn_ref, kn32_ref, vn32_ref, ck_ref, cv_ref, bias_ref,
                        o_ref, sk_ref, sv_ref):
    n_cache = ck_ref.shape[1]
    t_new = q_ref.shape[0]
    k_heads = _head_major(ck_ref[0])
    v_heads = _head_major(cv_ref[0])
    for h in range(N_HEADS):
        hs = slice(h * HEAD_DIM, (h + 1) * HEAD_DIM)
        q = q_ref[:, hs]
        kc = k_heads[h].astype(BF16)
        vc = v_heads[h].astype(BF16)
        s1 = (lax.dot_general(q, kc, _NT, preferred_element_type=F32) * SCALE2
              + bias_ref[h, :t_new, :n_cache])
        s2 = (lax.dot_general(q, kn_ref[:, hs], _NT, preferred_element_type=F32) * SCALE2
              + bias_ref[h, :t_new, n_cache:n_cache + t_new])
        o_ref[:, hs] = _softmax_pv([(s1, vc), (s2, vn_ref[:, hs])]).astype(BF16)
    for cache_ref, new_ref, state_ref in ((ck_ref, kn32_ref, sk_ref), (cv_ref, vn32_ref, sv_ref)):
        state_ref[0, :n_cache - t_new] = cache_ref[0, t_new:]
        state_ref[0, n_cache - t_new:] = new_ref[...]


def _band_sample(qkv, ka32, va32, cache_k, cache_v, bias, batch, t_new):
    n_cache = cache_k.shape[1]
    cache_block = (1, n_cache, N_HEADS, HEAD_DIM)
    new_block = (t_new, N_HEADS, HEAD_DIM)
    blocks = 4 * t_new * W_MIX * 2 + 2 * t_new * W_MIX * 4 + 4 * n_cache * W_MIX * 4 + bias.size * 4
    return pl.pallas_call(
        _band_sample_kernel,
        grid=(batch,),
        in_specs=[
            pl.BlockSpec((t_new, W_MIX), lambda b: (b, 0)),
            pl.BlockSpec((t_new, W_MIX), lambda b: (b, 1)),
            pl.BlockSpec((t_new, W_MIX), lambda b: (b, 2)),
            pl.BlockSpec(new_block, lambda b: (b, 0, 0)),
            pl.BlockSpec(new_block, lambda b: (b, 0, 0)),
            pl.BlockSpec(cache_block, lambda b: (b, 0, 0, 0)),
            pl.BlockSpec(cache_block, lambda b: (b, 0, 0, 0)),
            pl.BlockSpec(bias.shape, lambda b: (0, 0, 0)),
        ],
        out_specs=[
            pl.BlockSpec((t_new, W_MIX), lambda b: (b, 0)),
            pl.BlockSpec(cache_block, lambda b: (b, 0, 0, 0)),
            pl.BlockSpec(cache_block, lambda b: (b, 0, 0, 0)),
        ],
        out_shape=[
            jax.ShapeDtypeStruct((batch * t_new, W_MIX), BF16),
            jax.ShapeDtypeStruct(cache_k.shape, F32),
            jax.ShapeDtypeStruct(cache_v.shape, F32),
        ],
        compiler_params=pltpu.CompilerParams(
            dimension_semantics=("parallel",),
            vmem_limit_bytes=_vmem_limit(blocks, 0, 4 * MIB)),
        name="band_sample",
    )(qkv, qkv, qkv, ka32, va32, cache_k, cache_v, bias)


def _fox_sample_kernel(q_ref, kn_ref, vn_ref, ck_ref, cv_ref, f_ref, o_ref):
    n_cache = ck_ref.shape[1]
    t_new = q_ref.shape[0]
    row = lax.broadcasted_iota(jnp.int32, (t_new, t_new), 0)
    col = lax.broadcasted_iota(jnp.int32, (t_new, t_new), 1)
    causal = row >= col
    k_heads = _head_major(ck_ref[0])
    v_heads = _head_major(cv_ref[0])
    for h in range(N_HEADS):
        hs = slice(h * HEAD_DIM, (h + 1) * HEAD_DIM)
        q = q_ref[:, hs]
        f2 = f_ref[0, h:h + 1, :] * LOG2E
        kc = k_heads[h].astype(BF16)
        vc = v_heads[h].astype(BF16)
        s1 = lax.dot_general(q, kc, _NT, preferred_element_type=F32) * SCALE2 - f2[:, :n_cache]
        s2 = (lax.dot_general(q, kn_ref[:, hs], _NT, preferred_element_type=F32) * SCALE2
              - f2[:, n_cache:n_cache + t_new])
        parts = [(s1, vc), (jnp.where(causal, s2, NEG), vn_ref[:, hs])]
        o_ref[:, hs] = _softmax_pv(parts).astype(BF16)


def _fox_sample(qkv, cache_k, cache_v, f_rows, batch, t_new):
    n_cache = cache_k.shape[1]
    f_len = f_rows.shape[-1]
    cache_block = (1, n_cache, N_HEADS, HEAD_DIM)
    blocks = 4 * t_new * W_MIX * 2 + 2 * n_cache * W_MIX * 4 + N_HEADS * f_len * 4
    return pl.pallas_call(
        _fox_sample_kernel,
        grid=(batch,),
        in_specs=[
            pl.BlockSpec((t_new, W_MIX), lambda b: (b, 3)),
            pl.BlockSpec((t_new, W_MIX), lambda b: (b, 4)),
            pl.BlockSpec((t_new, W_MIX), lambda b: (b, 5)),
            pl.BlockSpec(cache_block, lambda b: (b, 0, 0, 0)),
            pl.BlockSpec(cache_block, lambda b: (b, 0, 0, 0)),
            pl.BlockSpec((1, N_HEADS, f_len), lambda b: (b, 0, 0)),
        ],
        out_specs=pl.BlockSpec((t_new, W_MIX), lambda b: (b, 0)),
        out_shape=jax.ShapeDtypeStruct((batch * t_new, W_MIX), BF16),
        compiler_params=pltpu.CompilerParams(
            dimension_semantics=("parallel",),
            vmem_limit_bytes=_vmem_limit(blocks, 0, 8 * MIB)),
        name="fox_sample",
    )(qkv, qkv, qkv, cache_k, cache_v, f_rows)


def _merge_kernel(oa_ref, ob_ref, gate_ref, x_ref, wa_ref, wb_ref, wo_ref, h_ref):
    ya = jnp.dot(oa_ref[...], wa_ref[...], preferred_element_type=F32)
    yb = jnp.dot(ob_ref[...], wb_ref[...], preferred_element_type=F32)
    m = (gate_ref[:, :D_MODEL].astype(F32) * ya + gate_ref[:, D_MODEL:].astype(F32) * yb)
    h_ref[...] = x_ref[...] + jnp.dot(m.astype(BF16), wo_ref[...], preferred_element_type=F32)


def _resident(shape):
    return pl.BlockSpec(shape, lambda *_: (0,) * len(shape), pipeline_mode=pl.Buffered(1))


def _merge(oa, ob, gates, x, w_a, w_b, w_o, *, tm):
    t = x.shape[0]
    blocks = 2 * tm * W_MIX * 2 + tm * 2 * D_MODEL * 2 + 2 * tm * D_MODEL * 4
    weights = (2 * W_MIX * D_MODEL + D_MODEL * D_MODEL) * 2
    return pl.pallas_call(
        _merge_kernel,
        grid=(t // tm,),
        in_specs=[
            pl.BlockSpec((tm, W_MIX), lambda i: (i, 0)),
            pl.BlockSpec((tm, W_MIX), lambda i: (i, 0)),
            pl.BlockSpec((tm, 2 * D_MODEL), lambda i: (i, 0)),
            pl.BlockSpec((tm, D_MODEL), lambda i: (i, 0)),
            _resident((W_MIX, D_MODEL)),
            _resident((W_MIX, D_MODEL)),
            _resident((D_MODEL, D_MODEL)),
        ],
        out_specs=pl.BlockSpec((tm, D_MODEL), lambda i: (i, 0)),
        out_shape=jax.ShapeDtypeStruct((t, D_MODEL), F32),
        compiler_params=pltpu.CompilerParams(
            dimension_semantics=("parallel",),
            vmem_limit_bytes=_vmem_limit(blocks, weights, 4 * tm * D_MODEL * 4)),
        name="merge",
    )(oa, ob, gates, x, w_a, w_b, w_o)


def _ffn_kernel(h_ref, g_ref, wu_ref, wd_ref, o_ref, n_ref):
    tf = wu_ref.shape[1]
    chunks = [slice(c * FFN_CHUNK, (c + 1) * FFN_CHUNK) for c in range(tf // FFN_CHUNK)]

    def hidden(c):
        a = jnp.dot(n_ref[...], wu_ref[:, c], preferred_element_type=F32)
        r = jnp.square(jnp.maximum(a, 0.0)).astype(BF16)
        return jnp.dot(r, wd_ref[c, :], preferred_element_type=F32)

    @pl.when(pl.program_id(1) == 0)
    def _():
        _norm_to_bf16(h_ref, g_ref, n_ref)
        o_ref[...] = h_ref[...] + hidden(chunks[0])

    @pl.when(pl.program_id(1) != 0)
    def _():
        o_ref[...] += hidden(chunks[0])

    for c in chunks[1:]:
        o_ref[...] += hidden(c)


def _ffn(h, g, w_up, w_down, *, tm, tf):
    t = h.shape[0]
    blocks = 2 * tm * D_MODEL * 4 + 2 * D_MODEL * tf * 2
    scratch = tm * D_MODEL * 2
    return pl.pallas_call(
        _ffn_kernel,
        grid=(t // tm, D_FF // tf),
        in_specs=[
            pl.BlockSpec((tm, D_MODEL), lambda i, f: (i, 0)),
            pl.BlockSpec((1, D_MODEL), lambda i, f: (0, 0)),
            pl.BlockSpec((D_MODEL, tf), lambda i, f: (0, f)),
            pl.BlockSpec((tf, D_MODEL), lambda i, f: (f, 0)),
        ],
        out_specs=pl.BlockSpec((tm, D_MODEL), lambda i, f: (i, 0)),
        out_shape=jax.ShapeDtypeStruct((t, D_MODEL), F32),
        scratch_shapes=[pltpu.VMEM((tm, D_MODEL), BF16)],
        compiler_params=pltpu.CompilerParams(
            dimension_semantics=("parallel", "arbitrary"),
            vmem_limit_bytes=_vmem_limit(blocks, scratch, 2 * tm * tf * 4)),
        name="ffn",
    )(h, g, w_up, w_down)


def _ple_kernel(h_ref, p_ref, gp_ref, gf_ref, wg_ref, wp_ref, y_ref):
    h = h_ref[...]
    n = _rmsnorm_rows(h, gp_ref[...]).astype(BF16)
    gate = _sigmoid(jnp.dot(n, wg_ref[...], preferred_element_type=F32))
    proj = jnp.dot(p_ref[...].astype(BF16), wp_ref[...], preferred_element_type=F32)
    y_ref[...] = _rmsnorm_rows(h + proj * gate, gf_ref[...])


def _ple_final(h, p, g_ple, g_final, w_gate, w_proj, *, tm):
    t = h.shape[0]
    blocks = 2 * tm * D_MODEL * 4 + tm * D_PLE * 4
    weights = (D_MODEL * D_MODEL + D_PLE * D_MODEL) * 2
    return pl.pallas_call(
        _ple_kernel,
        grid=(t // tm,),
        in_specs=[
            pl.BlockSpec((tm, D_MODEL), lambda i: (i, 0)),
            pl.BlockSpec((tm, D_PLE), lambda i: (i, 0)),
            pl.BlockSpec((1, D_MODEL), lambda i: (0, 0)),
            pl.BlockSpec((1, D_MODEL), lambda i: (0, 0)),
            _resident((D_MODEL, D_MODEL)),
            _resident((D_PLE, D_MODEL)),
        ],
        out_specs=pl.BlockSpec((tm, D_MODEL), lambda i: (i, 0)),
        out_shape=jax.ShapeDtypeStruct((t, D_MODEL), F32),
        compiler_params=pltpu.CompilerParams(
            dimension_semantics=("parallel",),
            vmem_limit_bytes=_vmem_limit(blocks, weights, 4 * tm * D_MODEL * 4)),
        name="ple_final",
    )(h, p, g_ple, g_final, w_gate, w_proj)


def _pad_lanes(x, multiple):
    pad = (-x.shape[-1]) % multiple
    return jnp.pad(x, ((0, 0),) * (x.ndim - 1) + ((0, pad),)) if pad else x


def _token_tile(t, preferred):
    return preferred if t % preferred == 0 else t


def kernel(x_prompt, x_sample, p_prompt, p_sample, cache_a_k, cache_a_v, cache_b_k, cache_b_v,
           cache_b_logf, g_mix, w_in, b_f, rel_bias, w_a_proj, w_b_proj, w_o, g_mlp, w_up, w_down,
           g_ple, w_ple_gate, w_ple_proj, g_final):
    depth = w_in.shape[0]
    assert depth == 1, "single-layer step"
    batch, s_len, _ = x_prompt.shape
    dec_batch, t_new, _ = x_sample.shape
    n_cache_a = cache_a_k.shape[2]
    assert s_len % FOX_T == 0 and s_len >= A_REACH and t_new == CHUNK and n_cache_a == A_REACH

    w = w_in[0]
    n_qkv = 6 * SEG_COLS
    w_t = jnp.swapaxes(w, 0, 1).astype(BF16)
    w_g = w_t[n_qkv + N_HEADS:]
    w_f = _pad_lanes(w[:, n_qkv:n_qkv + N_HEADS], V7X_LANES).astype(BF16)
    b_f_row = _pad_lanes(b_f[0][None, :].astype(F32), V7X_LANES)
    g_mix_row = g_mix[0][None, :].astype(F32)
    g_mlp_row = g_mlp[0][None, :].astype(F32)
    g_ple_row = g_ple[0][None, :].astype(F32)
    g_final_row = g_final[None, :].astype(F32)
    w_a = w_a_proj[0].astype(BF16)
    w_b = w_b_proj[0].astype(BF16)
    w_o_b = w_o[0].astype(BF16)
    w_up_b = w_up[0].astype(BF16)
    w_down_b = w_down[0].astype(BF16)
    w_pg = w_ple_gate[0].astype(BF16)
    w_pp = w_ple_proj[0].astype(BF16)
    bias_raw, bias_band = _rel_bias_tiles(rel_bias[0])

    def project(x2d, *, a_rows, a_period, tm):
        gates, lf, n = _in_proj_g(x2d, g_mix_row, w_g, w_f, b_f_row, tm=tm)
        return (gates, lf) + tuple(_in_proj_qkv(n, w_t, a_rows=a_rows, a_period=a_period, tm=tm))

    def finish(x2d, p2d, oa, ob, gates):
        t = x2d.shape[0]
        h1 = _merge(oa, ob, gates, x2d, w_a, w_b, w_o_b, tm=_token_tile(t, 256))
        h2 = _ffn(h1, g_mlp_row, w_up_b, w_down_b, tm=_token_tile(t, 1024), tf=1024)
        return _ple_final(h2, p2d, g_ple_row, g_final_row, w_pg, w_pp, tm=_token_tile(t, 512))

    tp = batch * s_len
    xp = x_prompt.reshape(tp, D_MODEL)
    tm_p = _token_tile(s_len, 1024)
    gates_p, lf_p, qkv_p, ka_p, va_p, kb_p, vb_p = project(
        xp, a_rows=min(A_REACH, tm_p), a_period=s_len // tm_p, tm=tm_p)
    lf_rows = lf_p.reshape(batch, s_len, N_HEADS).transpose(0, 2, 1).reshape(batch * N_HEADS, s_len)
    f_p = _cumsum_lanes(lf_rows).reshape(batch * N_HEADS, 1, s_len)
    oa_p, ob_p = _mixers_prompt(qkv_p, bias_band, f_p, batch, s_len)
    y_prompt = finish(xp, p_prompt[0].reshape(tp, D_PLE), oa_p, ob_p, gates_p)

    ts = dec_batch * t_new
    xs = x_sample.reshape(ts, D_MODEL)
    gates_s, lf_s, qkv_s, ka_s, va_s, kb_s, vb_s = project(xs, a_rows=ts, a_period=1, tm=ts)
    lf_new = lf_s.reshape(dec_batch, t_new, N_HEADS)
    lf_all = jnp.concatenate([cache_b_logf[0].astype(F32), lf_new], axis=1)
    lf_all = _pad_lanes(lf_all.transpose(0, 2, 1), V7X_LANES)
    f_s = _cumsum_lanes(lf_all.reshape(dec_batch * N_HEADS, -1)).reshape(dec_batch, N_HEADS, -1)
    oa_s, sk_s, sv_s = _band_sample(qkv_s, ka_s, va_s, cache_a_k[0], cache_a_v[0], bias_raw,
                                    dec_batch, t_new)
    ob_s = _fox_sample(qkv_s, cache_b_k[0], cache_b_v[0], f_s, dec_batch, t_new)
    y_sample = finish(xs, p_sample[0].reshape(ts, D_PLE), oa_s, ob_s, gates_s)

    def state(a, b):
        return a.reshape(1, b, -1, N_HEADS, HEAD_DIM)

    return (y_prompt.reshape(batch, s_len, D_MODEL),
            y_sample.reshape(dec_batch, t_new, D_MODEL),
            state(ka_p, batch), state(va_p, batch), state(kb_p, batch), state(vb_p, batch),
            lf_p.reshape(1, batch, s_len, N_HEADS),
            sk_s[None], sv_s[None],
            state(kb_s, dec_batch), state(vb_s, dec_batch),
            lf_new[None])
```

```python
import functools
import math

import jax
import jax.numpy as jnp
from jax import lax
from jax.experimental import pallas as pl
from jax.experimental.pallas import tpu as pltpu

F32 = jnp.float32
BF16 = jnp.bfloat16

D_MODEL = 2048
CHUNK = 64
N_PAST_CHUNKS = 8
A_REACH = N_PAST_CHUNKS * CHUNK
HEAD_DIM = 128
N_HEADS = 8
W_MIX = N_HEADS * HEAD_DIM
REL_CLIP = 128
D_FF = 4 * D_MODEL
D_PLE = 256
RMS_EPS = 1e-6
SCALE = HEAD_DIM ** -0.5
NEG = -1e30
LOG2E = math.log2(math.e)
SCALE2 = SCALE * LOG2E

V7X_LANES = 128
V7X_VMEM_LIMIT_CAP = 62 * 1024 * 1024
MIB = 1024 * 1024

NORM_ROWS = 128
BAND_TQ = 256
BAND_W = A_REACH + BAND_TQ
BAND_G = BAND_W + BAND_TQ
PV_COLS = 2 * HEAD_DIM
FOX_T = 256
MIX_HEADS = 2
GATE_COLS = 2048
SEG_COLS = 1024
FFN_CHUNK = 1024

_NT = (((1,), (1,)), ((), ()))


def _vmem_limit(block_bytes, scratch_bytes, temp_bytes):
    est = 2 * block_bytes + scratch_bytes + temp_bytes
    return int(min(max(est, 16 * MIB), V7X_VMEM_LIMIT_CAP))


def _sigmoid(x):
    return 0.5 * jnp.tanh(0.5 * x) + 0.5


def _rmsnorm_rows(x, g):
    ms = jnp.mean(x * x, axis=-1, keepdims=True)
    return (x * lax.rsqrt(ms + RMS_EPS)) * g


def _norm_to_bf16(x_ref, g_ref, n_ref):
    rows = min(NORM_ROWS, x_ref.shape[0])

    def body(c, carry):
        r = pl.ds(pl.multiple_of(c * rows, rows), rows)
        n_ref[r, :] = _rmsnorm_rows(x_ref[r, :], g_ref[...]).astype(BF16)
        return carry

    lax.fori_loop(0, x_ref.shape[0] // rows, body, 0)


def _softmax_pv(parts):
    m = functools.reduce(jnp.maximum, [jnp.max(s, axis=-1, keepdims=True) for s, _ in parts])
    ps = [jnp.exp2(s - m) for s, _ in parts]
    l = functools.reduce(jnp.add, [jnp.sum(p, axis=-1, keepdims=True) for p in ps])
    o = functools.reduce(jnp.add, [jnp.dot(p.astype(BF16), v, preferred_element_type=F32)
                                   for p, (_, v) in zip(ps, parts)])
    return o / l


def _widen_values(v_ref, vp_ref):
    lane = lax.broadcasted_iota(jnp.int32, v_ref.shape, 1)
    vp_ref[:, :HEAD_DIM] = v_ref[...]
    vp_ref[:, HEAD_DIM:] = jnp.where(lane == 0, 1.0, 0.0).astype(BF16)


def _softmax_weights(parts):
    m = functools.reduce(jnp.maximum, [jnp.max(s, axis=-1, keepdims=True) for s, _ in parts])
    return [(jnp.exp2(s - m).astype(BF16), v) for s, v in parts]


def _weighted_values_wide(weighted):
    o = functools.reduce(jnp.add, [jnp.dot(p, v, preferred_element_type=F32) for p, v in weighted])
    return o[:, :HEAD_DIM] / o[:, HEAD_DIM:HEAD_DIM + 1]


def _attention_tiles(n_tiles, logits, store):
    stage1 = {0: logits(0)}
    if n_tiles > 1:
        stage1[1] = logits(1)
    stage2 = {0: _softmax_weights(stage1.pop(0))}
    for t in range(n_tiles):
        if t + 2 < n_tiles:
            stage1[t + 2] = logits(t + 2)
        if t + 1 < n_tiles:
            stage2[t + 1] = _softmax_weights(stage1.pop(t + 1))
        store(t, _weighted_values_wide(stage2.pop(t)))


def _store_heads(ref, rows):
    stacked = jnp.stack([rows[:, h * HEAD_DIM:(h + 1) * HEAD_DIM] for h in range(N_HEADS)], axis=0)
    ref[...] = jnp.swapaxes(stacked, 0, 1)


def _head_major(rows):
    return jnp.swapaxes(rows, 0, 1)


def _inproj_g_kernel(x_ref, g_ref, w_ref, wf_ref, bf_ref, gate_ref, lf_ref, n_ref):
    @pl.when(pl.program_id(1) == 0)
    def _():
        _norm_to_bf16(x_ref, g_ref, n_ref)
        z = jnp.dot(n_ref[...], wf_ref[...], preferred_element_type=F32) + bf_ref[...]
        lf = jnp.minimum(z, 0.0) - jnp.log1p(jnp.exp(-jnp.abs(z)))
        lf_ref[...] = lf[:, :N_HEADS]

    a = lax.dot_general(n_ref[...], w_ref[...], _NT, preferred_element_type=F32)
    gate_ref[...] = _sigmoid(a).astype(BF16)


def _in_proj_g(x, g, w_g, w_f, b_f, *, tm):
    t = x.shape[0]
    tn = GATE_COLS
    blocks = (tm * D_MODEL * 4 + D_MODEL * tn * 2 + D_MODEL * V7X_LANES * 2
              + tm * tn * 2 + tm * V7X_LANES * 4 + tm * D_MODEL * 2)
    return pl.pallas_call(
        _inproj_g_kernel,
        grid=(t // tm, 4 * SEG_COLS // GATE_COLS),
        in_specs=[
            pl.BlockSpec((tm, D_MODEL), lambda i, j: (i, 0)),
            pl.BlockSpec((1, D_MODEL), lambda i, j: (0, 0)),
            pl.BlockSpec((tn, D_MODEL), lambda i, j: (j, 0)),
            pl.BlockSpec((D_MODEL, V7X_LANES), lambda i, j: (0, 0)),
            pl.BlockSpec((1, V7X_LANES), lambda i, j: (0, 0)),
        ],
        out_specs=[
            pl.BlockSpec((tm, tn), lambda i, j: (i, j)),
            pl.BlockSpec((tm, N_HEADS), lambda i, j: (i, 0)),
            pl.BlockSpec((tm, D_MODEL), lambda i, j: (i, 0)),
        ],
        out_shape=[
            jax.ShapeDtypeStruct((t, 4 * SEG_COLS), BF16),
            jax.ShapeDtypeStruct((t, N_HEADS), F32),
            jax.ShapeDtypeStruct((t, D_MODEL), BF16),
        ],
        compiler_params=pltpu.CompilerParams(
            dimension_semantics=("parallel", "arbitrary"),
            vmem_limit_bytes=_vmem_limit(blocks, 0, 10 * MIB)),
        name="in_proj_g",
    )(x, g, w_g, w_f, b_f)


def _inproj_qkv_kernel(n_ref, w_ref, qkv_ref, ka_ref, va_ref, kb_ref, vb_ref, *, a_rows, a_period):
    i = pl.program_id(0)
    j = pl.program_id(1)
    tm = n_ref.shape[0]
    is_tail = (i % a_period) == (a_period - 1)
    is_a_kv = jnp.logical_or(j == 1, j == 2)

    def project():
        return lax.dot_general(n_ref[...], w_ref[...], _NT, preferred_element_type=F32)

    plain = jnp.logical_or(jnp.logical_or(j == 0, j == 3),
                           jnp.logical_and(is_a_kv, jnp.logical_not(is_tail)))

    @pl.when(plain)
    def _():
        qkv_ref[...] = project().astype(BF16)

    for col, tail_ref in ((1, ka_ref), (2, va_ref)):
        @pl.when(jnp.logical_and(j == col, is_tail))
        def _(tail_ref=tail_ref):
            r = project()
            qkv_ref[...] = r.astype(BF16)
            _store_heads(tail_ref, r[tm - a_rows:, :])

    for col, full_ref in ((4, kb_ref), (5, vb_ref)):
        @pl.when(j == col)
        def _(full_ref=full_ref):
            r = project()
            qkv_ref[...] = r.astype(BF16)
            _store_heads(full_ref, r)


def _in_proj_qkv(n, w_t, *, a_rows, a_period, tm):
    t = n.shape[0]
    tn = SEG_COLS
    n_a = (t // tm) // a_period * a_rows
    blocks = (tm * D_MODEL * 2 + D_MODEL * tn * 2 + tm * tn * 2
              + 2 * a_rows * tn * 4 + 2 * tm * tn * 4)
    kern = functools.partial(_inproj_qkv_kernel, a_rows=a_rows, a_period=a_period)
    head_block = lambda rows: (rows, N_HEADS, HEAD_DIM)
    return pl.pallas_call(
        kern,
        grid=(t // tm, 6),
        in_specs=[
            pl.BlockSpec((tm, D_MODEL), lambda i, j: (i, 0)),
            pl.BlockSpec((tn, D_MODEL), lambda i, j: (j, 0)),
        ],
        out_specs=[
            pl.BlockSpec((tm, tn), lambda i, j: (i, j)),
            pl.BlockSpec(head_block(a_rows), lambda i, j: (i // a_period, 0, 0)),
            pl.BlockSpec(head_block(a_rows), lambda i, j: (i // a_period, 0, 0)),
            pl.BlockSpec(head_block(tm), lambda i, j: (i, 0, 0)),
            pl.BlockSpec(head_block(tm), lambda i, j: (i, 0, 0)),
        ],
        out_shape=[
            jax.ShapeDtypeStruct((t, 6 * SEG_COLS), BF16),
            jax.ShapeDtypeStruct(head_block(n_a), F32),
            jax.ShapeDtypeStruct(head_block(n_a), F32),
            jax.ShapeDtypeStruct(head_block(t), F32),
            jax.ShapeDtypeStruct(head_block(t), F32),
        ],
        compiler_params=pltpu.CompilerParams(
            dimension_semantics=("parallel", "arbitrary"),
            vmem_limit_bytes=_vmem_limit(blocks, 0, 6 * MIB)),
        name="in_proj_qkv",
    )(n, w_t)


def _cumsum_kernel(x_ref, o_ref):
    rows, length = x_ref.shape
    r_i = lax.broadcasted_iota(jnp.int32, (V7X_LANES, V7X_LANES), 0)
    c_i = lax.broadcasted_iota(jnp.int32, (V7X_LANES, V7X_LANES), 1)
    tri = (r_i <= c_i).astype(BF16)
    carry = jnp.zeros((rows, 1), F32)
    for c in range(length // V7X_LANES):
        x = x_ref[:, c * V7X_LANES:(c + 1) * V7X_LANES]
        hi = x.astype(BF16)
        r1 = x - hi.astype(F32)
        mid = r1.astype(BF16)
        lo = (r1 - mid.astype(F32)).astype(BF16)
        blk = (jnp.dot(hi, tri, preferred_element_type=F32)
               + jnp.dot(mid, tri, preferred_element_type=F32)
               + jnp.dot(lo, tri, preferred_element_type=F32)) + carry
        o_ref[:, c * V7X_LANES:(c + 1) * V7X_LANES] = blk
        carry = blk[:, V7X_LANES - 1:V7X_LANES]


def _cumsum_lanes(x):
    return pl.pallas_call(
        _cumsum_kernel,
        out_shape=jax.ShapeDtypeStruct(x.shape, F32),
        name="logf_cumsum",
    )(x)


def _rel_bias_kernel(g_ref, raw_ref, band_ref):
    row = lax.broadcasted_iota(jnp.int32, (BAND_TQ, BAND_W), 0)
    col = lax.broadcasted_iota(jnp.int32, (BAND_TQ, BAND_W), 1)
    q_chunk = row // CHUNK
    k_chunk = col // CHUNK - N_PAST_CHUNKS
    in_band = jnp.logical_and(k_chunk >= q_chunk - N_PAST_CHUNKS, k_chunk <= q_chunk)
    for h in range(N_HEADS):
        rows = jnp.broadcast_to(g_ref[h:h + 1, :], (BAND_TQ, BAND_G))
        toeplitz = pltpu.roll(rows, 0, 1, stride=1, stride_axis=0)[:, :BAND_W] * LOG2E
        raw_ref[h] = toeplitz
        band_ref[h] = jnp.where(in_band, toeplitz, NEG)


def _rel_bias_tiles(table):
    far_past = table[:, 2 * REL_CLIP:]
    far_future = table[:, :1]
    near = jnp.flip(table[:, :2 * REL_CLIP], axis=1)
    n_past = A_REACH - REL_CLIP + 1
    n_future = BAND_W - n_past - near.shape[1]
    g = jnp.concatenate([jnp.broadcast_to(far_past, (N_HEADS, n_past)), near,
                         jnp.broadcast_to(far_future, (N_HEADS, n_future)),
                         jnp.broadcast_to(far_past, (N_HEADS, BAND_G - BAND_W))], axis=1)
    shape = jax.ShapeDtypeStruct((N_HEADS, BAND_TQ, BAND_W), F32)
    return pl.pallas_call(_rel_bias_kernel, out_shape=[shape, shape], name="rel_bias")(g.astype(F32))


def _mixers_prompt_kernel(qa_ref, ka_ref, va_ref, qb_ref, kb_ref, vb_ref, bias_ref, f_ref,
                          oa_ref, ob_ref, vpa_ref, vpb_ref):
    s_len = qa_ref.shape[0]
    row = lax.broadcasted_iota(jnp.int32, (FOX_T, FOX_T), 0)
    col = lax.broadcasted_iota(jnp.int32, (FOX_T, FOX_T), 1)
    causal = row >= col
    lanes = [slice(h * HEAD_DIM, (h + 1) * HEAD_DIM) for h in range(MIX_HEADS)]
    f2 = []
    for h, ls in enumerate(lanes):
        _widen_values(va_ref.at[:, ls], vpa_ref.at[h])
        _widen_values(vb_ref.at[:, ls], vpb_ref.at[h])
        f2.append(f_ref[h] * LOG2E)

    def band_logits(h, t):
        q0 = t * BAND_TQ
        k0 = max(q0 - A_REACH, 0)
        n_keys = q0 + BAND_TQ - k0
        s = lax.dot_general(qa_ref[q0:q0 + BAND_TQ, lanes[h]], ka_ref[k0:k0 + n_keys, lanes[h]], _NT,
                            preferred_element_type=F32)
        return [(s * SCALE2 + bias_ref[h, :, BAND_W - n_keys:], vpa_ref[h, k0:k0 + n_keys, :])]

    def fox_logits(h, qi):
        q0 = qi * FOX_T
        q = qb_ref[q0:q0 + FOX_T, lanes[h]]
        s_diag = (lax.dot_general(q, kb_ref[q0:q0 + FOX_T, lanes[h]], _NT, preferred_element_type=F32)
                  * SCALE2 - f2[h][:, q0:q0 + FOX_T])
        parts = [(jnp.where(causal, s_diag, NEG), vpb_ref[h, q0:q0 + FOX_T, :])]
        if qi:
            s_past = (lax.dot_general(q, kb_ref[0:q0, lanes[h]], _NT, preferred_element_type=F32) * SCALE2
                      - f2[h][:, 0:q0])
            parts.append((s_past, vpb_ref[h, 0:q0, :]))
        return parts

    n_band = s_len // BAND_TQ
    n_fox = s_len // FOX_T
    order = []
    for t in range(max(n_band, n_fox)):
        for h in range(MIX_HEADS):
            if t < n_band:
                order.append(("band", h, t))
            if t < n_fox:
                order.append(("fox", h, t))

    def logits(i):
        kind, h, t = order[i]
        return band_logits(h, t) if kind == "band" else fox_logits(h, t)

    def store(i, o):
        kind, h, t = order[i]
        if kind == "band":
            oa_ref[t * BAND_TQ:(t + 1) * BAND_TQ, lanes[h]] = o.astype(BF16)
        else:
            ob_ref[t * FOX_T:(t + 1) * FOX_T, lanes[h]] = o.astype(BF16)

    _attention_tiles(len(order), logits, store)


def _mixers_prompt(qkv, bias, f_rows, batch, s_len):
    t = batch * s_len
    groups = N_HEADS // MIX_HEADS
    width = MIX_HEADS * HEAD_DIM
    head = lambda seg: pl.BlockSpec((s_len, width), lambda b, g: (b, seg * groups + g))
    blocks = 8 * s_len * width * 2 + MIX_HEADS * (BAND_TQ * BAND_W * 4 + 8 * s_len * 4)
    scratch = 2 * MIX_HEADS * s_len * PV_COLS * 2
    out = jax.ShapeDtypeStruct((t, W_MIX), BF16)
    vp = pltpu.VMEM((MIX_HEADS, s_len, PV_COLS), BF16)
    return pl.pallas_call(
        _mixers_prompt_kernel,
        grid=(batch, groups),
        in_specs=[head(s) for s in range(6)] + [
            pl.BlockSpec((MIX_HEADS, BAND_TQ, BAND_W), lambda b, g: (g, 0, 0)),
            pl.BlockSpec((MIX_HEADS, 1, s_len), lambda b, g: (b * groups + g, 0, 0))],
        out_specs=[head(0), head(0)],
        out_shape=[out, out],
        scratch_shapes=[vp, vp],
        compiler_params=pltpu.CompilerParams(
            dimension_semantics=("parallel", "parallel"),
            vmem_limit_bytes=_vmem_limit(blocks, scratch, 8 * FOX_T * s_len * 4)),
        name="mixers_prompt",
    )(qkv, qkv, qkv, qkv, qkv, qkv, bias, f_rows)


def _band_sample_kernel(q_ref, kn_ref, vn_ref, kn32_ref, vn32_ref, ck_ref, cv_ref, bias_ref,
                        o_ref, sk_ref, sv_ref):
    n_cache = ck_ref.shape[1]
    t_new = q_ref.shape[0]
    k_heads = _head_major(ck_ref[0])
    v_heads = _head_major(cv_ref[0])
    for h in range(N_HEADS):
        hs = slice(h * HEAD_DIM, (h + 1) * HEAD_DIM)
        q = q_ref[:, hs]
        kc = k_heads[h].astype(BF16)
        vc = v_heads[h].astype(BF16)
        s1 = (lax.dot_general(q, kc, _NT, preferred_element_type=F32) * SCALE2
              + bias_ref[h, :t_new, :n_cache])
        s2 = (lax.dot_general(q, kn_ref[:, hs], _NT, preferred_element_type=F32) * SCALE2
              + bias_ref[h, :t_new, n_cache:n_cache + t_new])
        o_ref[:, hs] = _softmax_pv([(s1, vc), (s2, vn_ref[:, hs])]).astype(BF16)
    for cache_ref, new_ref, state_ref in ((ck_ref, kn32_ref, sk_ref), (cv_ref, vn32_ref, sv_ref)):
        state_ref[0, :n_cache - t_new] = cache_ref[0, t_new:]
        state_ref[0, n_cache - t_new:] = new_ref[...]


def _band_sample(qkv, ka32, va32, cache_k, cache_v, bias, batch, t_new):
    n_cache = cache_k.shape[1]
    cache_block = (1, n_cache, N_HEADS, HEAD_DIM)
    new_block = (t_new, N_HEADS, HEAD_DIM)
    blocks = 4 * t_new * W_MIX * 2 + 2 * t_new * W_MIX * 4 + 4 * n_cache * W_MIX * 4 + bias.size * 4
    return pl.pallas_call(
        _band_sample_kernel,
        grid=(batch,),
        in_specs=[
            pl.BlockSpec((t_new, W_MIX), lambda b: (b, 0)),
            pl.BlockSpec((t_new, W_MIX), lambda b: (b, 1)),
            pl.BlockSpec((t_new, W_MIX), lambda b: (b, 2)),
            pl.BlockSpec(new_block, lambda b: (b, 0, 0)),
            pl.BlockSpec(new_block, lambda b: (b, 0, 0)),
            pl.BlockSpec(cache_block, lambda b: (b, 0, 0, 0)),
            pl.BlockSpec(cache_block, lambda b: (b, 0, 0, 0)),
            pl.BlockSpec(bias.shape, lambda b: (0, 0, 0)),
        ],
        out_specs=[
            pl.BlockSpec((t_new, W_MIX), lambda b: (b, 0)),
            pl.BlockSpec(cache_block, lambda b: (b, 0, 0, 0)),
            pl.BlockSpec(cache_block, lambda b: (b, 0, 0, 0)),
        ],
        out_shape=[
            jax.ShapeDtypeStruct((batch * t_new, W_MIX), BF16),
            jax.ShapeDtypeStruct(cache_k.shape, F32),
            jax.ShapeDtypeStruct(cache_v.shape, F32),
        ],
        compiler_params=pltpu.CompilerParams(
            dimension_semantics=("parallel",),
            vmem_limit_bytes=_vmem_limit(blocks, 0, 4 * MIB)),
        name="band_sample",
    )(qkv, qkv, qkv, ka32, va32, cache_k, cache_v, bias)


def _fox_sample_kernel(q_ref, kn_ref, vn_ref, ck_ref, cv_ref, f_ref, o_ref):
    n_cache = ck_ref.shape[1]
    t_new = q_ref.shape[0]
    row = lax.broadcasted_iota(jnp.int32, (t_new, t_new), 0)
    col = lax.broadcasted_iota(jnp.int32, (t_new, t_new), 1)
    causal = row >= col
    k_heads = _head_major(ck_ref[0])
    v_heads = _head_major(cv_ref[0])
    for h in range(N_HEADS):
        hs = slice(h * HEAD_DIM, (h + 1) * HEAD_DIM)
        q = q_ref[:, hs]
        f2 = f_ref[0, h:h + 1, :] * LOG2E
        kc = k_heads[h].astype(BF16)
        vc = v_heads[h].astype(BF16)
        s1 = lax.dot_general(q, kc, _NT, preferred_element_type=F32) * SCALE2 - f2[:, :n_cache]
        s2 = (lax.dot_general(q, kn_ref[:, hs], _NT, preferred_element_type=F32) * SCALE2
              - f2[:, n_cache:n_cache + t_new])
        parts = [(s1, vc), (jnp.where(causal, s2, NEG), vn_ref[:, hs])]
        o_ref[:, hs] = _softmax_pv(parts).astype(BF16)


def _fox_sample(qkv, cache_k, cache_v, f_rows, batch, t_new):
    n_cache = cache_k.shape[1]
    f_len = f_rows.shape[-1]
    cache_block = (1, n_cache, N_HEADS, HEAD_DIM)
    blocks = 4 * t_new * W_MIX * 2 + 2 * n_cache * W_MIX * 4 + N_HEADS * f_len * 4
    return pl.pallas_call(
        _fox_sample_kernel,
        grid=(batch,),
        in_specs=[
            pl.BlockSpec((t_new, W_MIX), lambda b: (b, 3)),
            pl.BlockSpec((t_new, W_MIX), lambda b: (b, 4)),
            pl.BlockSpec((t_new, W_MIX), lambda b: (b, 5)),
            pl.BlockSpec(cache_block, lambda b: (b, 0, 0, 0)),
            pl.BlockSpec(cache_block, lambda b: (b, 0, 0, 0)),
            pl.BlockSpec((1, N_HEADS, f_len), lambda b: (b, 0, 0)),
        ],
        out_specs=pl.BlockSpec((t_new, W_MIX), lambda b: (b, 0)),
        out_shape=jax.ShapeDtypeStruct((batch * t_new, W_MIX), BF16),
        compiler_params=pltpu.CompilerParams(
            dimension_semantics=("parallel",),
            vmem_limit_bytes=_vmem_limit(blocks, 0, 8 * MIB)),
        name="fox_sample",
    )(qkv, qkv, qkv, cache_k, cache_v, f_rows)


def _merge_kernel(oa_ref, ob_ref, gate_ref, x_ref, wa_ref, wb_ref, wo_ref, h_ref):
    ya = jnp.dot(oa_ref[...], wa_ref[...], preferred_element_type=F32)
    yb = jnp.dot(ob_ref[...], wb_ref[...], preferred_element_type=F32)
    m = (gate_ref[:, :D_MODEL].astype(F32) * ya + gate_ref[:, D_MODEL:].astype(F32) * yb)
    h_ref[...] = x_ref[...] + jnp.dot(m.astype(BF16), wo_ref[...], preferred_element_type=F32)


def _resident(shape):
    return pl.BlockSpec(shape, lambda *_: (0,) * len(shape), pipeline_mode=pl.Buffered(1))


def _merge(oa, ob, gates, x, w_a, w_b, w_o, *, tm):
    t = x.shape[0]
    blocks = 2 * tm * W_MIX * 2 + tm * 2 * D_MODEL * 2 + 2 * tm * D_MODEL * 4
    weights = (2 * W_MIX * D_MODEL + D_MODEL * D_MODEL) * 2
    return pl.pallas_call(
        _merge_kernel,
        grid=(t // tm,),
        in_specs=[
            pl.BlockSpec((tm, W_MIX), lambda i: (i, 0)),
            pl.BlockSpec((tm, W_MIX), lambda i: (i, 0)),
            pl.BlockSpec((tm, 2 * D_MODEL), lambda i: (i, 0)),
            pl.BlockSpec((tm, D_MODEL), lambda i: (i, 0)),
            _resident((W_MIX, D_MODEL)),
            _resident((W_MIX, D_MODEL)),
            _resident((D_MODEL, D_MODEL)),
        ],
        out_specs=pl.BlockSpec((tm, D_MODEL), lambda i: (i, 0)),
        out_shape=jax.ShapeDtypeStruct((t, D_MODEL), F32),
        compiler_params=pltpu.CompilerParams(
            dimension_semantics=("parallel",),
            vmem_limit_bytes=_vmem_limit(blocks, weights, 4 * tm * D_MODEL * 4)),
        name="merge",
    )(oa, ob, gates, x, w_a, w_b, w_o)


def _ffn_kernel(h_ref, g_ref, wu_ref, wd_ref, o_ref, n_ref):
    tf = wu_ref.shape[1]
    chunks = [slice(c * FFN_CHUNK, (c + 1) * FFN_CHUNK) for c in range(tf // FFN_CHUNK)]

    def hidden(c):
        a = jnp.dot(n_ref[...], wu_ref[:, c], preferred_element_type=F32)
        r = jnp.square(jnp.maximum(a, 0.0)).astype(BF16)
        return jnp.dot(r, wd_ref[c, :], preferred_element_type=F32)

    @pl.when(pl.program_id(1) == 0)
    def _():
        _norm_to_bf16(h_ref, g_ref, n_ref)
        o_ref[...] = h_ref[...] + hidden(chunks[0])

    @pl.when(pl.program_id(1) != 0)
    def _():
        o_ref[...] += hidden(chunks[0])

    for c in chunks[1:]:
        o_ref[...] += hidden(c)


def _ffn(h, g, w_up, w_down, *, tm, tf):
    t = h.shape[0]
    blocks = 2 * tm * D_MODEL * 4 + 2 * D_MODEL * tf * 2
    scratch = tm * D_MODEL * 2
    return pl.pallas_call(
        _ffn_kernel,
        grid=(t // tm, D_FF // tf),
        in_specs=[
            pl.BlockSpec((tm, D_MODEL), lambda i, f: (i, 0)),
            pl.BlockSpec((1, D_MODEL), lambda i, f: (0, 0)),
            pl.BlockSpec((D_MODEL, tf), lambda i, f: (0, f)),
            pl.BlockSpec((tf, D_MODEL), lambda i, f: (f, 0)),
        ],
        out_specs=pl.BlockSpec((tm, D_MODEL), lambda i, f: (i, 0)),
        out_shape=jax.ShapeDtypeStruct((t, D_MODEL), F32),
        scratch_shapes=[pltpu.VMEM((tm, D_MODEL), BF16)],
        compiler_params=pltpu.CompilerParams(
            dimension_semantics=("parallel", "arbitrary"),
            vmem_limit_bytes=_vmem_limit(blocks, scratch, 2 * tm * tf * 4)),
        name="ffn",
    )(h, g, w_up, w_down)


def _ple_kernel(h_ref, p_ref, gp_ref, gf_ref, wg_ref, wp_ref, y_ref):
    h = h_ref[...]
    n = _rmsnorm_rows(h, gp_ref[...]).astype(BF16)
    gate = _sigmoid(jnp.dot(n, wg_ref[...], preferred_element_type=F32))
    proj = jnp.dot(p_ref[...].astype(BF16), wp_ref[...], preferred_element_type=F32)
    y_ref[...] = _rmsnorm_rows(h + proj * gate, gf_ref[...])


def _ple_final(h, p, g_ple, g_final, w_gate, w_proj, *, tm):
    t = h.shape[0]
    blocks = 2 * tm * D_MODEL * 4 + tm * D_PLE * 4
    weights = (D_MODEL * D_MODEL + D_PLE * D_MODEL) * 2
    return pl.pallas_call(
        _ple_kernel,
        grid=(t // tm,),
        in_specs=[
            pl.BlockSpec((tm, D_MODEL), lambda i: (i, 0)),
            pl.BlockSpec((tm, D_PLE), lambda i: (i, 0)),
            pl.BlockSpec((1, D_MODEL), lambda i: (0, 0)),
            pl.BlockSpec((1, D_MODEL), lambda i: (0, 0)),
            _resident((D_MODEL, D_MODEL)),
            _resident((D_PLE, D_MODEL)),
        ],
        out_specs=pl.BlockSpec((tm, D_MODEL), lambda i: (i, 0)),
        out_shape=jax.ShapeDtypeStruct((t, D_MODEL), F32),
        compiler_params=pltpu.CompilerParams(
            dimension_semantics=("parallel",),
            vmem_limit_bytes=_vmem_limit(blocks, weights, 4 * tm * D_MODEL * 4)),
        name="ple_final",
    )(h, p, g_ple, g_final, w_gate, w_proj)


def _pad_lanes(x, multiple):
    pad = (-x.shape[-1]) % multiple
    return jnp.pad(x, ((0, 0),) * (x.ndim - 1) + ((0, pad),)) if pad else x


def _token_tile(t, preferred):
    return preferred if t % preferred == 0 else t


def kernel(x_prompt, x_sample, p_prompt, p_sample, cache_a_k, cache_a_v, cache_b_k, cache_b_v,
           cache_b_logf, g_mix, w_in, b_f, rel_bias, w_a_proj, w_b_proj, w_o, g_mlp, w_up, w_down,
           g_ple, w_ple_gate, w_ple_proj, g_final):
    depth = w_in.shape[0]
    assert depth == 1, "single-layer step"
    batch, s_len, _ = x_prompt.shape
    dec_batch, t_new, _ = x_sample.shape
    n_cache_a = cache_a_k.shape[2]
    assert s_len % FOX_T == 0 and s_len >= A_REACH and t_new == CHUNK and n_cache_a == A_REACH

    w = w_in[0]
    n_qkv = 6 * SEG_COLS
    w_t = jnp.swapaxes(w, 0, 1).astype(BF16)
    w_g = w_t[n_qkv + N_HEADS:]
    w_f = _pad_lanes(w[:, n_qkv:n_qkv + N_HEADS], V7X_LANES).astype(BF16)
    b_f_row = _pad_lanes(b_f[0][None, :].astype(F32), V7X_LANES)
    g_mix_row = g_mix[0][None, :].astype(F32)
    g_mlp_row = g_mlp[0][None, :].astype(F32)
    g_ple_row = g_ple[0][None, :].astype(F32)
    g_final_row = g_final[None, :].astype(F32)
    w_a = w_a_proj[0].astype(BF16)
    w_b = w_b_proj[0].astype(BF16)
    w_o_b = w_o[0].astype(BF16)
    w_up_b = w_up[0].astype(BF16)
    w_down_b = w_down[0].astype(BF16)
    w_pg = w_ple_gate[0].astype(BF16)
    w_pp = w_ple_proj[0].astype(BF16)
    bias_raw, bias_band = _rel_bias_tiles(rel_bias[0])

    def project(x2d, *, a_rows, a_period, tm):
        gates, lf, n = _in_proj_g(x2d, g_mix_row, w_g, w_f, b_f_row, tm=tm)
        return (gates, lf) + tuple(_in_proj_qkv(n, w_t, a_rows=a_rows, a_period=a_period, tm=tm))

    def finish(x2d, p2d, oa, ob, gates):
        t = x2d.shape[0]
        h1 = _merge(oa, ob, gates, x2d, w_a, w_b, w_o_b, tm=_token_tile(t, 512))
        h2 = _ffn(h1, g_mlp_row, w_up_b, w_down_b, tm=_token_tile(t, 1024), tf=1024)
        return _ple_final(h2, p2d, g_ple_row, g_final_row, w_pg, w_pp, tm=_token_tile(t, 512))

    tp = batch * s_len
    xp = x_prompt.reshape(tp, D_MODEL)
    tm_p = _token_tile(s_len, 1024)
    gates_p, lf_p, qkv_p, ka_p, va_p, kb_p, vb_p = project(
        xp, a_rows=min(A_REACH, tm_p), a_period=s_len // tm_p, tm=tm_p)
    lf_rows = lf_p.reshape(batch, s_len, N_HEADS).transpose(0, 2, 1).reshape(batch * N_HEADS, s_len)
    f_p = _cumsum_lanes(lf_rows).reshape(batch * N_HEADS, 1, s_len)
    oa_p, ob_p = _mixers_prompt(qkv_p, bias_band, f_p, batch, s_len)
    y_prompt = finish(xp, p_prompt[0].reshape(tp, D_PLE), oa_p, ob_p, gates_p)

    ts = dec_batch * t_new
    xs = x_sample.reshape(ts, D_MODEL)
    gates_s, lf_s, qkv_s, ka_s, va_s, kb_s, vb_s = project(xs, a_rows=ts, a_period=1, tm=ts)
    lf_new = lf_s.reshape(dec_batch, t_new, N_HEADS)
    lf_all = jnp.concatenate([cache_b_logf[0].astype(F32), lf_new], axis=1)
    lf_all = _pad_lanes(lf_all.transpose(0, 2, 1), V7X_LANES)
    f_s = _cumsum_lanes(lf_all.reshape(dec_batch * N_HEADS, -1)).reshape(dec_batch, N_HEADS, -1)
    oa_s, sk_s, sv_s = _band_sample(qkv_s, ka_s, va_s, cache_a_k[0], cache_a_v[0], bias_raw,
                                    dec_batch, t_new)
    ob_s = _fox_sample(qkv_s, cache_b_k[0], cache_b_v[0], f_s, dec_batch, t_new)
    y_sample = finish(xs, p_sample[0].reshape(ts, D_PLE), oa_s, ob_s, gates_s)

    def state(a, b):
        return a.reshape(1, b, -1, N_HEADS, HEAD_DIM)

    return (y_prompt.reshape(batch, s_len, D_MODEL),
            y_sample.reshape(dec_batch, t_new, D_MODEL),
            state(ka_p, batch), state(va_p, batch), state(kb_p, batch), state(vb_p, batch),
            lf_p.reshape(1, batch, s_len, N_HEADS),
            sk_s[None], sv_s[None],
            state(kb_s, dec_batch), state(vb_s, dec_batch),
            lf_new[None])
```

```python
import functools
import math

import jax
import jax.numpy as jnp
from jax import lax
from jax.experimental import pallas as pl
from jax.experimental.pallas import tpu as pltpu

F32 = jnp.float32
BF16 = jnp.bfloat16

D_MODEL = 2048
CHUNK = 64
N_PAST_CHUNKS = 8
A_REACH = N_PAST_CHUNKS * CHUNK
HEAD_DIM = 128
N_HEADS = 8
W_MIX = N_HEADS * HEAD_DIM
REL_CLIP = 128
D_FF = 4 * D_MODEL
D_PLE = 256
RMS_EPS = 1e-6
SCALE = HEAD_DIM ** -0.5
NEG = -1e30
LOG2E = math.log2(math.e)
SCALE2 = SCALE * LOG2E

V7X_LANES = 128
V7X_VMEM_LIMIT_CAP = 62 * 1024 * 1024
MIB = 1024 * 1024

NORM_ROWS = 128
BAND_TQ = 256
BAND_W = A_REACH + BAND_TQ
BAND_G = BAND_W + BAND_TQ
PV_COLS = 2 * HEAD_DIM
FOX_T = 256
MIX_HEADS = 4
GATE_COLS = 2048
SEG_COLS = 1024
FFN_CHUNK = 1024

_NT = (((1,), (1,)), ((), ()))


def _vmem_limit(block_bytes, scratch_bytes, temp_bytes):
    est = 2 * block_bytes + scratch_bytes + temp_bytes
    return int(min(max(est, 16 * MIB), V7X_VMEM_LIMIT_CAP))


def _sigmoid(x):
    return 0.5 * jnp.tanh(0.5 * x) + 0.5


def _rmsnorm_rows(x, g):
    ms = jnp.mean(x * x, axis=-1, keepdims=True)
    return (x * lax.rsqrt(ms + RMS_EPS)) * g


def _norm_to_bf16(x_ref, g_ref, n_ref):
    rows = min(NORM_ROWS, x_ref.shape[0])

    def body(c, carry):
        r = pl.ds(pl.multiple_of(c * rows, rows), rows)
        n_ref[r, :] = _rmsnorm_rows(x_ref[r, :], g_ref[...]).astype(BF16)
        return carry

    lax.fori_loop(0, x_ref.shape[0] // rows, body, 0)


def _softmax_pv(parts):
    m = functools.reduce(jnp.maximum, [jnp.max(s, axis=-1, keepdims=True) for s, _ in parts])
    ps = [jnp.exp2(s - m) for s, _ in parts]
    l = functools.reduce(jnp.add, [jnp.sum(p, axis=-1, keepdims=True) for p in ps])
    o = functools.reduce(jnp.add, [jnp.dot(p.astype(BF16), v, preferred_element_type=F32)
                                   for p, (_, v) in zip(ps, parts)])
    return o / l


def _widen_values(v_ref, vp_ref):
    lane = lax.broadcasted_iota(jnp.int32, v_ref.shape, 1)
    vp_ref[:, :HEAD_DIM] = v_ref[...]
    vp_ref[:, HEAD_DIM:] = jnp.where(lane == 0, 1.0, 0.0).astype(BF16)


def _softmax_weights(parts):
    m = functools.reduce(jnp.maximum, [jnp.max(s, axis=-1, keepdims=True) for s, _ in parts])
    return [(jnp.exp2(s - m).astype(BF16), v) for s, v in parts]


def _weighted_values_wide(weighted):
    o = functools.reduce(jnp.add, [jnp.dot(p, v, preferred_element_type=F32) for p, v in weighted])
    return o[:, :HEAD_DIM] / o[:, HEAD_DIM:HEAD_DIM + 1]


def _attention_tiles(n_tiles, logits, store):
    stage1 = {0: logits(0)}
    if n_tiles > 1:
        stage1[1] = logits(1)
    stage2 = {0: _softmax_weights(stage1.pop(0))}
    for t in range(n_tiles):
        if t + 2 < n_tiles:
            stage1[t + 2] = logits(t + 2)
        if t + 1 < n_tiles:
            stage2[t + 1] = _softmax_weights(stage1.pop(t + 1))
        store(t, _weighted_values_wide(stage2.pop(t)))


def _store_heads(ref, rows):
    stacked = jnp.stack([rows[:, h * HEAD_DIM:(h + 1) * HEAD_DIM] for h in range(N_HEADS)], axis=0)
    ref[...] = jnp.swapaxes(stacked, 0, 1)


def _head_major(rows):
    return jnp.swapaxes(rows, 0, 1)


def _inproj_g_kernel(x_ref, g_ref, w_ref, wf_ref, bf_ref, gate_ref, lf_ref, n_ref):
    @pl.when(pl.program_id(1) == 0)
    def _():
        _norm_to_bf16(x_ref, g_ref, n_ref)
        z = jnp.dot(n_ref[...], wf_ref[...], preferred_element_type=F32) + bf_ref[...]
        lf = jnp.minimum(z, 0.0) - jnp.log1p(jnp.exp(-jnp.abs(z)))
        lf_ref[...] = lf[:, :N_HEADS]

    a = lax.dot_general(n_ref[...], w_ref[...], _NT, preferred_element_type=F32)
    gate_ref[...] = _sigmoid(a).astype(BF16)


def _in_proj_g(x, g, w_g, w_f, b_f, *, tm):
    t = x.shape[0]
    tn = GATE_COLS
    blocks = (tm * D_MODEL * 4 + D_MODEL * tn * 2 + D_MODEL * V7X_LANES * 2
              + tm * tn * 2 + tm * V7X_LANES * 4 + tm * D_MODEL * 2)
    return pl.pallas_call(
        _inproj_g_kernel,
        grid=(t // tm, 4 * SEG_COLS // GATE_COLS),
        in_specs=[
            pl.BlockSpec((tm, D_MODEL), lambda i, j: (i, 0)),
            pl.BlockSpec((1, D_MODEL), lambda i, j: (0, 0)),
            pl.BlockSpec((tn, D_MODEL), lambda i, j: (j, 0)),
            pl.BlockSpec((D_MODEL, V7X_LANES), lambda i, j: (0, 0)),
            pl.BlockSpec((1, V7X_LANES), lambda i, j: (0, 0)),
        ],
        out_specs=[
            pl.BlockSpec((tm, tn), lambda i, j: (i, j)),
            pl.BlockSpec((tm, N_HEADS), lambda i, j: (i, 0)),
            pl.BlockSpec((tm, D_MODEL), lambda i, j: (i, 0)),
        ],
        out_shape=[
            jax.ShapeDtypeStruct((t, 4 * SEG_COLS), BF16),
            jax.ShapeDtypeStruct((t, N_HEADS), F32),
            jax.ShapeDtypeStruct((t, D_MODEL), BF16),
        ],
        compiler_params=pltpu.CompilerParams(
            dimension_semantics=("parallel", "arbitrary"),
            vmem_limit_bytes=_vmem_limit(blocks, 0, 10 * MIB)),
        name="in_proj_g",
    )(x, g, w_g, w_f, b_f)


def _inproj_qkv_kernel(n_ref, w_ref, qkv_ref, ka_ref, va_ref, kb_ref, vb_ref, *, a_rows, a_period):
    i = pl.program_id(0)
    j = pl.program_id(1)
    tm = n_ref.shape[0]
    is_tail = (i % a_period) == (a_period - 1)
    is_a_kv = jnp.logical_or(j == 1, j == 2)

    def project():
        return lax.dot_general(n_ref[...], w_ref[...], _NT, preferred_element_type=F32)

    plain = jnp.logical_or(jnp.logical_or(j == 0, j == 3),
                           jnp.logical_and(is_a_kv, jnp.logical_not(is_tail)))

    @pl.when(plain)
    def _():
        qkv_ref[...] = project().astype(BF16)

    for col, tail_ref in ((1, ka_ref), (2, va_ref)):
        @pl.when(jnp.logical_and(j == col, is_tail))
        def _(tail_ref=tail_ref):
            r = project()
            qkv_ref[...] = r.astype(BF16)
            _store_heads(tail_ref, r[tm - a_rows:, :])

    for col, full_ref in ((4, kb_ref), (5, vb_ref)):
        @pl.when(j == col)
        def _(full_ref=full_ref):
            r = project()
            qkv_ref[...] = r.astype(BF16)
            _store_heads(full_ref, r)


def _in_proj_qkv(n, w_t, *, a_rows, a_period, tm):
    t = n.shape[0]
    tn = SEG_COLS
    n_a = (t // tm) // a_period * a_rows
    blocks = (tm * D_MODEL * 2 + D_MODEL * tn * 2 + tm * tn * 2
              + 2 * a_rows * tn * 4 + 2 * tm * tn * 4)
    kern = functools.partial(_inproj_qkv_kernel, a_rows=a_rows, a_period=a_period)
    head_block = lambda rows: (rows, N_HEADS, HEAD_DIM)
    return pl.pallas_call(
        kern,
        grid=(t // tm, 6),
        in_specs=[
            pl.BlockSpec((tm, D_MODEL), lambda i, j: (i, 0)),
            pl.BlockSpec((tn, D_MODEL), lambda i, j: (j, 0)),
        ],
        out_specs=[
            pl.BlockSpec((tm, tn), lambda i, j: (i, j)),
            pl.BlockSpec(head_block(a_rows), lambda i, j: (i // a_period, 0, 0)),
            pl.BlockSpec(head_block(a_rows), lambda i, j: (i // a_period, 0, 0)),
            pl.BlockSpec(head_block(tm), lambda i, j: (i, 0, 0)),
            pl.BlockSpec(head_block(tm), lambda i, j: (i, 0, 0)),
        ],
        out_shape=[
            jax.ShapeDtypeStruct((t, 6 * SEG_COLS), BF16),
            jax.ShapeDtypeStruct(head_block(n_a), F32),
            jax.ShapeDtypeStruct(head_block(n_a), F32),
            jax.ShapeDtypeStruct(head_block(t), F32),
            jax.ShapeDtypeStruct(head_block(t), F32),
        ],
        compiler_params=pltpu.CompilerParams(
            dimension_semantics=("parallel", "arbitrary"),
            vmem_limit_bytes=_vmem_limit(blocks, 0, 6 * MIB)),
        name="in_proj_qkv",
    )(n, w_t)


def _cumsum_kernel(x_ref, o_ref):
    rows, length = x_ref.shape
    r_i = lax.broadcasted_iota(jnp.int32, (V7X_LANES, V7X_LANES), 0)
    c_i = lax.broadcasted_iota(jnp.int32, (V7X_LANES, V7X_LANES), 1)
    tri = (r_i <= c_i).astype(BF16)
    carry = jnp.zeros((rows, 1), F32)
    for c in range(length // V7X_LANES):
        x = x_ref[:, c * V7X_LANES:(c + 1) * V7X_LANES]
        hi = x.astype(BF16)
        r1 = x - hi.astype(F32)
        mid = r1.astype(BF16)
        lo = (r1 - mid.astype(F32)).astype(BF16)
        blk = (jnp.dot(hi, tri, preferred_element_type=F32)
               + jnp.dot(mid, tri, preferred_element_type=F32)
               + jnp.dot(lo, tri, preferred_element_type=F32)) + carry
        o_ref[:, c * V7X_LANES:(c + 1) * V7X_LANES] = blk
        carry = blk[:, V7X_LANES - 1:V7X_LANES]


def _cumsum_lanes(x):
    return pl.pallas_call(
        _cumsum_kernel,
        out_shape=jax.ShapeDtypeStruct(x.shape, F32),
        name="logf_cumsum",
    )(x)


def _rel_bias_kernel(g_ref, raw_ref, band_ref):
    row = lax.broadcasted_iota(jnp.int32, (BAND_TQ, BAND_W), 0)
    col = lax.broadcasted_iota(jnp.int32, (BAND_TQ, BAND_W), 1)
    q_chunk = row // CHUNK
    k_chunk = col // CHUNK - N_PAST_CHUNKS
    in_band = jnp.logical_and(k_chunk >= q_chunk - N_PAST_CHUNKS, k_chunk <= q_chunk)
    for h in range(N_HEADS):
        rows = jnp.broadcast_to(g_ref[h:h + 1, :], (BAND_TQ, BAND_G))
        toeplitz = pltpu.roll(rows, 0, 1, stride=1, stride_axis=0)[:, :BAND_W] * LOG2E
        raw_ref[h] = toeplitz
        band_ref[h] = jnp.where(in_band, toeplitz, NEG)


def _rel_bias_tiles(table):
    far_past = table[:, 2 * REL_CLIP:]
    far_future = table[:, :1]
    near = jnp.flip(table[:, :2 * REL_CLIP], axis=1)
    n_past = A_REACH - REL_CLIP + 1
    n_future = BAND_W - n_past - near.shape[1]
    g = jnp.concatenate([jnp.broadcast_to(far_past, (N_HEADS, n_past)), near,
                         jnp.broadcast_to(far_future, (N_HEADS, n_future)),
                         jnp.broadcast_to(far_past, (N_HEADS, BAND_G - BAND_W))], axis=1)
    shape = jax.ShapeDtypeStruct((N_HEADS, BAND_TQ, BAND_W), F32)
    return pl.pallas_call(_rel_bias_kernel, out_shape=[shape, shape], name="rel_bias")(g.astype(F32))


def _mixers_prompt_kernel(qa_ref, ka_ref, va_ref, qb_ref, kb_ref, vb_ref, bias_ref, f_ref,
                          oa_ref, ob_ref, vpa_ref, vpb_ref):
    s_len = qa_ref.shape[0]
    row = lax.broadcasted_iota(jnp.int32, (FOX_T, FOX_T), 0)
    col = lax.broadcasted_iota(jnp.int32, (FOX_T, FOX_T), 1)
    causal = row >= col
    lanes = [slice(h * HEAD_DIM, (h + 1) * HEAD_DIM) for h in range(MIX_HEADS)]
    f2 = []
    for h, ls in enumerate(lanes):
        _widen_values(va_ref.at[:, ls], vpa_ref.at[h])
        _widen_values(vb_ref.at[:, ls], vpb_ref.at[h])
        f2.append(f_ref[h] * LOG2E)

    def band_logits(h, t):
        q0 = t * BAND_TQ
        k0 = max(q0 - A_REACH, 0)
        n_keys = q0 + BAND_TQ - k0
        s = lax.dot_general(qa_ref[q0:q0 + BAND_TQ, lanes[h]], ka_ref[k0:k0 + n_keys, lanes[h]], _NT,
                            preferred_element_type=F32)
        return [(s * SCALE2 + bias_ref[h, :, BAND_W - n_keys:], vpa_ref[h, k0:k0 + n_keys, :])]

    def fox_logits(h, qi):
        q0 = qi * FOX_T
        q = qb_ref[q0:q0 + FOX_T, lanes[h]]
        s_diag = (lax.dot_general(q, kb_ref[q0:q0 + FOX_T, lanes[h]], _NT, preferred_element_type=F32)
                  * SCALE2 - f2[h][:, q0:q0 + FOX_T])
        parts = [(jnp.where(causal, s_diag, NEG), vpb_ref[h, q0:q0 + FOX_T, :])]
        if qi:
            s_past = (lax.dot_general(q, kb_ref[0:q0, lanes[h]], _NT, preferred_element_type=F32) * SCALE2
                      - f2[h][:, 0:q0])
            parts.append((s_past, vpb_ref[h, 0:q0, :]))
        return parts

    n_band = s_len // BAND_TQ
    n_fox = s_len // FOX_T
    order = []
    for t in range(max(n_band, n_fox)):
        for h in range(MIX_HEADS):
            if t < n_band:
                order.append(("band", h, t))
            if t < n_fox:
                order.append(("fox", h, t))

    def logits(i):
        kind, h, t = order[i]
        return band_logits(h, t) if kind == "band" else fox_logits(h, t)

    def store(i, o):
        kind, h, t = order[i]
        if kind == "band":
            oa_ref[t * BAND_TQ:(t + 1) * BAND_TQ, lanes[h]] = o.astype(BF16)
        else:
            ob_ref[t * FOX_T:(t + 1) * FOX_T, lanes[h]] = o.astype(BF16)

    _attention_tiles(len(order), logits, store)


def _mixers_prompt(qkv, bias, f_rows, batch, s_len):
    t = batch * s_len
    groups = N_HEADS // MIX_HEADS
    width = MIX_HEADS * HEAD_DIM
    head = lambda seg: pl.BlockSpec((s_len, width), lambda b, g: (b, seg * groups + g))
    blocks = 8 * s_len * width * 2 + MIX_HEADS * (BAND_TQ * BAND_W * 4 + 8 * s_len * 4)
    scratch = 2 * MIX_HEADS * s_len * PV_COLS * 2
    out = jax.ShapeDtypeStruct((t, W_MIX), BF16)
    vp = pltpu.VMEM((MIX_HEADS, s_len, PV_COLS), BF16)
    return pl.pallas_call(
        _mixers_prompt_kernel,
        grid=(batch, groups),
        in_specs=[head(s) for s in range(6)] + [
            pl.BlockSpec((MIX_HEADS, BAND_TQ, BAND_W), lambda b, g: (g, 0, 0)),
            pl.BlockSpec((MIX_HEADS, 1, s_len), lambda b, g: (b * groups + g, 0, 0))],
        out_specs=[head(0), head(0)],
        out_shape=[out, out],
        scratch_shapes=[vp, vp],
        compiler_params=pltpu.CompilerParams(
            dimension_semantics=("parallel", "parallel"),
            vmem_limit_bytes=_vmem_limit(blocks, scratch, 8 * FOX_T * s_len * 4)),
        name="mixers_prompt",
    )(qkv, qkv, qkv, qkv, qkv, qkv, bias, f_rows)


def _band_sample_kernel(q_ref, kn_ref, vn_ref, kn32_ref, vn32_ref, ck_ref, cv_ref, bias_ref,
                        o_ref, sk_ref, sv_ref):
    n_cache = ck_ref.shape[1]
    t_new = q_ref.shape[0]
    k_heads = _head_major(ck_ref[0])
    v_heads = _head_major(cv_ref[0])
    for h in range(N_HEADS):
        hs = slice(h * HEAD_DIM, (h + 1) * HEAD_DIM)
        q = q_ref[:, hs]
        kc = k_heads[h].astype(BF16)
        vc = v_heads[h].astype(BF16)
        s1 = (lax.dot_general(q, kc, _NT, preferred_element_type=F32) * SCALE2
              + bias_ref[h, :t_new, :n_cache])
        s2 = (lax.dot_general(q, kn_ref[:, hs], _NT, preferred_element_type=F32) * SCALE2
              + bias_ref[h, :t_new, n_cache:n_cache + t_new])
        o_ref[:, hs] = _softmax_pv([(s1, vc), (s2, vn_ref[:, hs])]).astype(BF16)
    for cache_ref, new_ref, state_ref in ((ck_ref, kn32_ref, sk_ref), (cv_ref, vn32_ref, sv_ref)):
        state_ref[0, :n_cache - t_new] = cache_ref[0, t_new:]
        state_ref[0, n_cache - t_new:] = new_ref[...]


def _band_sample(qkv, ka32, va32, cache_k, cache_v, bias, batch, t_new):
    n_cache = cache_k.shape[1]
    cache_block = (1, n_cache, N_HEADS, HEAD_DIM)
    new_block = (t_new, N_HEADS, HEAD_DIM)
    blocks = 4 * t_new * W_MIX * 2 + 2 * t_new * W_MIX * 4 + 4 * n_cache * W_MIX * 4 + bias.size * 4
    return pl.pallas_call(
        _band_sample_kernel,
        grid=(batch,),
        in_specs=[
            pl.BlockSpec((t_new, W_MIX), lambda b: (b, 0)),
            pl.BlockSpec((t_new, W_MIX), lambda b: (b, 1)),
            pl.BlockSpec((t_new, W_MIX), lambda b: (b, 2)),
            pl.BlockSpec(new_block, lambda b: (b, 0, 0)),
            pl.BlockSpec(new_block, lambda b: (b, 0, 0)),
            pl.BlockSpec(cache_block, lambda b: (b, 0, 0, 0)),
            pl.BlockSpec(cache_block, lambda b: (b, 0, 0, 0)),
            pl.BlockSpec(bias.shape, lambda b: (0, 0, 0)),
        ],
        out_specs=[
            pl.BlockSpec((t_new, W_MIX), lambda b: (b, 0)),
            pl.BlockSpec(cache_block, lambda b: (b, 0, 0, 0)),
            pl.BlockSpec(cache_block, lambda b: (b, 0, 0, 0)),
        ],
        out_shape=[
            jax.ShapeDtypeStruct((batch * t_new, W_MIX), BF16),
            jax.ShapeDtypeStruct(cache_k.shape, F32),
            jax.ShapeDtypeStruct(cache_v.shape, F32),
        ],
        compiler_params=pltpu.CompilerParams(
            dimension_semantics=("parallel",),
            vmem_limit_bytes=_vmem_limit(blocks, 0, 4 * MIB)),
        name="band_sample",
    )(qkv, qkv, qkv, ka32, va32, cache_k, cache_v, bias)


def _fox_sample_kernel(q_ref, kn_ref, vn_ref, ck_ref, cv_ref, f_ref, o_ref):
    n_cache = ck_ref.shape[1]
    t_new = q_ref.shape[0]
    row = lax.broadcasted_iota(jnp.int32, (t_new, t_new), 0)
    col = lax.broadcasted_iota(jnp.int32, (t_new, t_new), 1)
    causal = row >= col
    k_heads = _head_major(ck_ref[0])
    v_heads = _head_major(cv_ref[0])
    for h in range(N_HEADS):
        hs = slice(h * HEAD_DIM, (h + 1) * HEAD_DIM)
        q = q_ref[:, hs]
        f2 = f_ref[0, h:h + 1, :] * LOG2E
        kc = k_heads[h].astype(BF16)
        vc = v_heads[h].astype(BF16)
        s1 = lax.dot_general(q, kc, _NT, preferred_element_type=F32) * SCALE2 - f2[:, :n_cache]
        s2 = (lax.dot_general(q, kn_ref[:, hs], _NT, preferred_element_type=F32) * SCALE2
              - f2[:, n_cache:n_cache + t_new])
        parts = [(s1, vc), (jnp.where(causal, s2, NEG), vn_ref[:, hs])]
        o_ref[:, hs] = _softmax_pv(parts).astype(BF16)


def _fox_sample(qkv, cache_k, cache_v, f_rows, batch, t_new):
    n_cache = cache_k.shape[1]
    f_len = f_rows.shape[-1]
    cache_block = (1, n_cache, N_HEADS, HEAD_DIM)
    blocks = 4 * t_new * W_MIX * 2 + 2 * n_cache * W_MIX * 4 + N_HEADS * f_len * 4
    return pl.pallas_call(
        _fox_sample_kernel,
        grid=(batch,),
        in_specs=[
            pl.BlockSpec((t_new, W_MIX), lambda b: (b, 3)),
            pl.BlockSpec((t_new, W_MIX), lambda b: (b, 4)),
            pl.BlockSpec((t_new, W_MIX), lambda b: (b, 5)),
            pl.BlockSpec(cache_block, lambda b: (b, 0, 0, 0)),
            pl.BlockSpec(cache_block, lambda b: (b, 0, 0, 0)),
            pl.BlockSpec((1, N_HEADS, f_len), lambda b: (b, 0, 0)),
        ],
        out_specs=pl.BlockSpec((t_new, W_MIX), lambda b: (b, 0)),
        out_shape=jax.ShapeDtypeStruct((batch * t_new, W_MIX), BF16),
        compiler_params=pltpu.CompilerParams(
            dimension_semantics=("parallel",),
            vmem_limit_bytes=_vmem_limit(blocks, 0, 8 * MIB)),
        name="fox_sample",
    )(qkv, qkv, qkv, cache_k, cache_v, f_rows)


def _merge_kernel(oa_ref, ob_ref, gate_ref, x_ref, wa_ref, wb_ref, wo_ref, h_ref):
    ya = jnp.dot(oa_ref[...], wa_ref[...], preferred_element_type=F32)
    yb = jnp.dot(ob_ref[...], wb_ref[...], preferred_element_type=F32)
    m = (gate_ref[:, :D_MODEL].astype(F32) * ya + gate_ref[:, D_MODEL:].astype(F32) * yb)
    h_ref[...] = x_ref[...] + jnp.dot(m.astype(BF16), wo_ref[...], preferred_element_type=F32)


def _resident(shape):
    return pl.BlockSpec(shape, lambda *_: (0,) * len(shape), pipeline_mode=pl.Buffered(1))


def _merge(oa, ob, gates, x, w_a, w_b, w_o, *, tm):
    t = x.shape[0]
    blocks = 2 * tm * W_MIX * 2 + tm * 2 * D_MODEL * 2 + 2 * tm * D_MODEL * 4
    weights = (2 * W_MIX * D_MODEL + D_MODEL * D_MODEL) * 2
    return pl.pallas_call(
        _merge_kernel,
        grid=(t // tm,),
        in_specs=[
            pl.BlockSpec((tm, W_MIX), lambda i: (i, 0)),
            pl.BlockSpec((tm, W_MIX), lambda i: (i, 0)),
            pl.BlockSpec((tm, 2 * D_MODEL), lambda i: (i, 0)),
            pl.BlockSpec((tm, D_MODEL), lambda i: (i, 0)),
            _resident((W_MIX, D_MODEL)),
            _resident((W_MIX, D_MODEL)),
            _resident((D_MODEL, D_MODEL)),
        ],
        out_specs=pl.BlockSpec((tm, D_MODEL), lambda i: (i, 0)),
        out_shape=jax.ShapeDtypeStruct((t, D_MODEL), F32),
        compiler_params=pltpu.CompilerParams(
            dimension_semantics=("parallel",),
            vmem_limit_bytes=_vmem_limit(blocks, weights, 4 * tm * D_MODEL * 4)),
        name="merge",
    )(oa, ob, gates, x, w_a, w_b, w_o)


def _ffn_kernel(h_ref, g_ref, wu_ref, wd_ref, o_ref, n_ref):
    tf = wu_ref.shape[1]
    chunks = [slice(c * FFN_CHUNK, (c + 1) * FFN_CHUNK) for c in range(tf // FFN_CHUNK)]

    def hidden(c):
        a = jnp.dot(n_ref[...], wu_ref[:, c], preferred_element_type=F32)
        r = jnp.square(jnp.maximum(a, 0.0)).astype(BF16)
        return jnp.dot(r, wd_ref[c, :], preferred_element_type=F32)

    @pl.when(pl.program_id(1) == 0)
    def _():
        _norm_to_bf16(h_ref, g_ref, n_ref)
        o_ref[...] = h_ref[...] + hidden(chunks[0])

    @pl.when(pl.program_id(1) != 0)
    def _():
        o_ref[...] += hidden(chunks[0])

    for c in chunks[1:]:
        o_ref[...] += hidden(c)


def _ffn(h, g, w_up, w_down, *, tm, tf):
    t = h.shape[0]
    blocks = 2 * tm * D_MODEL * 4 + 2 * D_MODEL * tf * 2
    scratch = tm * D_MODEL * 2
    return pl.pallas_call(
        _ffn_kernel,
        grid=(t // tm, D_FF // tf),
        in_specs=[
            pl.BlockSpec((tm, D_MODEL), lambda i, f: (i, 0)),
            pl.BlockSpec((1, D_MODEL), lambda i, f: (0, 0)),
            pl.BlockSpec((D_MODEL, tf), lambda i, f: (0, f)),
            pl.BlockSpec((tf, D_MODEL), lambda i, f: (f, 0)),
        ],
        out_specs=pl.BlockSpec((tm, D_MODEL), lambda i, f: (i, 0)),
        out_shape=jax.ShapeDtypeStruct((t, D_MODEL), F32),
        scratch_shapes=[pltpu.VMEM((tm, D_MODEL), BF16)],
        compiler_params=pltpu.CompilerParams(
            dimension_semantics=("parallel", "arbitrary"),
            vmem_limit_bytes=_vmem_limit(blocks, scratch, 2 * tm * tf * 4)),
        name="ffn",
    )(h, g, w_up, w_down)


def _ple_kernel(h_ref, p_ref, gp_ref, gf_ref, wg_ref, wp_ref, y_ref):
    h = h_ref[...]
    n = _rmsnorm_rows(h, gp_ref[...]).astype(BF16)
    gate = _sigmoid(jnp.dot(n, wg_ref[...], preferred_element_type=F32))
    proj = jnp.dot(p_ref[...].astype(BF16), wp_ref[...], preferred_element_type=F32)
    y_ref[...] = _rmsnorm_rows(h + proj * gate, gf_ref[...])


def _ple_final(h, p, g_ple, g_final, w_gate, w_proj, *, tm):
    t = h.shape[0]
    blocks = 2 * tm * D_MODEL * 4 + tm * D_PLE * 4
    weights = (D_MODEL * D_MODEL + D_PLE * D_MODEL) * 2
    return pl.pallas_call(
        _ple_kernel,
        grid=(t // tm,),
        in_specs=[
            pl.BlockSpec((tm, D_MODEL), lambda i: (i, 0)),
            pl.BlockSpec((tm, D_PLE), lambda i: (i, 0)),
            pl.BlockSpec((1, D_MODEL), lambda i: (0, 0)),
            pl.BlockSpec((1, D_MODEL), lambda i: (0, 0)),
            _resident((D_MODEL, D_MODEL)),
            _resident((D_PLE, D_MODEL)),
        ],
        out_specs=pl.BlockSpec((tm, D_MODEL), lambda i: (i, 0)),
        out_shape=jax.ShapeDtypeStruct((t, D_MODEL), F32),
        compiler_params=pltpu.CompilerParams(
            dimension_semantics=("parallel",),
            vmem_limit_bytes=_vmem_limit(blocks, weights, 4 * tm * D_MODEL * 4)),
        name="ple_final",
    )(h, p, g_ple, g_final, w_gate, w_proj)


def _pad_lanes(x, multiple):
    pad = (-x.shape[-1]) % multiple
    return jnp.pad(x, ((0, 0),) * (x.ndim - 1) + ((0, pad),)) if pad else x


def _token_tile(t, preferred):
    return preferred if t % preferred == 0 else t


def kernel(x_prompt, x_sample, p_prompt, p_sample, cache_a_k, cache_a_v, cache_b_k, cache_b_v,
           cache_b_logf, g_mix, w_in, b_f, rel_bias, w_a_proj, w_b_proj, w_o, g_mlp, w_up, w_down,
           g_ple, w_ple_gate, w_ple_proj, g_final):
    depth = w_in.shape[0]
    assert depth == 1, "single-layer step"
    batch, s_len, _ = x_prompt.shape
    dec_batch, t_new, _ = x_sample.shape
    n_cache_a = cache_a_k.shape[2]
    assert s_len % FOX_T == 0 and s_len >= A_REACH and t_new == CHUNK and n_cache_a == A_REACH

    w = w_in[0]
    n_qkv = 6 * SEG_COLS
    w_t = jnp.swapaxes(w, 0, 1).astype(BF16)
    w_g = w_t[n_qkv + N_HEADS:]
    w_f = _pad_lanes(w[:, n_qkv:n_qkv + N_HEADS], V7X_LANES).astype(BF16)
    b_f_row = _pad_lanes(b_f[0][None, :].astype(F32), V7X_LANES)
    g_mix_row = g_mix[0][None, :].astype(F32)
    g_mlp_row = g_mlp[0][None, :].astype(F32)
    g_ple_row = g_ple[0][None, :].astype(F32)
    g_final_row = g_final[None, :].astype(F32)
    w_a = w_a_proj[0].astype(BF16)
    w_b = w_b_proj[0].astype(BF16)
    w_o_b = w_o[0].astype(BF16)
    w_up_b = w_up[0].astype(BF16)
    w_down_b = w_down[0].astype(BF16)
    w_pg = w_ple_gate[0].astype(BF16)
    w_pp = w_ple_proj[0].astype(BF16)
    bias_raw, bias_band = _rel_bias_tiles(rel_bias[0])

    def project(x2d, *, a_rows, a_period, tm):
        gates, lf, n = _in_proj_g(x2d, g_mix_row, w_g, w_f, b_f_row, tm=tm)
        return (gates, lf) + tuple(_in_proj_qkv(n, w_t, a_rows=a_rows, a_period=a_period, tm=tm))

    def finish(x2d, p2d, oa, ob, gates):
        t = x2d.shape[0]
        h1 = _merge(oa, ob, gates, x2d, w_a, w_b, w_o_b, tm=_token_tile(t, 512))
        h2 = _ffn(h1, g_mlp_row, w_up_b, w_down_b, tm=_token_tile(t, 1024), tf=1024)
        return _ple_final(h2, p2d, g_ple_row, g_final_row, w_pg, w_pp, tm=_token_tile(t, 512))

    tp = batch * s_len
    xp = x_prompt.reshape(tp, D_MODEL)
    tm_p = _token_tile(s_len, 1024)
    gates_p, lf_p, qkv_p, ka_p, va_p, kb_p, vb_p = project(
        xp, a_rows=min(A_REACH, tm_p), a_period=s_len // tm_p, tm=tm_p)
    lf_rows = lf_p.reshape(batch, s_len, N_HEADS).transpose(0, 2, 1).reshape(batch * N_HEADS, s_len)
    f_p = _cumsum_lanes(lf_rows).reshape(batch * N_HEADS, 1, s_len)
    oa_p, ob_p = _mixers_prompt(qkv_p, bias_band, f_p, batch, s_len)
    y_prompt = finish(xp, p_prompt[0].reshape(tp, D_PLE), oa_p, ob_p, gates_p)

    ts = dec_batch * t_new
    xs = x_sample.reshape(ts, D_MODEL)
    gates_s, lf_s, qkv_s, ka_s, va_s, kb_s, vb_s = project(xs, a_rows=ts, a_period=1, tm=ts)
    lf_new = lf_s.reshape(dec_batch, t_new, N_HEADS)
    lf_all = jnp.concatenate([cache_b_logf[0].astype(F32), lf_new], axis=1)
    lf_all = _pad_lanes(lf_all.transpose(0, 2, 1), V7X_LANES)
    f_s = _cumsum_lanes(lf_all.reshape(dec_batch * N_HEADS, -1)).reshape(dec_batch, N_HEADS, -1)
    oa_s, sk_s, sv_s = _band_sample(qkv_s, ka_s, va_s, cache_a_k[0], cache_a_v[0], bias_raw,
                                    dec_batch, t_new)
    ob_s = _fox_sample(qkv_s, cache_b_k[0], cache_b_v[0], f_s, dec_batch, t_new)
    y_sample = finish(xs, p_sample[0].reshape(ts, D_PLE), oa_s, ob_s, gates_s)

    def state(a, b):
        return a.reshape(1, b, -1, N_HEADS, HEAD_DIM)

    return (y_prompt.reshape(batch, s_len, D_MODEL),
            y_sample.reshape(dec_batch, t_new, D_MODEL),
            state(ka_p, batch), state(va_p, batch), state(kb_p, batch), state(vb_p, batch),
            lf_p.reshape(1, batch, s_len, N_HEADS),
            sk_s[None], sv_s[None],
            state(kb_s, dec_batch), state(vb_s, dec_batch),
            lf_new[None])
```
